```python
import jax, jax.numpy as jnp
from jax import lax
import numpy as np

D_MODEL = 1024
BATCH = 8
SEQ = 2048
DEPTH = 2
DEC_BATCH = 128
DEC_SEQ = 4
PAST_LEN = 16384
PAGE_SIZE = 128

GROUP_DIM = 64
W_A = 6 * GROUP_DIM
W_B = 4 * GROUP_DIM
W_C = 6 * GROUP_DIM
D_MIX = W_A + W_B + W_C
D_IN = 2 * W_A + W_B + 3 * W_C
CONV_A_WIDTH = 31
CONV_A_HIST = CONV_A_WIDTH - 1
POOL_WINDOWS = (2, 4, 8, 16)
POOL_GROUP = W_B // len(POOL_WINDOWS)
POOL_HIST = max(POOL_WINDOWS) - 1
CONV_C_WIDTH = 3
CONV_C_HIST = CONV_C_WIDTH - 1
N_EXPERT_GROUPS = 4
EXPERTS_PER_GROUP = 4
N_EXPERTS = N_EXPERT_GROUPS * EXPERTS_PER_GROUP
TOP_K = 2
D_EXPERT = 512
LN_EPS = 1e-5
DEEPNORM_ALPHA = (2 * DEPTH) ** 0.25
DEEPNORM_BETA = (8 * DEPTH) ** -0.25

kernel_name = 'hybrid_conv_pool_shortconv_hiermoe_step'


def layer_norm(x, g, b):
    xf = x.astype(jnp.float32)
    mu = jnp.mean(xf, axis=-1, keepdims=True)
    var = jnp.mean(jnp.square(xf - mu), axis=-1, keepdims=True)
    return ((xf - mu) * lax.rsqrt(var + LN_EPS) * g.astype(jnp.float32) + b.astype(jnp.float32)).astype(x.dtype)


def causal_dwconv(u_ext, w):
    C = u_ext.shape[-1]
    return lax.conv_general_dilated(u_ext, w[:, None, :].astype(u_ext.dtype), window_strides=(1,), padding='VALID', dimension_numbers=('NWC', 'WIO', 'NWC'), feature_group_count=C)


def multiscale_pool(u_ext, pos0, w_pool):
    Bsz, L, C = u_ext.shape
    T = L - POOL_HIST
    uf = u_ext.astype(jnp.float32)
    cs = jnp.concatenate([jnp.zeros((Bsz, 1, C), jnp.float32), jnp.cumsum(uf, axis=1)], axis=1)
    pos = pos0 + jnp.arange(T)
    cur = uf[:, POOL_HIST:]
    outs = []
    for g, w in enumerate(POOL_WINDOWS):
        lo, hi = g * POOL_GROUP, (g + 1) * POOL_GROUP
        wsum = cs[:, POOL_HIST + 1:, lo:hi] - cs[:, POOL_HIST + 1 - w:POOL_HIST + 1 - w + T, lo:hi]
        cnt = jnp.minimum(pos + 1, w).astype(jnp.float32)[None, :, None]
        d = wsum / cnt - cur[:, :, lo:hi]
        outs.append(jnp.einsum('btc,ce->bte', d, w_pool[g].astype(jnp.float32)))
    return jnp.concatenate(outs, axis=-1).astype(u_ext.dtype)


def mixer(x, st_a, st_b, st_c, pos0, w_in, b_in, conv_a_w, conv_a_b, ln_a_g, ln_a_b, w_pool, pool_scale, conv_c_w, w_out, b_out):
    proj = jnp.einsum('btd,de->bte', x, w_in) + b_in
    s = (W_A, 2 * W_A, 2 * W_A + W_B, 2 * W_A + W_B + W_C, 2 * W_A + W_B + 2 * W_C)
    a_val, a_gate, u_b, c_bg, c_cg, c_h = jnp.split(proj, s, axis=-1)
    a_in = a_val * jax.nn.sigmoid(a_gate)
    a_ext = jnp.concatenate([st_a.astype(a_in.dtype), a_in], axis=1)
    a = causal_dwconv(a_ext, conv_a_w) + conv_a_b
    a = jax.nn.silu(layer_norm(a, ln_a_g, ln_a_b))
    b_ext = jnp.concatenate([st_b.astype(u_b.dtype), u_b], axis=1)
    bo = multiscale_pool(b_ext, pos0, w_pool) * pool_scale
    c_in = c_cg * c_h
    c_ext = jnp.concatenate([st_c.astype(c_in.dtype), c_in], axis=1)
    co = c_bg * causal_dwconv(c_ext, conv_c_w)
    y = jnp.einsum('bte,ed->btd', jnp.concatenate([a, bo, co], axis=-1), w_out) + b_out
    return (y, a_ext[:, -CONV_A_HIST:], b_ext[:, -POOL_HIST:], c_ext[:, -CONV_C_HIST:])


def hier_moe(x, w_rg, b_rg, w_re, b_re, w_gate, w_up, w_down):
    Bsz, T, D = x.shape
    xt = x.reshape(Bsz * T, D)
    N = xt.shape[0]
    g_logits = (xt @ w_rg + b_rg).astype(jnp.float32)
    g_prob = jax.nn.softmax(g_logits, axis=-1)
    g_idx = jnp.argmax(g_logits, axis=-1)
    g_p = jnp.take_along_axis(g_prob, g_idx[:, None], axis=-1)
    e_logits = (xt @ w_re + b_re).astype(jnp.float32).reshape(N, N_EXPERT_GROUPS, EXPERTS_PER_GROUP)
    e_sel = jnp.take_along_axis(e_logits, g_idx[:, None, None], axis=1)[:, 0]
    top_v, top_i = lax.top_k(e_sel, TOP_K)
    top_w = jax.nn.softmax(top_v, axis=-1) * g_p
    expert_id = g_idx[:, None] * EXPERTS_PER_GROUP + top_i
    gate = jnp.sum(jax.nn.one_hot(expert_id, N_EXPERTS, dtype=jnp.float32) * top_w[..., None], axis=1).astype(x.dtype)
    y = jnp.zeros_like(xt)
    for e in range(N_EXPERTS):
        h = jax.nn.silu(xt @ w_gate[e]) * (xt @ w_up[e])
        y = y + gate[:, e:e + 1] * (h @ w_down[e])
    return y.reshape(Bsz, T, D)


def trunk(x, st_a, st_b, st_c, pos0, w_in, b_in, conv_a_w, conv_a_b, ln_a_g, ln_a_b, w_pool, pool_scale, conv_c_w, w_out, b_out, ln1_g, ln1_b, w_rg, b_rg, w_re, b_re, w_gate, w_up, w_down, ln2_g, ln2_b):
    na, nb, nc = [], [], []
    for l in range(DEPTH):
        m, sa, sb, sc = mixer(x, st_a[l], st_b[l], st_c[l], pos0, w_in[l], b_in[l], conv_a_w[l], conv_a_b[l], ln_a_g[l], ln_a_b[l], w_pool[l], pool_scale[l], conv_c_w[l], w_out[l], b_out[l])
        x = layer_norm(DEEPNORM_ALPHA * x + m, ln1_g[l], ln1_b[l])
        x = layer_norm(DEEPNORM_ALPHA * x + hier_moe(x, w_rg[l], b_rg[l], w_re[l], b_re[l], w_gate[l], w_up[l], w_down[l]), ln2_g[l], ln2_b[l])
        na.append(sa)
        nb.append(sb)
        nc.append(sc)
    return (x, jnp.stack(na), jnp.stack(nb), jnp.stack(nc))


def setup_inputs(seed: int = 0) -> dict:
    key = jax.random.key(seed)
    ks = jax.random.split(key, 32)
    f = jnp.float32
    nrm = lambda k, shape, s: jax.random.normal(k, shape, f) * s
    L = DEPTH
    return {
        'x_prompt': nrm(ks[0], (BATCH, SEQ, D_MODEL), 1.0),
        'x_sample': nrm(ks[1], (DEC_BATCH, DEC_SEQ, D_MODEL), 1.0),
        'state_conv_a': nrm(ks[2], (L, DEC_BATCH, CONV_A_HIST, W_A), 0.5),
        'state_pool_b': nrm(ks[3], (L, DEC_BATCH, POOL_HIST, W_B), 1.0),
        'state_conv_c': nrm(ks[4], (L, DEC_BATCH, CONV_C_HIST, W_C), 0.5),
        'w_in': nrm(ks[5], (L, D_MODEL, D_IN), D_MODEL ** -0.5),
        'b_in': nrm(ks[6], (L, D_IN), 0.02),
        'conv_a_w': nrm(ks[7], (L, CONV_A_WIDTH, W_A), CONV_A_WIDTH ** -0.5),
        'conv_a_b': nrm(ks[8], (L, W_A), 0.02),
        'ln_a_g': 1.0 + nrm(ks[9], (L, W_A), 0.02),
        'ln_a_b': nrm(ks[10], (L, W_A), 0.02),
        'w_pool': nrm(ks[11], (L, len(POOL_WINDOWS), POOL_GROUP, POOL_GROUP), POOL_GROUP ** -0.5),
        'pool_scale': 1.0 + nrm(ks[12], (L, W_B), 0.1),
        'conv_c_w': nrm(ks[13], (L, CONV_C_WIDTH, W_C), CONV_C_WIDTH ** -0.5),
        'w_out': nrm(ks[14], (L, D_MIX, D_MODEL), DEEPNORM_BETA * D_MIX ** -0.5),
        'b_out': nrm(ks[15], (L, D_MODEL), 0.02),
        'ln1_g': 1.0 + nrm(ks[16], (L, D_MODEL), 0.02),
        'ln1_b': nrm(ks[17], (L, D_MODEL), 0.02),
        'w_router_group': nrm(ks[18], (L, D_MODEL, N_EXPERT_GROUPS), D_MODEL ** -0.5),
        'b_router_group': nrm(ks[19], (L, N_EXPERT_GROUPS), 0.01),
        'w_router_expert': nrm(ks[20], (L, D_MODEL, N_EXPERTS), D_MODEL ** -0.5),
        'b_router_expert': nrm(ks[21], (L, N_EXPERTS), 0.01),
        'w_gate': nrm(ks[22], (L, N_EXPERTS, D_MODEL, D_EXPERT), D_MODEL ** -0.5),
        'w_up': nrm(ks[23], (L, N_EXPERTS, D_MODEL, D_EXPERT), D_MODEL ** -0.5),
        'w_down': nrm(ks[24], (L, N_EXPERTS, D_EXPERT, D_MODEL), DEEPNORM_BETA * D_EXPERT ** -0.5),
        'ln2_g': 1.0 + nrm(ks[25], (L, D_MODEL), 0.02),
        'ln2_b': nrm(ks[26], (L, D_MODEL), 0.02),
    }


def reference(x_prompt, x_sample, state_conv_a, state_pool_b, state_conv_c, w_in, b_in, conv_a_w, conv_a_b, ln_a_g, ln_a_b, w_pool, pool_scale, conv_c_w, w_out, b_out, ln1_g, ln1_b, w_router_group, b_router_group, w_router_expert, b_router_expert, w_gate, w_up, w_down, ln2_g, ln2_b):
    params = (w_in, b_in, conv_a_w, conv_a_b, ln_a_g, ln_a_b, w_pool, pool_scale, conv_c_w, w_out, b_out, ln1_g, ln1_b, w_router_group, b_router_group, w_router_expert, b_router_expert, w_gate, w_up, w_down, ln2_g, ln2_b)
    dt = x_prompt.dtype
    za = jnp.zeros((DEPTH, BATCH, CONV_A_HIST, W_A), dt)
    zb = jnp.zeros((DEPTH, BATCH, POOL_HIST, W_B), dt)
    zc = jnp.zeros((DEPTH, BATCH, CONV_C_HIST, W_C), dt)
    y_prompt, pa, pb, pc = trunk(x_prompt, za, zb, zc, 0, *params)
    y_sample, sa, sb, sc = trunk(x_sample, state_conv_a, state_pool_b, state_conv_c, PAST_LEN, *params)
    return (y_prompt, y_sample, pa, sa, pb, sb, pc, sc)
```

```python
import functools

import jax
import jax.numpy as jnp
from jax import lax
from jax.experimental import pallas as pl
from jax.experimental.pallas import tpu as pltpu

F32 = jnp.float32
BF16 = jnp.bfloat16
I32 = jnp.int32

POOL_WINDOWS = (2, 4, 8, 16)
TOP_K = 2
LN_EPS = 1e-5
PAST_LEN = 16384

LANES = 128
SUBLANES = 8
VMEM_LIMIT_BYTES = 56 * 1024 * 1024

ROW_CHUNK = 32
PROMPT_TILE = 256
MOE_TILE = 256
COMBINE_TILE = 256
ROUTER_GROUP_ROW = 0
ROUTER_EXPERT_ROW = 8


def _dot(a, b):
    return jnp.dot(a, b, preferred_element_type=F32)


def _layer_norm(x, g, b):
    mu = jnp.mean(x, axis=-1, keepdims=True)
    xc = x - mu
    var = jnp.mean(xc * xc, axis=-1, keepdims=True)
    return xc * lax.rsqrt(var + LN_EPS) * g + b


def _sigmoid(x):
    return 1.0 / (1.0 + jnp.exp(-x))


def _mixer_rows(x_ref, p, bufs, outs, *, rows, stride, new_a, new_b, new_c, pos_of_row,
                alpha, n_groups, n_experts):
    (w_in, b_in, wa, ba, lnag, lnab, wbd, pscale, wc, w_out, b_out, ln1g, ln1b,
     wr, br) = p
    proj, buf_a, buf_b, buf_c, dbuf, cat = bufs
    x1_ref, eidx_ref, wts_ref = outs
    ka = wa.shape[0]
    kc = wc.shape[0]
    w_a = wa.shape[1]
    w_b = pscale.shape[1]
    w_c = wc.shape[1]
    o_gate, o_ub, o_cbg, o_ccg, o_ch = w_a, 2 * w_a, 2 * w_a + w_b, 2 * w_a + w_b + w_c, 2 * w_a + w_b + 2 * w_c

    x = x_ref[...]
    proj[...] = _dot(x.astype(BF16), w_in[...]) + b_in[...]

    buf_a[pl.ds(new_a, rows), :] = proj[:, 0:w_a] * _sigmoid(proj[:, o_gate:o_gate + w_a])
    buf_b[pl.ds(new_b, rows), :] = proj[:, o_ub:o_ub + w_b]
    buf_c[pl.ds(new_c, rows), :] = proj[:, o_ccg:o_ccg + w_c] * proj[:, o_ch:o_ch + w_c]

    lane_b = lax.broadcasted_iota(I32, (ROW_CHUNK, w_b), 1)
    group_b = lane_b // (w_b // len(POOL_WINDOWS))
    win_b = jnp.zeros((ROW_CHUNK, w_b), I32)
    for g, w in enumerate(POOL_WINDOWS):
        win_b = jnp.where(group_b == g, w, win_b)

    def chunk(c, carry):
        r0 = c * ROW_CHUNK if isinstance(c, int) else pl.multiple_of(c * ROW_CHUNK, ROW_CHUNK)
        acc = jnp.zeros((ROW_CHUNK, w_a), F32) + ba[...]
        for k in range(ka):
            src = buf_a[pl.ds(r0 + (new_a - (ka - 1 - k) * stride), ROW_CHUNK), :]
            acc = acc + wa[k:k + 1, :] * src
        a = _layer_norm(acc, lnag[...], lnab[...])
        cat[pl.ds(r0, ROW_CHUNK), 0:w_a] = (a * _sigmoid(a)).astype(BF16)
        cur = buf_b[pl.ds(r0 + new_b, ROW_CHUNK), :]
        run = cur
        wsum = jnp.zeros((ROW_CHUNK, w_b), F32)
        for j in range(1, max(POOL_WINDOWS) + 1):
            if j in POOL_WINDOWS:
                wsum = jnp.where(win_b == j, run, wsum)
            if j < max(POOL_WINDOWS):
                run = run + buf_b[pl.ds(r0 + (new_b - j * stride), ROW_CHUNK), :]
        pos = pos_of_row(r0 + lax.broadcasted_iota(I32, (ROW_CHUNK, w_b), 0))
        cnt = jnp.minimum(pos + 1, win_b).astype(F32)
        dbuf[pl.ds(r0, ROW_CHUNK), :] = (wsum / cnt - cur).astype(BF16)
        accc = jnp.zeros((ROW_CHUNK, w_c), F32)
        for k in range(kc):
            src = buf_c[pl.ds(r0 + (new_c - (kc - 1 - k) * stride), ROW_CHUNK), :]
            accc = accc + wc[k:k + 1, :] * src
        cbg = proj[pl.ds(r0, ROW_CHUNK), o_cbg:o_cbg + w_c]
        cat[pl.ds(r0, ROW_CHUNK), w_a + w_b:w_a + w_b + w_c] = (cbg * accc).astype(BF16)
        return carry

    if stride % SUBLANES == 0:
        lax.fori_loop(0, rows // ROW_CHUNK, chunk, 0)
    else:
        for c in range(rows // ROW_CHUNK):
            chunk(c, 0)

    cat[:, w_a:w_a + w_b] = (_dot(dbuf[...], wbd[...]) * pscale[...]).astype(BF16)

    m = _dot(cat[...], w_out[...]) + b_out[...]
    x1 = _layer_norm(alpha * x + m, ln1g[...], ln1b[...])
    x1_ref[...] = x1

    x1_hi = x1.astype(BF16)
    x1_lo = (x1 - x1_hi.astype(F32)).astype(BF16)
    p_hi = _dot(x1_hi, wr[...])
    p_lo = _dot(x1_lo, wr[...])
    logits = (p_hi[:, 0:LANES] + p_hi[:, LANES:2 * LANES]) + (p_lo[:, 0:LANES] + p_lo[:, LANES:2 * LANES])
    lt = jnp.transpose(logits + br[...])

    neg = jnp.float32(-jnp.inf)
    gl = lt[ROUTER_GROUP_ROW:ROUTER_GROUP_ROW + SUBLANES, :]
    grow = lax.broadcasted_iota(I32, gl.shape, 0)
    gvalid = grow < n_groups
    glm = jnp.where(gvalid, gl, neg)
    gmax = jnp.max(glm, axis=0, keepdims=True)
    gidx = jnp.min(jnp.where(glm == gmax, grow, SUBLANES), axis=0, keepdims=True)
    gsum = jnp.sum(jnp.where(gvalid, jnp.exp(gl - gmax), 0.0), axis=0, keepdims=True)
    g_p = 1.0 / gsum

    el = lt[ROUTER_EXPERT_ROW:ROUTER_EXPERT_ROW + n_experts, :]
    erow = lax.broadcasted_iota(I32, el.shape, 0)
    v = jnp.where(erow // (n_experts // n_groups) == gidx, el, neg)
    v1 = jnp.max(v, axis=0, keepdims=True)
    i1 = jnp.min(jnp.where(v == v1, erow, n_experts), axis=0, keepdims=True)
    vv = jnp.where(erow == i1, neg, v)
    v2 = jnp.max(vv, axis=0, keepdims=True)
    i2 = jnp.min(jnp.where(vv == v2, erow, n_experts), axis=0, keepdims=True)
    e2 = jnp.exp(v2 - v1)
    den = 1.0 + e2
    eidx_ref[0:1, :] = i1
    eidx_ref[1:2, :] = i2
    wts_ref[0:1, :] = (1.0 / den) * g_p
    wts_ref[1:2, :] = (e2 / den) * g_p


def _prompt_mixer_kernel(x_ref, *refs, n_params, tile, hist, alpha, n_groups, n_experts):
    p = refs[:n_params]
    x1_ref, eidx_ref, wts_ref, sa_ref, sb_ref, sc_ref = refs[n_params:n_params + 6]
    bufs = refs[n_params + 6:]
    _, buf_a, buf_b, buf_c, _, _ = bufs
    ha, hb, hc = hist
    t = pl.program_id(1)

    @pl.when(t == 0)
    def _():
        buf_a[0:ha, :] = jnp.zeros((ha, buf_a.shape[1]), F32)
        buf_b[0:hb, :] = jnp.zeros((hb, buf_b.shape[1]), F32)
        buf_c[0:hc, :] = jnp.zeros((hc, buf_c.shape[1]), F32)

    _mixer_rows(x_ref, p, bufs, (x1_ref, eidx_ref, wts_ref), rows=tile, stride=1,
                new_a=ha, new_b=hb, new_c=hc,
                pos_of_row=lambda r: r + t * tile,
                alpha=alpha, n_groups=n_groups, n_experts=n_experts)

    buf_a[0:ha, :] = buf_a[tile:tile + ha, :]
    buf_b[0:hb, :] = buf_b[tile:tile + hb, :]
    buf_c[0:hc, :] = buf_c[tile:tile + hc, :]

    @pl.when(t == pl.num_programs(1) - 1)
    def _():
        na, nb, nc = sa_ref.shape[0], sb_ref.shape[0], sc_ref.shape[0]
        sa_ref[...] = buf_a[ha - na:ha, :]
        sb_ref[...] = buf_b[hb - nb:hb, :]
        sc_ref[...] = buf_c[hc - nc:hc, :]


def _sample_mixer_kernel(x_ref, sta_ref, stb_ref, stc_ref, *refs, n_params, batch, steps,
                         alpha, n_groups, n_experts):
    p = refs[:n_params]
    x1_ref, eidx_ref, wts_ref, sa_ref, sb_ref, sc_ref = refs[n_params:n_params + 6]
    bufs = refs[n_params + 6:]
    _, buf_a, buf_b, buf_c, _, _ = bufs
    w_a, w_b, w_c = buf_a.shape[1], buf_b.shape[1], buf_c.shape[1]
    na, nb, nc = sta_ref.shape[1] // w_a, stb_ref.shape[1] // w_b, stc_ref.shape[1] // w_c
    rows = batch * steps

    for j in range(na):
        buf_a[j * batch:(j + 1) * batch, :] = sta_ref[:, j * w_a:(j + 1) * w_a]
    for j in range(nb):
        buf_b[j * batch:(j + 1) * batch, :] = stb_ref[:, j * w_b:(j + 1) * w_b]
    for j in range(nc):
        buf_c[j * batch:(j + 1) * batch, :] = stc_ref[:, j * w_c:(j + 1) * w_c]

    _mixer_rows(x_ref, p, bufs, (x1_ref, eidx_ref, wts_ref), rows=rows, stride=batch,
                new_a=na * batch, new_b=nb * batch, new_c=nc * batch,
                pos_of_row=lambda r: PAST_LEN + r // batch,
                alpha=alpha, n_groups=n_groups, n_experts=n_experts)

    for j in range(na):
        sa_ref[:, j * w_a:(j + 1) * w_a] = buf_a[(j + steps) * batch:(j + steps + 1) * batch, :]
    for j in range(nb):
        sb_ref[:, j * w_b:(j + 1) * w_b] = buf_b[(j + steps) * batch:(j + steps + 1) * batch, :]
    for j in range(nc):
        sc_ref[:, j * w_c:(j + 1) * w_c] = buf_c[(j + steps) * batch:(j + steps + 1) * batch, :]


def _full_spec(a):
    nd = a.ndim
    return pl.BlockSpec(a.shape, lambda *_: (0,) * nd)


def _mixer_scratch(rows, ha, hb, hc, d_in, w_a, w_b, w_c):
    return [
        pltpu.VMEM((rows, d_in), F32),
        pltpu.VMEM((ha + rows, w_a), F32),
        pltpu.VMEM((hb + rows, w_b), F32),
        pltpu.VMEM((hc + rows, w_c), F32),
        pltpu.VMEM((rows, w_b), BF16),
        pltpu.VMEM((rows, w_a + w_b + w_c), BF16),
    ]


def _round_up(n, m):
    return (n + m - 1) // m * m


def _prompt_mixer(x, params, dims, alpha):
    b, t, d = x.shape
    w_a, w_b, w_c, ka, kb, kc, n_groups, n_experts = dims
    tile = PROMPT_TILE
    ha, hb, hc = _round_up(ka - 1, SUBLANES), _round_up(kb, SUBLANES), _round_up(kc - 1, SUBLANES)
    d_in = params[0].shape[1]
    kern = functools.partial(_prompt_mixer_kernel, n_params=len(params), tile=tile,
                             hist=(ha, hb, hc), alpha=alpha, n_groups=n_groups,
                             n_experts=n_experts)
    out_shape = (
        jax.ShapeDtypeStruct((b, t, d), F32),
        jax.ShapeDtypeStruct((b, TOP_K, t), I32),
        jax.ShapeDtypeStruct((b, TOP_K, t), F32),
        jax.ShapeDtypeStruct((b, ka - 1, w_a), F32),
        jax.ShapeDtypeStruct((b, kb, w_b), F32),
        jax.ShapeDtypeStruct((b, kc - 1, w_c), F32),
    )
    out_specs = (
        pl.BlockSpec((None, tile, d), lambda i, j: (i, j, 0)),
        pl.BlockSpec((None, TOP_K, tile), lambda i, j: (i, 0, j)),
        pl.BlockSpec((None, TOP_K, tile), lambda i, j: (i, 0, j)),
        pl.BlockSpec((None, ka - 1, w_a), lambda i, j: (i, 0, 0)),
        pl.BlockSpec((None, kb, w_b), lambda i, j: (i, 0, 0)),
        pl.BlockSpec((None, kc - 1, w_c), lambda i, j: (i, 0, 0)),
    )
    in_specs = [pl.BlockSpec((None, tile, d), lambda i, j: (i, j, 0))] + [_full_spec(a) for a in params]
    return pl.pallas_call(
        kern,
        grid=(b, t // tile),
        in_specs=in_specs,
        out_specs=out_specs,
        out_shape=out_shape,
        scratch_shapes=_mixer_scratch(tile, ha, hb, hc, d_in, w_a, w_b, w_c),
        compiler_params=pltpu.CompilerParams(
            dimension_semantics=("arbitrary", "arbitrary"),
            vmem_limit_bytes=VMEM_LIMIT_BYTES),
        name="prompt_mixer",
    )(x, *params)


def _sample_mixer(x_tm, st_a, st_b, st_c, params, dims, alpha, batch, steps):
    rows, d = x_tm.shape
    w_a, w_b, w_c, ka, kb, kc, n_groups, n_experts = dims
    d_in = params[0].shape[1]
    kern = functools.partial(_sample_mixer_kernel, n_params=len(params), batch=batch,
                             steps=steps, alpha=alpha, n_groups=n_groups, n_experts=n_experts)
    out_shape = (
        jax.ShapeDtypeStruct((rows, d), F32),
        jax.ShapeDtypeStruct((TOP_K, rows), I32),
        jax.ShapeDtypeStruct((TOP_K, rows), F32),
        jax.ShapeDtypeStruct(st_a.shape, F32),
        jax.ShapeDtypeStruct(st_b.shape, F32),
        jax.ShapeDtypeStruct(st_c.shape, F32),
    )
    args = (x_tm, st_a, st_b, st_c) + tuple(params)
    return pl.pallas_call(
        kern,
        grid=(1,),
        in_specs=[_full_spec(a) for a in args],
        out_specs=tuple(pl.BlockSpec(s.shape, lambda i, n=len(s.shape): (0,) * n) for s in out_shape),
        out_shape=out_shape,
        scratch_shapes=_mixer_scratch(rows, (ka - 1) * batch, kb * batch, (kc - 1) * batch,
                                      d_in, w_a, w_b, w_c),
        compiler_params=pltpu.CompilerParams(
            dimension_semantics=("arbitrary",),
            vmem_limit_bytes=VMEM_LIMIT_BYTES),
        name="sample_mixer",
    )(*args)


def _row_copy(src_hbm, row, dst_vmem, slot, sem):
    return pltpu.make_async_copy(src_hbm.at[pl.ds(row, 1)], dst_vmem.at[pl.ds(slot, 1)], sem)


def _gather_rows(src_hbm, idx_of_slot, dst_vmem, n, sem):
    def start(s, c):
        _row_copy(src_hbm, idx_of_slot(s), dst_vmem, s, sem).start()
        return c
    lax.fori_loop(0, n, start, 0)

    def wait(s, c):
        _row_copy(src_hbm, 0, dst_vmem, s, sem).wait()
        return c
    lax.fori_loop(0, n, wait, 0)


def _moe_kernel(te_ref, nu_ref, src_ref, x_hbm, wg_ref, wu_ref, wd_ref, y_ref, xbuf, sem, *, tile):
    i = pl.program_id(0)

    @pl.when(i < nu_ref[0])
    def _():
        _gather_rows(x_hbm, lambda s: src_ref[0, 0, s], xbuf, tile, sem)
        xb = xbuf[...].astype(BF16)
        hg = _dot(xb, wg_ref[...])
        hu = _dot(xb, wu_ref[...])
        h = hg * _sigmoid(hg) * hu
        y_ref[...] = _dot(h.astype(BF16), wd_ref[...])

    @pl.when(i >= nu_ref[0])
    def _():
        y_ref[...] = jnp.zeros(y_ref.shape, F32)


def _moe(x1, src, tile_expert, n_used, w_gate, w_up, w_down, layer):
    n, d = x1.shape
    tile = MOE_TILE
    n_tiles = src.shape[0] // tile
    d_e = w_gate.shape[-1]
    grid_spec = pltpu.PrefetchScalarGridSpec(
        num_scalar_prefetch=2,
        grid=(n_tiles,),
        in_specs=[
            pl.BlockSpec((1, 1, tile), lambda i, te, nu: (i, 0, 0), memory_space=pltpu.SMEM),
            pl.BlockSpec(memory_space=pl.ANY),
            pl.BlockSpec((None, None, d, d_e), lambda i, te, nu: (layer, te[i], 0, 0)),
            pl.BlockSpec((None, None, d, d_e), lambda i, te, nu: (layer, te[i], 0, 0)),
            pl.BlockSpec((None, None, d_e, d), lambda i, te, nu: (layer, te[i], 0, 0)),
        ],
        out_specs=pl.BlockSpec((tile, d), lambda i, te, nu: (i, 0)),
        scratch_shapes=[pltpu.VMEM((tile, d), F32), pltpu.SemaphoreType.DMA(())],
    )
    return pl.pallas_call(
        functools.partial(_moe_kernel, tile=tile),
        grid_spec=grid_spec,
        out_shape=jax.ShapeDtypeStruct((n_tiles * tile, d), F32),
        compiler_params=pltpu.CompilerParams(
            dimension_semantics=("arbitrary",),
            vmem_limit_bytes=VMEM_LIMIT_BYTES),
        name="moe",
    )(tile_expert, n_used, src.reshape(n_tiles, 1, tile), x1, w_gate, w_up, w_down)


def _combine_kernel(pos_ref, x1_ref, w_ref, g_ref, b_ref, y_hbm, o_ref, ybuf, sem, *, tile, alpha):
    _gather_rows(y_hbm, lambda s: pos_ref[0, 0, s], ybuf, TOP_K * tile, sem)
    w = w_ref[...]
    moe = w[:, 0:1] * ybuf[0:tile, :] + w[:, 1:2] * ybuf[tile:2 * tile, :]
    o_ref[...] = _layer_norm(alpha * x1_ref[...] + moe, g_ref[...], b_ref[...])


def _combine(x1, y, pos, wts, g, b, alpha):
    n, d = x1.shape
    tile = min(COMBINE_TILE, n)
    n_tiles = n // tile
    pos_t = pos.reshape(TOP_K, n_tiles, tile).transpose(1, 0, 2).reshape(n_tiles, 1, TOP_K * tile)
    return pl.pallas_call(
        functools.partial(_combine_kernel, tile=tile, alpha=alpha),
        grid=(n_tiles,),
        in_specs=[
            pl.BlockSpec((1, 1, TOP_K * tile), lambda i: (i, 0, 0), memory_space=pltpu.SMEM),
            pl.BlockSpec((tile, d), lambda i: (i, 0)),
            pl.BlockSpec((tile, TOP_K), lambda i: (i, 0)),
            _full_spec(g),
            _full_spec(b),
            pl.BlockSpec(memory_space=pl.ANY),
        ],
        out_specs=pl.BlockSpec((tile, d), lambda i: (i, 0)),
        out_shape=jax.ShapeDtypeStruct((n, d), F32),
        scratch_shapes=[pltpu.VMEM((TOP_K * tile, d), F32), pltpu.SemaphoreType.DMA(())],
        compiler_params=pltpu.CompilerParams(
            dimension_semantics=("arbitrary",),
            vmem_limit_bytes=VMEM_LIMIT_BYTES),
        name="combine",
    )(pos_t, x1, wts.T, g, b, y)


def _route_tables(eidx, n_experts, tile):
    k, n = eidx.shape
    pairs = k * n
    n_tiles = pairs // tile + n_experts
    e = eidx.reshape(pairs)
    onehot = (e[:, None] == jnp.arange(n_experts, dtype=I32)[None, :]).astype(I32)
    csum = jnp.cumsum(onehot, axis=0)
    rank = jnp.sum(onehot * csum, axis=1) - 1
    counts = csum[-1]
    ptiles = (counts + tile - 1) // tile
    tile_end = jnp.cumsum(ptiles)
    offs = (tile_end - ptiles) * tile
    pos = jnp.sum(onehot * offs[None, :], axis=1) + rank
    n_used = tile_end[-1]
    j = jnp.arange(n_tiles, dtype=I32)
    te = jnp.sum((j[:, None] >= tile_end[None, :]).astype(I32), axis=1)
    te_last = jnp.sum((n_used - 1 >= tile_end).astype(I32))
    te = jnp.where(j < n_used, te, te_last).astype(I32)
    tok = jnp.tile(jnp.arange(n, dtype=I32), k)
    src = jnp.zeros((n_tiles * tile,), I32).at[pos].set(tok, unique_indices=True)
    return pos.reshape(k, n), src, te, n_used.reshape(1).astype(I32)


def _layer_params(l, w_in_b, b_in, conv_a_w, conv_a_b, ln_a_g, ln_a_b, w_pool_bd, pool_scale,
                  conv_c_w, w_out_b, b_out, ln1_g, ln1_b, w_router, b_router):
    row = lambda a: a[l][None, :]
    return (w_in_b[l], row(b_in), conv_a_w[l], row(conv_a_b), row(ln_a_g), row(ln_a_b),
            w_pool_bd[l], row(pool_scale), conv_c_w[l], w_out_b[l], row(b_out), row(ln1_g),
            row(ln1_b), w_router[l], b_router[l])


def kernel(x_prompt, x_sample, state_conv_a, state_pool_b, state_conv_c, w_in, b_in, conv_a_w, conv_a_b, ln_a_g, ln_a_b, w_pool, pool_scale, conv_c_w, w_out, b_out, ln1_g, ln1_b, w_router_group, b_router_group, w_router_expert, b_router_expert, w_gate, w_up, w_down, ln2_g, ln2_b):
    depth = w_in.shape[0]
    bp, tp, d = x_prompt.shape
    bs, ts, _ = x_sample.shape
    ka, w_a = conv_a_w.shape[1:]
    kb, w_b = state_pool_b.shape[2:]
    kc, w_c = conv_c_w.shape[1:]
    n_groups = w_router_group.shape[-1]
    n_experts = w_router_expert.shape[-1]
    dims = (w_a, w_b, w_c, ka, kb, kc, n_groups, n_experts)
    alpha = float((2 * depth) ** 0.25)
    assert tp % PROMPT_TILE == 0 and PROMPT_TILE % ROW_CHUNK == 0 and (bs * ts) % ROW_CHUNK == 0
    assert n_groups <= SUBLANES and ROUTER_EXPERT_ROW + n_experts <= LANES

    w_in_b = w_in.astype(BF16)
    w_out_b = w_out.astype(BF16)
    wg_b, wu_b, wd_b = w_gate.astype(BF16), w_up.astype(BF16), w_down.astype(BF16)
    n_pg, pg = w_pool.shape[1], w_pool.shape[2]
    eye = jnp.eye(n_pg, dtype=F32)
    w_pool_bd = (w_pool[:, :, :, None, :] * eye[None, :, None, :, None]).reshape(depth, n_pg * pg, n_pg * pg).astype(BF16)
    wr = jnp.zeros((depth, d, LANES), F32)
    wr = wr.at[:, :, ROUTER_GROUP_ROW:ROUTER_GROUP_ROW + n_groups].set(w_router_group)
    wr = wr.at[:, :, ROUTER_EXPERT_ROW:ROUTER_EXPERT_ROW + n_experts].set(w_router_expert)
    wr_hi = wr.astype(BF16)
    wr_lo = (wr - wr_hi.astype(F32)).astype(BF16)
    w_router = jnp.concatenate([wr_hi, wr_lo], axis=-1)
    b_router = jnp.zeros((depth, 1, LANES), F32)
    b_router = b_router.at[:, 0, ROUTER_GROUP_ROW:ROUTER_GROUP_ROW + n_groups].set(b_router_group)
    b_router = b_router.at[:, 0, ROUTER_EXPERT_ROW:ROUTER_EXPERT_ROW + n_experts].set(b_router_expert)

    xp = x_prompt
    xs = x_sample.transpose(1, 0, 2).reshape(ts * bs, d)
    new_states = [[] for _ in range(6)]
    for l in range(depth):
        params = _layer_params(l, w_in_b, b_in, conv_a_w, conv_a_b, ln_a_g, ln_a_b, w_pool_bd,
                               pool_scale, conv_c_w, w_out_b, b_out, ln1_g, ln1_b, w_router, b_router)
        g2, b2 = ln2_g[l][None, :], ln2_b[l][None, :]

        x1, eidx, wts, pa, pb, pc = _prompt_mixer(xp, params, dims, alpha)
        x1 = x1.reshape(bp * tp, d)
        eidx = eidx.transpose(1, 0, 2).reshape(TOP_K, bp * tp)
        wts = wts.transpose(1, 0, 2).reshape(TOP_K, bp * tp)
        pos, src, te, nu = _route_tables(eidx, n_experts, MOE_TILE)
        y = _moe(x1, src, te, nu, wg_b, wu_b, wd_b, l)
        xp = _combine(x1, y, pos, wts, g2, b2, alpha).reshape(bp, tp, d)

        s1, seidx, swts, sa, sb, sc = _sample_mixer(
            xs, state_conv_a[l].reshape(bs, -1), state_pool_b[l].reshape(bs, -1),
            state_conv_c[l].reshape(bs, -1), params, dims, alpha, bs, ts)
        spos, ssrc, ste, snu = _route_tables(seidx, n_experts, MOE_TILE)
        ys = _moe(s1, ssrc, ste, snu, wg_b, wu_b, wd_b, l)
        xs = _combine(s1, ys, spos, swts, g2, b2, alpha)

        for lst, val in zip(new_states, (pa, sa.reshape(bs, ka - 1, w_a), pb, sb.reshape(bs, kb, w_b),
                                         pc, sc.reshape(bs, kc - 1, w_c))):
            lst.append(val)

    y_prompt = xp
    y_sample = xs.reshape(ts, bs, d).transpose(1, 0, 2)
    return (y_prompt, y_sample) + tuple(jnp.stack(s) for s in new_states)
```

```python
import functools

import jax
import jax.numpy as jnp
from jax import lax
from jax.experimental import pallas as pl
from jax.experimental.pallas import tpu as pltpu

F32 = jnp.float32
BF16 = jnp.bfloat16
I32 = jnp.int32

POOL_WINDOWS = (2, 4, 8, 16)
TOP_K = 2
LN_EPS = 1e-5
PAST_LEN = 16384

LANES = 128
SUBLANES = 8
SUBLANE_SHIFT = 3
VMEM_LIMIT_BYTES = 56 * 1024 * 1024

ROW_CHUNK = 32
PROMPT_TILE = 256
MOE_TILE = 256
COMBINE_TILE = 256
GATHER_UNROLL = 16
GATHER_PITCH = 9
ROUTER_GROUP_ROW = 0
ROUTER_EXPERT_ROW = 8


def _dot(a, b):
    return jnp.dot(a, b, preferred_element_type=F32)


def _layer_norm(x, g, b):
    mu = jnp.mean(x, axis=-1, keepdims=True)
    xc = x - mu
    var = jnp.mean(xc * xc, axis=-1, keepdims=True)
    return xc * lax.rsqrt(var + LN_EPS) * g + b


def _sigmoid(x):
    return 1.0 / (1.0 + jnp.exp(-x))


def _store_row_tiles(ref, val):
    rows, d = val.shape
    for j in range(d // LANES):
        ref[pl.ds(j, rows, stride=SUBLANES), :] = val[:, j * LANES:(j + 1) * LANES]


def _load_row_tiles(ref, rows, first=0, pitch=SUBLANES):
    return jnp.concatenate(
        [ref[pl.ds(first + j, rows, stride=pitch), :] for j in range(SUBLANES)], axis=-1)


def _mixer_rows(x_ref, p, bufs, outs, *, rows, stride, new_a, new_b, new_c, pos_of_row,
                alpha, n_groups, n_experts):
    (w_in, b_in, wa, ba, lnag, lnab, wbd, pscale, wc, w_out, b_out, ln1g, ln1b,
     wr, br) = p
    proj, buf_a, buf_b, buf_c, dbuf, cat = bufs
    x1_ref, eidx_ref, wts_ref = outs
    ka = wa.shape[0]
    kc = wc.shape[0]
    w_a = wa.shape[1]
    w_b = pscale.shape[1]
    w_c = wc.shape[1]
    o_gate, o_ub, o_cbg, o_ccg, o_ch = w_a, 2 * w_a, 2 * w_a + w_b, 2 * w_a + w_b + w_c, 2 * w_a + w_b + 2 * w_c

    x = x_ref[...]
    proj[...] = _dot(x.astype(BF16), w_in[...]) + b_in[...]

    buf_a[pl.ds(new_a, rows), :] = proj[:, 0:w_a] * _sigmoid(proj[:, o_gate:o_gate + w_a])
    buf_b[pl.ds(new_b, rows), :] = proj[:, o_ub:o_ub + w_b]
    buf_c[pl.ds(new_c, rows), :] = proj[:, o_ccg:o_ccg + w_c] * proj[:, o_ch:o_ch + w_c]

    lane_b = lax.broadcasted_iota(I32, (ROW_CHUNK, w_b), 1)
    group_b = lane_b // (w_b // len(POOL_WINDOWS))
    win_b = jnp.zeros((ROW_CHUNK, w_b), I32)
    for g, w in enumerate(POOL_WINDOWS):
        win_b = jnp.where(group_b == g, w, win_b)

    def chunk(c, carry):
        r0 = c * ROW_CHUNK if isinstance(c, int) else pl.multiple_of(c * ROW_CHUNK, ROW_CHUNK)
        acc = jnp.zeros((ROW_CHUNK, w_a), F32) + ba[...]
        for k in range(ka):
            src = buf_a[pl.ds(r0 + (new_a - (ka - 1 - k) * stride), ROW_CHUNK), :]
            acc = acc + wa[k:k + 1, :] * src
        a = _layer_norm(acc, lnag[...], lnab[...])
        cat[pl.ds(r0, ROW_CHUNK), 0:w_a] = (a * _sigmoid(a)).astype(BF16)
        cur = buf_b[pl.ds(r0 + new_b, ROW_CHUNK), :]
        run = cur
        wsum = jnp.zeros((ROW_CHUNK, w_b), F32)
        for j in range(1, max(POOL_WINDOWS) + 1):
            if j in POOL_WINDOWS:
                wsum = jnp.where(win_b == j, run, wsum)
            if j < max(POOL_WINDOWS):
                run = run + buf_b[pl.ds(r0 + (new_b - j * stride), ROW_CHUNK), :]
        pos = pos_of_row(r0 + lax.broadcasted_iota(I32, (ROW_CHUNK, w_b), 0))
        cnt = jnp.minimum(pos + 1, win_b).astype(F32)
        dbuf[pl.ds(r0, ROW_CHUNK), :] = (wsum / cnt - cur).astype(BF16)
        accc = jnp.zeros((ROW_CHUNK, w_c), F32)
        for k in range(kc):
            src = buf_c[pl.ds(r0 + (new_c - (kc - 1 - k) * stride), ROW_CHUNK), :]
            accc = accc + wc[k:k + 1, :] * src
        cbg = proj[pl.ds(r0, ROW_CHUNK), o_cbg:o_cbg + w_c]
        cat[pl.ds(r0, ROW_CHUNK), w_a + w_b:w_a + w_b + w_c] = (cbg * accc).astype(BF16)
        return carry

    if stride % SUBLANES == 0:
        lax.fori_loop(0, rows // ROW_CHUNK, chunk, 0)
    else:
        for c in range(rows // ROW_CHUNK):
            chunk(c, 0)

    cat[:, w_a:w_a + w_b] = (_dot(dbuf[...], wbd[...]) * pscale[...]).astype(BF16)

    m = _dot(cat[...], w_out[...]) + b_out[...]
    x1 = _layer_norm(alpha * x + m, ln1g[...], ln1b[...])
    _store_row_tiles(x1_ref, x1)

    x1_hi = x1.astype(BF16)
    x1_lo = (x1 - x1_hi.astype(F32)).astype(BF16)
    p_hi = _dot(x1_hi, wr[...])
    p_lo = _dot(x1_lo, wr[...])
    logits = (p_hi[:, 0:LANES] + p_hi[:, LANES:2 * LANES]) + (p_lo[:, 0:LANES] + p_lo[:, LANES:2 * LANES])
    lt = jnp.transpose(logits + br[...])

    neg = jnp.float32(-jnp.inf)
    gl = lt[ROUTER_GROUP_ROW:ROUTER_GROUP_ROW + SUBLANES, :]
    grow = lax.broadcasted_iota(I32, gl.shape, 0)
    gvalid = grow < n_groups
    glm = jnp.where(gvalid, gl, neg)
    gmax = jnp.max(glm, axis=0, keepdims=True)
    gidx = jnp.min(jnp.where(glm == gmax, grow, SUBLANES), axis=0, keepdims=True)
    gsum = jnp.sum(jnp.where(gvalid, jnp.exp(gl - gmax), 0.0), axis=0, keepdims=True)
    g_p = 1.0 / gsum

    el = lt[ROUTER_EXPERT_ROW:ROUTER_EXPERT_ROW + n_experts, :]
    erow = lax.broadcasted_iota(I32, el.shape, 0)
    v = jnp.where(erow // (n_experts // n_groups) == gidx, el, neg)
    v1 = jnp.max(v, axis=0, keepdims=True)
    i1 = jnp.min(jnp.where(v == v1, erow, n_experts), axis=0, keepdims=True)
    vv = jnp.where(erow == i1, neg, v)
    v2 = jnp.max(vv, axis=0, keepdims=True)
    i2 = jnp.min(jnp.where(vv == v2, erow, n_experts), axis=0, keepdims=True)
    e2 = jnp.exp(v2 - v1)
    den = 1.0 + e2
    eidx_ref[0:1, :] = i1
    eidx_ref[1:2, :] = i2
    wts_ref[0:1, :] = (1.0 / den) * g_p
    wts_ref[1:2, :] = (e2 / den) * g_p


def _prompt_mixer_kernel(x_ref, *refs, n_params, tile, hist, alpha, n_groups, n_experts):
    p = refs[:n_params]
    x1_ref, eidx_ref, wts_ref, sa_ref, sb_ref, sc_ref = refs[n_params:n_params + 6]
    bufs = refs[n_params + 6:]
    _, buf_a, buf_b, buf_c, _, _ = bufs
    ha, hb, hc = hist
    t = pl.program_id(1)

    @pl.when(t == 0)
    def _():
        buf_a[0:ha, :] = jnp.zeros((ha, buf_a.shape[1]), F32)
        buf_b[0:hb, :] = jnp.zeros((hb, buf_b.shape[1]), F32)
        buf_c[0:hc, :] = jnp.zeros((hc, buf_c.shape[1]), F32)

    _mixer_rows(x_ref, p, bufs, (x1_ref, eidx_ref, wts_ref), rows=tile, stride=1,
                new_a=ha, new_b=hb, new_c=hc,
                pos_of_row=lambda r: r + t * tile,
                alpha=alpha, n_groups=n_groups, n_experts=n_experts)

    buf_a[0:ha, :] = buf_a[tile:tile + ha, :]
    buf_b[0:hb, :] = buf_b[tile:tile + hb, :]
    buf_c[0:hc, :] = buf_c[tile:tile + hc, :]

    @pl.when(t == pl.num_programs(1) - 1)
    def _():
        na, nb, nc = sa_ref.shape[0], sb_ref.shape[0], sc_ref.shape[0]
        sa_ref[...] = buf_a[ha - na:ha, :]
        sb_ref[...] = buf_b[hb - nb:hb, :]
        sc_ref[...] = buf_c[hc - nc:hc, :]


def _sample_mixer_kernel(x_ref, sta_ref, stb_ref, stc_ref, *refs, n_params, batch, steps,
                         alpha, n_groups, n_experts):
    p = refs[:n_params]
    x1_ref, eidx_ref, wts_ref, sa_ref, sb_ref, sc_ref = refs[n_params:n_params + 6]
    bufs = refs[n_params + 6:]
    _, buf_a, buf_b, buf_c, _, _ = bufs
    w_a, w_b, w_c = buf_a.shape[1], buf_b.shape[1], buf_c.shape[1]
    na, nb, nc = sta_ref.shape[1] // w_a, stb_ref.shape[1] // w_b, stc_ref.shape[1] // w_c
    rows = batch * steps

    for j in range(na):
        buf_a[j * batch:(j + 1) * batch, :] = sta_ref[:, j * w_a:(j + 1) * w_a]
    for j in range(nb):
        buf_b[j * batch:(j + 1) * batch, :] = stb_ref[:, j * w_b:(j + 1) * w_b]
    for j in range(nc):
        buf_c[j * batch:(j + 1) * batch, :] = stc_ref[:, j * w_c:(j + 1) * w_c]

    _mixer_rows(x_ref, p, bufs, (x1_ref, eidx_ref, wts_ref), rows=rows, stride=batch,
                new_a=na * batch, new_b=nb * batch, new_c=nc * batch,
                pos_of_row=lambda r: PAST_LEN + r // batch,
                alpha=alpha, n_groups=n_groups, n_experts=n_experts)

    for j in range(na):
        sa_ref[:, j * w_a:(j + 1) * w_a] = buf_a[(j + steps) * batch:(j + steps + 1) * batch, :]
    for j in range(nb):
        sb_ref[:, j * w_b:(j + 1) * w_b] = buf_b[(j + steps) * batch:(j + steps + 1) * batch, :]
    for j in range(nc):
        sc_ref[:, j * w_c:(j + 1) * w_c] = buf_c[(j + steps) * batch:(j + steps + 1) * batch, :]


def _full_spec(a):
    nd = a.ndim
    return pl.BlockSpec(a.shape, lambda *_: (0,) * nd)


def _mixer_scratch(rows, ha, hb, hc, d_in, w_a, w_b, w_c):
    return [
        pltpu.VMEM((rows, d_in), F32),
        pltpu.VMEM((ha + rows, w_a), F32),
        pltpu.VMEM((hb + rows, w_b), F32),
        pltpu.VMEM((hc + rows, w_c), F32),
        pltpu.VMEM((rows, w_b), BF16),
        pltpu.VMEM((rows, w_a + w_b + w_c), BF16),
    ]


def _round_up(n, m):
    return (n + m - 1) // m * m


def _prompt_mixer(x, params, dims, alpha):
    b, t, d = x.shape
    w_a, w_b, w_c, ka, kb, kc, n_groups, n_experts = dims
    tile = PROMPT_TILE
    ha, hb, hc = _round_up(ka - 1, SUBLANES), _round_up(kb, SUBLANES), _round_up(kc - 1, SUBLANES)
    d_in = params[0].shape[1]
    kern = functools.partial(_prompt_mixer_kernel, n_params=len(params), tile=tile,
                             hist=(ha, hb, hc), alpha=alpha, n_groups=n_groups,
                             n_experts=n_experts)
    out_shape = (
        jax.ShapeDtypeStruct((b, t * SUBLANES, LANES), F32),
        jax.ShapeDtypeStruct((b, TOP_K, t), I32),
        jax.ShapeDtypeStruct((b, TOP_K, t), F32),
        jax.ShapeDtypeStruct((b, ka - 1, w_a), F32),
        jax.ShapeDtypeStruct((b, kb, w_b), F32),
        jax.ShapeDtypeStruct((b, kc - 1, w_c), F32),
    )
    out_specs = (
        pl.BlockSpec((None, tile * SUBLANES, LANES), lambda i, j: (i, j, 0)),
        pl.BlockSpec((None, TOP_K, tile), lambda i, j: (i, 0, j)),
        pl.BlockSpec((None, TOP_K, tile), lambda i, j: (i, 0, j)),
        pl.BlockSpec((None, ka - 1, w_a), lambda i, j: (i, 0, 0)),
        pl.BlockSpec((None, kb, w_b), lambda i, j: (i, 0, 0)),
        pl.BlockSpec((None, kc - 1, w_c), lambda i, j: (i, 0, 0)),
    )
    in_specs = [pl.BlockSpec((None, tile, d), lambda i, j: (i, j, 0))] + [_full_spec(a) for a in params]
    return pl.pallas_call(
        kern,
        grid=(b, t // tile),
        in_specs=in_specs,
        out_specs=out_specs,
        out_shape=out_shape,
        scratch_shapes=_mixer_scratch(tile, ha, hb, hc, d_in, w_a, w_b, w_c),
        compiler_params=pltpu.CompilerParams(
            dimension_semantics=("arbitrary", "arbitrary"),
            vmem_limit_bytes=VMEM_LIMIT_BYTES),
        name="prompt_mixer",
    )(x, *params)


def _sample_mixer(x_tm, st_a, st_b, st_c, params, dims, alpha, batch, steps):
    rows, d = x_tm.shape
    w_a, w_b, w_c, ka, kb, kc, n_groups, n_experts = dims
    d_in = params[0].shape[1]
    kern = functools.partial(_sample_mixer_kernel, n_params=len(params), batch=batch,
                             steps=steps, alpha=alpha, n_groups=n_groups, n_experts=n_experts)
    out_shape = (
        jax.ShapeDtypeStruct((rows * SUBLANES, LANES), F32),
        jax.ShapeDtypeStruct((TOP_K, rows), I32),
        jax.ShapeDtypeStruct((TOP_K, rows), F32),
        jax.ShapeDtypeStruct(st_a.shape, F32),
        jax.ShapeDtypeStruct(st_b.shape, F32),
        jax.ShapeDtypeStruct(st_c.shape, F32),
    )
    args = (x_tm, st_a, st_b, st_c) + tuple(params)
    return pl.pallas_call(
        kern,
        grid=(1,),
        in_specs=[_full_spec(a) for a in args],
        out_specs=tuple(pl.BlockSpec(s.shape, lambda i, n=len(s.shape): (0,) * n) for s in out_shape),
        out_shape=out_shape,
        scratch_shapes=_mixer_scratch(rows, (ka - 1) * batch, kb * batch, (kc - 1) * batch,
                                      d_in, w_a, w_b, w_c),
        compiler_params=pltpu.CompilerParams(
            dimension_semantics=("arbitrary",),
            vmem_limit_bytes=VMEM_LIMIT_BYTES),
        name="sample_mixer",
    )(*args)


def _start_row_gather(src_hbm, idx_ref, dst_vmem, n, sem):
    def body(c, carry):
        for u in range(GATHER_UNROLL):
            s = c * GATHER_UNROLL + u
            row = idx_ref[0, 0, s]
            pltpu.make_async_copy(
                src_hbm.at[pl.ds(pl.multiple_of(row * SUBLANES, SUBLANES), SUBLANES)],
                dst_vmem.at[pl.ds(s * GATHER_PITCH, SUBLANES)],
                sem).start()
        return carry
    lax.fori_loop(0, n // GATHER_UNROLL, body, 0)


def _wait_row_gather(src_hbm, dst_vmem, n, sem):
    pltpu.make_async_copy(src_hbm.at[pl.ds(0, n * SUBLANES)], dst_vmem.at[pl.ds(0, n * SUBLANES)],
                          sem).wait()


def _moe_kernel(te_ref, nu_ref, src0_ref, src1_ref, x_hbm, wg_ref, wu_ref, wd_ref, y_ref,
                xbuf, wg_b, wu_b, wd_b, sem, *, tile):
    i = pl.program_id(0)
    n_used = nu_ref[0]
    slot = i % 2

    @pl.when(i == 0)
    def _():
        _start_row_gather(x_hbm, src0_ref, xbuf.at[0], tile, sem.at[0])

    @pl.when(i + 1 < n_used)
    def _():
        _start_row_gather(x_hbm, src1_ref, xbuf.at[1 - slot], tile, sem.at[1 - slot])

    @pl.when(i < n_used)
    def _():
        @pl.when(jnp.logical_or(i == 0, te_ref[i] != te_ref[jnp.maximum(i - 1, 0)]))
        def _():
            wg_b[...] = wg_ref[...].astype(BF16)
            wu_b[...] = wu_ref[...].astype(BF16)
            wd_b[...] = wd_ref[...].astype(BF16)

        _wait_row_gather(x_hbm, xbuf.at[slot], tile, sem.at[slot])
        xb = _load_row_tiles(xbuf.at[slot], tile, pitch=GATHER_PITCH).astype(BF16)
        hg = _dot(xb, wg_b[...])
        hu = _dot(xb, wu_b[...])
        h = hg * _sigmoid(hg) * hu
        _store_row_tiles(y_ref, _dot(h.astype(BF16), wd_b[...]))

    @pl.when(i >= n_used)
    def _():
        y_ref[...] = jnp.zeros(y_ref.shape, F32)


def _moe(x1, src, tile_expert, n_used, w_gate, w_up, w_down, layer):
    d = w_gate.shape[-2]
    tile = MOE_TILE
    n_tiles = src.shape[0] // tile
    d_e = w_gate.shape[-1]
    grid_spec = pltpu.PrefetchScalarGridSpec(
        num_scalar_prefetch=2,
        grid=(n_tiles,),
        in_specs=[
            pl.BlockSpec((1, 1, tile), lambda i, te, nu: (i, 0, 0), memory_space=pltpu.SMEM),
            pl.BlockSpec((1, 1, tile), lambda i, te, nu: (jnp.minimum(i + 1, n_tiles - 1), 0, 0),
                         memory_space=pltpu.SMEM),
            pl.BlockSpec(memory_space=pl.ANY),
            pl.BlockSpec((None, None, d, d_e), lambda i, te, nu: (layer, te[i], 0, 0)),
            pl.BlockSpec((None, None, d, d_e), lambda i, te, nu: (layer, te[i], 0, 0)),
            pl.BlockSpec((None, None, d_e, d), lambda i, te, nu: (layer, te[i], 0, 0)),
        ],
        out_specs=pl.BlockSpec((tile * SUBLANES, LANES), lambda i, te, nu: (i, 0)),
        scratch_shapes=[
            pltpu.VMEM((2, tile * GATHER_PITCH, LANES), F32),
            pltpu.VMEM((d, d_e), BF16),
            pltpu.VMEM((d, d_e), BF16),
            pltpu.VMEM((d_e, d), BF16),
            pltpu.SemaphoreType.DMA((2,)),
        ],
    )
    src3 = src.reshape(n_tiles, 1, tile)
    return pl.pallas_call(
        functools.partial(_moe_kernel, tile=tile),
        grid_spec=grid_spec,
        out_shape=jax.ShapeDtypeStruct((n_tiles * tile * SUBLANES, LANES), F32),
        compiler_params=pltpu.CompilerParams(
            dimension_semantics=("arbitrary",),
            vmem_limit_bytes=VMEM_LIMIT_BYTES),
        name="moe",
    )(tile_expert, n_used, src3, src3, x1, w_gate, w_up, w_down)


def _combine_kernel(pos0_ref, pos1_ref, x1_ref, w_ref, g_ref, b_ref, y_hbm, o_ref, ybuf, sem, *,
                    tile, alpha):
    i = pl.program_id(0)
    rows = TOP_K * tile
    slot = i % 2

    @pl.when(i == 0)
    def _():
        _start_row_gather(y_hbm, pos0_ref, ybuf.at[0], rows, sem.at[0])

    @pl.when(i + 1 < pl.num_programs(0))
    def _():
        _start_row_gather(y_hbm, pos1_ref, ybuf.at[1 - slot], rows, sem.at[1 - slot])

    _wait_row_gather(y_hbm, ybuf.at[slot], rows, sem.at[slot])
    w = w_ref[...]
    y_a = _load_row_tiles(ybuf.at[slot], tile, pitch=GATHER_PITCH)
    y_b = _load_row_tiles(ybuf.at[slot], tile, first=tile * GATHER_PITCH, pitch=GATHER_PITCH)
    moe = w[:, 0:1] * y_a + w[:, 1:2] * y_b
    x1 = _load_row_tiles(x1_ref, tile)
    o_ref[...] = _layer_norm(alpha * x1 + moe, g_ref[...], b_ref[...])


def _combine(x1, y, pos, wts, g, b, alpha):
    n, d = x1.shape[0] // SUBLANES, g.shape[-1]
    tile = min(COMBINE_TILE, n)
    n_tiles = n // tile
    pos_t = pos.reshape(TOP_K, n_tiles, tile).transpose(1, 0, 2).reshape(n_tiles, 1, TOP_K * tile)
    return pl.pallas_call(
        functools.partial(_combine_kernel, tile=tile, alpha=alpha),
        grid=(n_tiles,),
        in_specs=[
            pl.BlockSpec((1, 1, TOP_K * tile), lambda i: (i, 0, 0), memory_space=pltpu.SMEM),
            pl.BlockSpec((1, 1, TOP_K * tile), lambda i: (jnp.minimum(i + 1, n_tiles - 1), 0, 0),
                         memory_space=pltpu.SMEM),
            pl.BlockSpec((tile * SUBLANES, LANES), lambda i: (i, 0)),
            pl.BlockSpec((tile, TOP_K), lambda i: (i, 0)),
            _full_spec(g),
            _full_spec(b),
            pl.BlockSpec(memory_space=pl.ANY),
        ],
        out_specs=pl.BlockSpec((tile, d), lambda i: (i, 0)),
        out_shape=jax.ShapeDtypeStruct((n, d), F32),
        scratch_shapes=[pltpu.VMEM((2, TOP_K * tile * GATHER_PITCH, LANES), F32),
                        pltpu.SemaphoreType.DMA((2,))],
        compiler_params=pltpu.CompilerParams(
            dimension_semantics=("arbitrary",),
            vmem_limit_bytes=VMEM_LIMIT_BYTES),
        name="combine",
    )(pos_t, pos_t, x1, wts.T, g, b, y)


def _route_tables(eidx, n_experts, tile):
    k, n = eidx.shape
    pairs = k * n
    n_tiles = pairs // tile + n_experts
    e = eidx.reshape(pairs)
    onehot = (e[:, None] == jnp.arange(n_experts, dtype=I32)[None, :]).astype(I32)
    csum = jnp.cumsum(onehot, axis=0)
    rank = jnp.sum(onehot * csum, axis=1) - 1
    counts = csum[-1]
    ptiles = (counts + tile - 1) // tile
    tile_end = jnp.cumsum(ptiles)
    offs = (tile_end - ptiles) * tile
    pos = jnp.sum(onehot * offs[None, :], axis=1) + rank
    n_used = tile_end[-1]
    j = jnp.arange(n_tiles, dtype=I32)
    te = jnp.sum((j[:, None] >= tile_end[None, :]).astype(I32), axis=1)
    te_last = jnp.sum((n_used - 1 >= tile_end).astype(I32))
    te = jnp.where(j < n_used, te, te_last).astype(I32)
    tok = jnp.tile(jnp.arange(n, dtype=I32), k)
    src = jnp.zeros((n_tiles * tile,), I32).at[pos].set(tok, unique_indices=True)
    return pos.reshape(k, n), src, te, n_used.reshape(1).astype(I32)


def _layer_params(l, w_in_b, b_in, conv_a_w, conv_a_b, ln_a_g, ln_a_b, w_pool_bd, pool_scale,
                  conv_c_w, w_out_b, b_out, ln1_g, ln1_b, w_router, b_router):
    row = lambda a: a[l][None, :]
    return (w_in_b[l], row(b_in), conv_a_w[l], row(conv_a_b), row(ln_a_g), row(ln_a_b),
            w_pool_bd[l], row(pool_scale), conv_c_w[l], w_out_b[l], row(b_out), row(ln1_g),
            row(ln1_b), w_router[l], b_router[l])


def kernel(x_prompt, x_sample, state_conv_a, state_pool_b, state_conv_c, w_in, b_in, conv_a_w, conv_a_b, ln_a_g, ln_a_b, w_pool, pool_scale, conv_c_w, w_out, b_out, ln1_g, ln1_b, w_router_group, b_router_group, w_router_expert, b_router_expert, w_gate, w_up, w_down, ln2_g, ln2_b):
    depth = w_in.shape[0]
    bp, tp, d = x_prompt.shape
    bs, ts, _ = x_sample.shape
    ka, w_a = conv_a_w.shape[1:]
    kb, w_b = state_pool_b.shape[2:]
    kc, w_c = conv_c_w.shape[1:]
    n_groups = w_router_group.shape[-1]
    n_experts = w_router_expert.shape[-1]
    dims = (w_a, w_b, w_c, ka, kb, kc, n_groups, n_experts)
    alpha = float((2 * depth) ** 0.25)
    assert tp % PROMPT_TILE == 0 and PROMPT_TILE % ROW_CHUNK == 0 and (bs * ts) % ROW_CHUNK == 0
    assert n_groups <= SUBLANES and ROUTER_EXPERT_ROW + n_experts <= LANES
    assert d == SUBLANES * LANES

    w_in_b = w_in.astype(BF16)
    w_out_b = w_out.astype(BF16)
    n_pg, pg = w_pool.shape[1], w_pool.shape[2]
    eye = jnp.eye(n_pg, dtype=F32)
    w_pool_bd = (w_pool[:, :, :, None, :] * eye[None, :, None, :, None]).reshape(depth, n_pg * pg, n_pg * pg).astype(BF16)
    wr = jnp.zeros((depth, d, LANES), F32)
    wr = wr.at[:, :, ROUTER_GROUP_ROW:ROUTER_GROUP_ROW + n_groups].set(w_router_group)
    wr = wr.at[:, :, ROUTER_EXPERT_ROW:ROUTER_EXPERT_ROW + n_experts].set(w_router_expert)
    wr_hi = wr.astype(BF16)
    wr_lo = (wr - wr_hi.astype(F32)).astype(BF16)
    w_router = jnp.concatenate([wr_hi, wr_lo], axis=-1)
    b_router = jnp.zeros((depth, 1, LANES), F32)
    b_router = b_router.at[:, 0, ROUTER_GROUP_ROW:ROUTER_GROUP_ROW + n_groups].set(b_router_group)
    b_router = b_router.at[:, 0, ROUTER_EXPERT_ROW:ROUTER_EXPERT_ROW + n_experts].set(b_router_expert)

    xp = x_prompt
    xs = x_sample.transpose(1, 0, 2).reshape(ts * bs, d)
    new_states = [[] for _ in range(6)]
    for l in range(depth):
        params = _layer_params(l, w_in_b, b_in, conv_a_w, conv_a_b, ln_a_g, ln_a_b, w_pool_bd,
                               pool_scale, conv_c_w, w_out_b, b_out, ln1_g, ln1_b, w_router, b_router)
        g2, b2 = ln2_g[l][None, :], ln2_b[l][None, :]

        x1, eidx, wts, pa, pb, pc = _prompt_mixer(xp, params, dims, alpha)
        x1 = x1.reshape(bp * tp * SUBLANES, LANES)
        eidx = eidx.transpose(1, 0, 2).reshape(TOP_K, bp * tp)
        wts = wts.transpose(1, 0, 2).reshape(TOP_K, bp * tp)
        pos, src, te, nu = _route_tables(eidx, n_experts, MOE_TILE)
        y = _moe(x1, src, te, nu, w_gate, w_up, w_down, l)
        xp = _combine(x1, y, pos, wts, g2, b2, alpha).reshape(bp, tp, d)

        s1, seidx, swts, sa, sb, sc = _sample_mixer(
            xs, state_conv_a[l].reshape(bs, -1), state_pool_b[l].reshape(bs, -1),
            state_conv_c[l].reshape(bs, -1), params, dims, alpha, bs, ts)
        spos, ssrc, ste, snu = _route_tables(seidx, n_experts, MOE_TILE)
        ys = _moe(s1, ssrc, ste, snu, w_gate, w_up, w_down, l)
        xs = _combine(s1, ys, spos, swts, g2, b2, alpha)

        for lst, val in zip(new_states, (pa, sa.reshape(bs, ka - 1, w_a), pb, sb.reshape(bs, kb, w_b),
                                         pc, sc.reshape(bs, kc - 1, w_c))):
            lst.append(val)

    y_prompt = xp
    y_sample = xs.reshape(ts, bs, d).transpose(1, 0, 2)
    return (y_prompt, y_sample) + tuple(jnp.stack(s) for s in new_states)
```

```python
import functools

import jax
import jax.numpy as jnp
from jax import lax
from jax.experimental import pallas as pl
from jax.experimental.pallas import tpu as pltpu

F32 = jnp.float32
BF16 = jnp.bfloat16
I32 = jnp.int32

POOL_WINDOWS = (2, 4, 8, 16)
TOP_K = 2
LN_EPS = 1e-5
PAST_LEN = 16384

LANES = 128
SUBLANES = 8
SUBLANE_SHIFT = 3
VMEM_LIMIT_BYTES = 56 * 1024 * 1024

ROW_CHUNK = 32
PROMPT_TILE = 256
MOE_TILE = 256
COMBINE_TILE = 256
GATHER_UNROLL = 16
GATHER_PITCH = 9
ROUTER_GROUP_ROW = 0
ROUTER_EXPERT_ROW = 8


def _dot(a, b):
    return jnp.dot(a, b, preferred_element_type=F32)


def _layer_norm(x, g, b):
    mu = jnp.mean(x, axis=-1, keepdims=True)
    xc = x - mu
    var = jnp.mean(xc * xc, axis=-1, keepdims=True)
    return xc * lax.rsqrt(var + LN_EPS) * g + b


def _sigmoid(x):
    return 1.0 / (1.0 + jnp.exp(-x))


def _store_row_tiles(ref, val):
    rows, d = val.shape
    for j in range(d // LANES):
        ref[pl.ds(j, rows, stride=SUBLANES), :] = val[:, j * LANES:(j + 1) * LANES]


def _load_row_tiles(ref, rows, first=0, pitch=SUBLANES):
    return jnp.concatenate(
        [ref[pl.ds(first + j, rows, stride=pitch), :] for j in range(SUBLANES)], axis=-1)


def _mixer_rows(x_ref, p, bufs, outs, *, rows, stride, new_a, new_b, new_c, pos_of_row,
                alpha, n_groups, n_experts):
    (w_in, b_in, wa, ba, lnag, lnab, wbd, pscale, wc, w_out, b_out, ln1g, ln1b,
     wr, br) = p
    proj, buf_a, buf_b, buf_c, dbuf, cat = bufs
    x1_ref, eidx_ref, wts_ref = outs
    ka = wa.shape[0]
    kc = wc.shape[0]
    w_a = wa.shape[1]
    w_b = pscale.shape[1]
    w_c = wc.shape[1]
    o_gate, o_ub, o_cbg, o_ccg, o_ch = w_a, 2 * w_a, 2 * w_a + w_b, 2 * w_a + w_b + w_c, 2 * w_a + w_b + 2 * w_c

    x = x_ref[...]
    proj[...] = _dot(x.astype(BF16), w_in[...]) + b_in[...]

    buf_a[pl.ds(new_a, rows), :] = proj[:, 0:w_a] * _sigmoid(proj[:, o_gate:o_gate + w_a])
    buf_b[pl.ds(new_b, rows), :] = proj[:, o_ub:o_ub + w_b]
    buf_c[pl.ds(new_c, rows), :] = proj[:, o_ccg:o_ccg + w_c] * proj[:, o_ch:o_ch + w_c]

    lane_b = lax.broadcasted_iota(I32, (ROW_CHUNK, w_b), 1)
    group_b = lane_b // (w_b // len(POOL_WINDOWS))
    win_b = jnp.zeros((ROW_CHUNK, w_b), I32)
    for g, w in enumerate(POOL_WINDOWS):
        win_b = jnp.where(group_b == g, w, win_b)

    def chunk(c, carry):
        r0 = c * ROW_CHUNK if isinstance(c, int) else pl.multiple_of(c * ROW_CHUNK, ROW_CHUNK)
        acc = jnp.zeros((ROW_CHUNK, w_a), F32) + ba[...]
        for k in range(ka):
            src = buf_a[pl.ds(r0 + (new_a - (ka - 1 - k) * stride), ROW_CHUNK), :]
            acc = acc + wa[k:k + 1, :] * src
        a = _layer_norm(acc, lnag[...], lnab[...])
        cat[pl.ds(r0, ROW_CHUNK), 0:w_a] = (a * _sigmoid(a)).astype(BF16)
        cur = buf_b[pl.ds(r0 + new_b, ROW_CHUNK), :]
        run = cur
        wsum = jnp.zeros((ROW_CHUNK, w_b), F32)
        for j in range(1, max(POOL_WINDOWS) + 1):
            if j in POOL_WINDOWS:
                wsum = jnp.where(win_b == j, run, wsum)
            if j < max(POOL_WINDOWS):
                run = run + buf_b[pl.ds(r0 + (new_b - j * stride), ROW_CHUNK), :]
        pos = pos_of_row(r0 + lax.broadcasted_iota(I32, (ROW_CHUNK, w_b), 0))
        cnt = jnp.minimum(pos + 1, win_b).astype(F32)
        dbuf[pl.ds(r0, ROW_CHUNK), :] = (wsum / cnt - cur).astype(BF16)
        accc = jnp.zeros((ROW_CHUNK, w_c), F32)
        for k in range(kc):
            src = buf_c[pl.ds(r0 + (new_c - (kc - 1 - k) * stride), ROW_CHUNK), :]
            accc = accc + wc[k:k + 1, :] * src
        cbg = proj[pl.ds(r0, ROW_CHUNK), o_cbg:o_cbg + w_c]
        cat[pl.ds(r0, ROW_CHUNK), w_a + w_b:w_a + w_b + w_c] = (cbg * accc).astype(BF16)
        return carry

    if stride % SUBLANES == 0:
        lax.fori_loop(0, rows // ROW_CHUNK, chunk, 0)
    else:
        for c in range(rows // ROW_CHUNK):
            chunk(c, 0)

    cat[:, w_a:w_a + w_b] = (_dot(dbuf[...], wbd[...]) * pscale[...]).astype(BF16)

    m = _dot(cat[...], w_out[...]) + b_out[...]
    x1 = _layer_norm(alpha * x + m, ln1g[...], ln1b[...])
    _store_row_tiles(x1_ref, x1)

    x1_hi = x1.astype(BF16)
    x1_lo = (x1 - x1_hi.astype(F32)).astype(BF16)
    p_hi = _dot(x1_hi, wr[...])
    p_lo = _dot(x1_lo, wr[...])
    logits = (p_hi[:, 0:LANES] + p_hi[:, LANES:2 * LANES]) + (p_lo[:, 0:LANES] + p_lo[:, LANES:2 * LANES])
    lt = jnp.transpose(logits + br[...])

    neg = jnp.float32(-jnp.inf)
    gl = lt[ROUTER_GROUP_ROW:ROUTER_GROUP_ROW + SUBLANES, :]
    grow = lax.broadcasted_iota(I32, gl.shape, 0)
    gvalid = grow < n_groups
    glm = jnp.where(gvalid, gl, neg)
    gmax = jnp.max(glm, axis=0, keepdims=True)
    gidx = jnp.min(jnp.where(glm == gmax, grow, SUBLANES), axis=0, keepdims=True)
    gsum = jnp.sum(jnp.where(gvalid, jnp.exp(gl - gmax), 0.0), axis=0, keepdims=True)
    g_p = 1.0 / gsum

    el = lt[ROUTER_EXPERT_ROW:ROUTER_EXPERT_ROW + n_experts, :]
    erow = lax.broadcasted_iota(I32, el.shape, 0)
    v = jnp.where(erow // (n_experts // n_groups) == gidx, el, neg)
    v1 = jnp.max(v, axis=0, keepdims=True)
    i1 = jnp.min(jnp.where(v == v1, erow, n_experts), axis=0, keepdims=True)
    vv = jnp.where(erow == i1, neg, v)
    v2 = jnp.max(vv, axis=0, keepdims=True)
    i2 = jnp.min(jnp.where(vv == v2, erow, n_experts), axis=0, keepdims=True)
    e2 = jnp.exp(v2 - v1)
    den = 1.0 + e2
    eidx_ref[0:1, :] = i1
    eidx_ref[1:2, :] = i2
    wts_ref[0:1, :] = (1.0 / den) * g_p
    wts_ref[1:2, :] = (e2 / den) * g_p


def _prompt_mixer_kernel(x_ref, *refs, n_params, n_seq, extra_tiles, tile, hist, alpha, n_groups,
                         n_experts):
    p = refs[:n_params]
    x1_ref, eidx_ref, wts_ref, sa_ref, sb_ref, sc_ref = refs[n_params:n_params + 6]
    bufs = refs[n_params + 6:]
    _, buf_a, buf_b, buf_c, _, _ = bufs
    ha, hb, hc = hist
    s = pl.program_id(0)
    t = pl.program_id(1)

    @pl.when(s < n_seq)
    def _():
        @pl.when(t == 0)
        def _():
            buf_a[0:ha, :] = jnp.zeros((ha, buf_a.shape[1]), F32)
            buf_b[0:hb, :] = jnp.zeros((hb, buf_b.shape[1]), F32)
            buf_c[0:hc, :] = jnp.zeros((hc, buf_c.shape[1]), F32)

        _mixer_rows(x_ref, p, bufs, (x1_ref, eidx_ref, wts_ref), rows=tile, stride=1,
                    new_a=ha, new_b=hb, new_c=hc,
                    pos_of_row=lambda r: r + t * tile,
                    alpha=alpha, n_groups=n_groups, n_experts=n_experts)

        buf_a[0:ha, :] = buf_a[tile:tile + ha, :]
        buf_b[0:hb, :] = buf_b[tile:tile + hb, :]
        buf_c[0:hc, :] = buf_c[tile:tile + hc, :]

        @pl.when(t == pl.num_programs(1) - 1)
        def _():
            na, nb, nc = sa_ref.shape[0], sb_ref.shape[0], sc_ref.shape[0]
            sa_ref[...] = buf_a[ha - na:ha, :]
            sb_ref[...] = buf_b[hb - nb:hb, :]
            sc_ref[...] = buf_c[hc - nc:hc, :]

    @pl.when(jnp.logical_and(s == n_seq, t < extra_tiles))
    def _():
        x1_ref[...] = jnp.zeros(x1_ref.shape, F32)


def _sample_mixer_kernel(x1_all_hbm, x_ref, sta_ref, stb_ref, stc_ref, *refs, n_params, batch, steps,
                         alpha, n_groups, n_experts):
    del x1_all_hbm
    p = refs[:n_params]
    x1_ref, eidx_ref, wts_ref, sa_ref, sb_ref, sc_ref = refs[n_params:n_params + 6]
    bufs = refs[n_params + 6:]
    _, buf_a, buf_b, buf_c, _, _ = bufs
    w_a, w_b, w_c = buf_a.shape[1], buf_b.shape[1], buf_c.shape[1]
    na, nb, nc = sta_ref.shape[1] // w_a, stb_ref.shape[1] // w_b, stc_ref.shape[1] // w_c
    rows = batch * steps

    for j in range(na):
        buf_a[j * batch:(j + 1) * batch, :] = sta_ref[:, j * w_a:(j + 1) * w_a]
    for j in range(nb):
        buf_b[j * batch:(j + 1) * batch, :] = stb_ref[:, j * w_b:(j + 1) * w_b]
    for j in range(nc):
        buf_c[j * batch:(j + 1) * batch, :] = stc_ref[:, j * w_c:(j + 1) * w_c]

    _mixer_rows(x_ref, p, bufs, (x1_ref, eidx_ref, wts_ref), rows=rows, stride=batch,
                new_a=na * batch, new_b=nb * batch, new_c=nc * batch,
                pos_of_row=lambda r: PAST_LEN + r // batch,
                alpha=alpha, n_groups=n_groups, n_experts=n_experts)

    for j in range(na):
        sa_ref[:, j * w_a:(j + 1) * w_a] = buf_a[(j + steps) * batch:(j + steps + 1) * batch, :]
    for j in range(nb):
        sb_ref[:, j * w_b:(j + 1) * w_b] = buf_b[(j + steps) * batch:(j + steps + 1) * batch, :]
    for j in range(nc):
        sc_ref[:, j * w_c:(j + 1) * w_c] = buf_c[(j + steps) * batch:(j + steps + 1) * batch, :]


def _full_spec(a):
    nd = a.ndim
    return pl.BlockSpec(a.shape, lambda *_: (0,) * nd)


def _mixer_scratch(rows, ha, hb, hc, d_in, w_a, w_b, w_c):
    return [
        pltpu.VMEM((rows, d_in), F32),
        pltpu.VMEM((ha + rows, w_a), F32),
        pltpu.VMEM((hb + rows, w_b), F32),
        pltpu.VMEM((hc + rows, w_c), F32),
        pltpu.VMEM((rows, w_b), BF16),
        pltpu.VMEM((rows, w_a + w_b + w_c), BF16),
    ]


def _round_up(n, m):
    return (n + m - 1) // m * m


def _prompt_mixer(x, b, t, extra_rows, params, dims, alpha):
    d = x.shape[1]
    w_a, w_b, w_c, ka, kb, kc, n_groups, n_experts = dims
    tile = PROMPT_TILE
    nt = t // tile
    ha, hb, hc = _round_up(ka - 1, SUBLANES), _round_up(kb, SUBLANES), _round_up(kc - 1, SUBLANES)
    d_in = params[0].shape[1]
    extra_tiles = extra_rows // tile
    assert extra_rows % tile == 0 and extra_tiles <= nt
    kern = functools.partial(_prompt_mixer_kernel, n_params=len(params), n_seq=b,
                             extra_tiles=extra_tiles, tile=tile, hist=(ha, hb, hc), alpha=alpha,
                             n_groups=n_groups, n_experts=n_experts)
    last_block = b * nt + extra_tiles - 1

    def seq_tile(i, j):
        return jnp.minimum(i, b - 1), jnp.where(i < b, j, nt - 1)

    def row_block(i, j):
        s, t_ = seq_tile(i, j)
        return s * nt + t_

    out_shape = (
        jax.ShapeDtypeStruct(((b * t + extra_rows) * SUBLANES, LANES), F32),
        jax.ShapeDtypeStruct((b, TOP_K, t), I32),
        jax.ShapeDtypeStruct((b, TOP_K, t), F32),
        jax.ShapeDtypeStruct((b, ka - 1, w_a), F32),
        jax.ShapeDtypeStruct((b, kb, w_b), F32),
        jax.ShapeDtypeStruct((b, kc - 1, w_c), F32),
    )
    out_specs = (
        pl.BlockSpec((tile * SUBLANES, LANES), lambda i, j: (jnp.minimum(i * nt + j, last_block), 0)),
        pl.BlockSpec((None, TOP_K, tile), lambda i, j: (seq_tile(i, j)[0], 0, seq_tile(i, j)[1])),
        pl.BlockSpec((None, TOP_K, tile), lambda i, j: (seq_tile(i, j)[0], 0, seq_tile(i, j)[1])),
        pl.BlockSpec((None, ka - 1, w_a), lambda i, j: (jnp.minimum(i, b - 1), 0, 0)),
        pl.BlockSpec((None, kb, w_b), lambda i, j: (jnp.minimum(i, b - 1), 0, 0)),
        pl.BlockSpec((None, kc - 1, w_c), lambda i, j: (jnp.minimum(i, b - 1), 0, 0)),
    )
    in_specs = [pl.BlockSpec((tile, d), lambda i, j: (row_block(i, j), 0))] + [_full_spec(a) for a in params]
    return pl.pallas_call(
        kern,
        grid=(b + 1, nt),
        in_specs=in_specs,
        out_specs=out_specs,
        out_shape=out_shape,
        scratch_shapes=_mixer_scratch(tile, ha, hb, hc, d_in, w_a, w_b, w_c),
        compiler_params=pltpu.CompilerParams(
            dimension_semantics=("arbitrary", "arbitrary"),
            vmem_limit_bytes=VMEM_LIMIT_BYTES),
        name="prompt_mixer",
    )(x, *params)


def _sample_mixer(x1_all, x_tm, st_a, st_b, st_c, params, dims, alpha, batch, steps):
    rows, d = x_tm.shape
    w_a, w_b, w_c, ka, kb, kc, n_groups, n_experts = dims
    d_in = params[0].shape[1]
    first = x1_all.shape[0] // SUBLANES - rows
    assert first % rows == 0
    kern = functools.partial(_sample_mixer_kernel, n_params=len(params), batch=batch,
                             steps=steps, alpha=alpha, n_groups=n_groups, n_experts=n_experts)
    out_shape = (
        jax.ShapeDtypeStruct(x1_all.shape, F32),
        jax.ShapeDtypeStruct((TOP_K, rows), I32),
        jax.ShapeDtypeStruct((TOP_K, rows), F32),
        jax.ShapeDtypeStruct(st_a.shape, F32),
        jax.ShapeDtypeStruct(st_b.shape, F32),
        jax.ShapeDtypeStruct(st_c.shape, F32),
    )
    args = (x_tm, st_a, st_b, st_c) + tuple(params)
    out_specs = (pl.BlockSpec((rows * SUBLANES, LANES), lambda i: (first // rows, 0)),) + tuple(
        pl.BlockSpec(s.shape, lambda i, n=len(s.shape): (0,) * n) for s in out_shape[1:])
    return pl.pallas_call(
        kern,
        grid=(1,),
        in_specs=[pl.BlockSpec(memory_space=pl.ANY)] + [_full_spec(a) for a in args],
        out_specs=out_specs,
        out_shape=out_shape,
        input_output_aliases={0: 0},
        scratch_shapes=_mixer_scratch(rows, (ka - 1) * batch, kb * batch, (kc - 1) * batch,
                                      d_in, w_a, w_b, w_c),
        compiler_params=pltpu.CompilerParams(
            dimension_semantics=("arbitrary",),
            vmem_limit_bytes=VMEM_LIMIT_BYTES),
        name="sample_mixer",
    )(x1_all, *args)


def _start_row_gather(src_hbm, idx_ref, dst_vmem, n, sem):
    def body(c, carry):
        for u in range(GATHER_UNROLL):
            s = c * GATHER_UNROLL + u
            row = idx_ref[0, 0, s]
            pltpu.make_async_copy(
                src_hbm.at[pl.ds(pl.multiple_of(row * SUBLANES, SUBLANES), SUBLANES)],
                dst_vmem.at[pl.ds(s * GATHER_PITCH, SUBLANES)],
                sem).start()
        return carry
    lax.fori_loop(0, n // GATHER_UNROLL, body, 0)


def _wait_row_gather(src_hbm, dst_vmem, n, sem):
    pltpu.make_async_copy(src_hbm.at[pl.ds(0, n * SUBLANES)], dst_vmem.at[pl.ds(0, n * SUBLANES)],
                          sem).wait()


def _moe_kernel(te_ref, nu_ref, src0_ref, src1_ref, x_hbm, wg_ref, wu_ref, wd_ref, y_ref,
                xbuf, wg_b, wu_b, wd_b, sem, *, tile):
    i = pl.program_id(0)
    n_used = nu_ref[0]
    slot = i % 2

    @pl.when(i == 0)
    def _():
        _start_row_gather(x_hbm, src0_ref, xbuf.at[0], tile, sem.at[0])

    @pl.when(i + 1 < n_used)
    def _():
        _start_row_gather(x_hbm, src1_ref, xbuf.at[1 - slot], tile, sem.at[1 - slot])

    @pl.when(i < n_used)
    def _():
        @pl.when(jnp.logical_or(i == 0, te_ref[i] != te_ref[jnp.maximum(i - 1, 0)]))
        def _():
            wg_b[...] = wg_ref[...].astype(BF16)
            wu_b[...] = wu_ref[...].astype(BF16)
            wd_b[...] = wd_ref[...].astype(BF16)

        _wait_row_gather(x_hbm, xbuf.at[slot], tile, sem.at[slot])
        xb = _load_row_tiles(xbuf.at[slot], tile, pitch=GATHER_PITCH).astype(BF16)
        hg = _dot(xb, wg_b[...])
        hu = _dot(xb, wu_b[...])
        h = hg * _sigmoid(hg) * hu
        _store_row_tiles(y_ref, _dot(h.astype(BF16), wd_b[...]))

    @pl.when(i >= n_used)
    def _():
        y_ref[...] = jnp.zeros(y_ref.shape, F32)


def _moe(x1, src, tile_expert, n_used, w_gate, w_up, w_down, layer):
    d = w_gate.shape[-2]
    tile = MOE_TILE
    n_tiles = src.shape[0] // tile
    d_e = w_gate.shape[-1]
    grid_spec = pltpu.PrefetchScalarGridSpec(
        num_scalar_prefetch=2,
        grid=(n_tiles,),
        in_specs=[
            pl.BlockSpec((1, 1, tile), lambda i, te, nu: (i, 0, 0), memory_space=pltpu.SMEM),
            pl.BlockSpec((1, 1, tile), lambda i, te, nu: (jnp.minimum(i + 1, n_tiles - 1), 0, 0),
                         memory_space=pltpu.SMEM),
            pl.BlockSpec(memory_space=pl.ANY),
            pl.BlockSpec((None, None, d, d_e), lambda i, te, nu: (layer, te[i], 0, 0)),
            pl.BlockSpec((None, None, d, d_e), lambda i, te, nu: (layer, te[i], 0, 0)),
            pl.BlockSpec((None, None, d_e, d), lambda i, te, nu: (layer, te[i], 0, 0)),
        ],
        out_specs=pl.BlockSpec((tile * SUBLANES, LANES), lambda i, te, nu: (i, 0)),
        scratch_shapes=[
            pltpu.VMEM((2, tile * GATHER_PITCH, LANES), F32),
            pltpu.VMEM((d, d_e), BF16),
            pltpu.VMEM((d, d_e), BF16),
            pltpu.VMEM((d_e, d), BF16),
            pltpu.SemaphoreType.DMA((2,)),
        ],
    )
    src3 = src.reshape(n_tiles, 1, tile)
    return pl.pallas_call(
        functools.partial(_moe_kernel, tile=tile),
        grid_spec=grid_spec,
        out_shape=jax.ShapeDtypeStruct((n_tiles * tile * SUBLANES, LANES), F32),
        compiler_params=pltpu.CompilerParams(
            dimension_semantics=("arbitrary",),
            vmem_limit_bytes=VMEM_LIMIT_BYTES),
        name="moe",
    )(tile_expert, n_used, src3, src3, x1, w_gate, w_up, w_down)


def _combine_kernel(pos0_ref, pos1_ref, x1_ref, w_ref, g_ref, b_ref, y_hbm, op_ref, os_ref, ybuf, sem,
                    *, tile, alpha, prompt_tiles):
    i = pl.program_id(0)
    rows = TOP_K * tile
    slot = i % 2

    @pl.when(i == 0)
    def _():
        _start_row_gather(y_hbm, pos0_ref, ybuf.at[0], rows, sem.at[0])

    @pl.when(i + 1 < pl.num_programs(0))
    def _():
        _start_row_gather(y_hbm, pos1_ref, ybuf.at[1 - slot], rows, sem.at[1 - slot])

    _wait_row_gather(y_hbm, ybuf.at[slot], rows, sem.at[slot])
    w = w_ref[...]
    y_a = _load_row_tiles(ybuf.at[slot], tile, pitch=GATHER_PITCH)
    y_b = _load_row_tiles(ybuf.at[slot], tile, first=tile * GATHER_PITCH, pitch=GATHER_PITCH)
    moe = w[:, 0:1] * y_a + w[:, 1:2] * y_b
    x1 = _load_row_tiles(x1_ref, tile)
    out = _layer_norm(alpha * x1 + moe, g_ref[...], b_ref[...])

    @pl.when(i < prompt_tiles)
    def _():
        op_ref[...] = out

    @pl.when(i >= prompt_tiles)
    def _():
        os_ref[...] = out


def _combine(x1, y, pos, wts, g, b, alpha, n_prompt):
    n, d = x1.shape[0] // SUBLANES, g.shape[-1]
    tile = COMBINE_TILE
    n_tiles = n // tile
    prompt_tiles = n_prompt // tile
    assert n % tile == 0 and n_prompt % tile == 0 and 0 < prompt_tiles < n_tiles
    pos_t = pos.reshape(TOP_K, n_tiles, tile).transpose(1, 0, 2).reshape(n_tiles, 1, TOP_K * tile)
    return pl.pallas_call(
        functools.partial(_combine_kernel, tile=tile, alpha=alpha, prompt_tiles=prompt_tiles),
        grid=(n_tiles,),
        in_specs=[
            pl.BlockSpec((1, 1, TOP_K * tile), lambda i: (i, 0, 0), memory_space=pltpu.SMEM),
            pl.BlockSpec((1, 1, TOP_K * tile), lambda i: (jnp.minimum(i + 1, n_tiles - 1), 0, 0),
                         memory_space=pltpu.SMEM),
            pl.BlockSpec((tile * SUBLANES, LANES), lambda i: (i, 0)),
            pl.BlockSpec((tile, TOP_K), lambda i: (i, 0)),
            _full_spec(g),
            _full_spec(b),
            pl.BlockSpec(memory_space=pl.ANY),
        ],
        out_specs=(
            pl.BlockSpec((tile, d), lambda i: (jnp.minimum(i, prompt_tiles - 1), 0)),
            pl.BlockSpec((tile, d), lambda i: (jnp.maximum(i - prompt_tiles, 0), 0)),
        ),
        out_shape=(jax.ShapeDtypeStruct((n_prompt, d), F32),
                   jax.ShapeDtypeStruct((n - n_prompt, d), F32)),
        scratch_shapes=[pltpu.VMEM((2, TOP_K * tile * GATHER_PITCH, LANES), F32),
                        pltpu.SemaphoreType.DMA((2,))],
        compiler_params=pltpu.CompilerParams(
            dimension_semantics=("arbitrary",),
            vmem_limit_bytes=VMEM_LIMIT_BYTES),
        name="combine",
    )(pos_t, pos_t, x1, wts.T, g, b, y)


def _route_tables(eidx, n_experts, tile):
    k, n = eidx.shape
    pairs = k * n
    n_tiles = pairs // tile + n_experts
    e = eidx.reshape(pairs)
    onehot = (e[:, None] == jnp.arange(n_experts, dtype=I32)[None, :]).astype(I32)
    csum = jnp.cumsum(onehot, axis=0)
    rank = jnp.sum(onehot * csum, axis=1) - 1
    counts = csum[-1]
    ptiles = (counts + tile - 1) // tile
    tile_end = jnp.cumsum(ptiles)
    offs = (tile_end - ptiles) * tile
    pos = jnp.sum(onehot * offs[None, :], axis=1) + rank
    n_used = tile_end[-1]
    j = jnp.arange(n_tiles, dtype=I32)
    te = jnp.sum((j[:, None] >= tile_end[None, :]).astype(I32), axis=1)
    te_last = jnp.sum((n_used - 1 >= tile_end).astype(I32))
    te = jnp.where(j < n_used, te, te_last).astype(I32)
    tok = jnp.tile(jnp.arange(n, dtype=I32), k)
    src = jnp.zeros((n_tiles * tile,), I32).at[pos].set(tok, unique_indices=True)
    return pos.reshape(k, n), src, te, n_used.reshape(1).astype(I32)


def _layer_params(l, w_in_b, b_in, conv_a_w, conv_a_b, ln_a_g, ln_a_b, w_pool_bd, pool_scale,
                  conv_c_w, w_out_b, b_out, ln1_g, ln1_b, w_router, b_router):
    row = lambda a: a[l][None, :]
    return (w_in_b[l], row(b_in), conv_a_w[l], row(conv_a_b), row(ln_a_g), row(ln_a_b),
            w_pool_bd[l], row(pool_scale), conv_c_w[l], w_out_b[l], row(b_out), row(ln1_g),
            row(ln1_b), w_router[l], b_router[l])


def kernel(x_prompt, x_sample, state_conv_a, state_pool_b, state_conv_c, w_in, b_in, conv_a_w, conv_a_b, ln_a_g, ln_a_b, w_pool, pool_scale, conv_c_w, w_out, b_out, ln1_g, ln1_b, w_router_group, b_router_group, w_router_expert, b_router_expert, w_gate, w_up, w_down, ln2_g, ln2_b):
    depth = w_in.shape[0]
    bp, tp, d = x_prompt.shape
    bs, ts, _ = x_sample.shape
    ka, w_a = conv_a_w.shape[1:]
    kb, w_b = state_pool_b.shape[2:]
    kc, w_c = conv_c_w.shape[1:]
    n_groups = w_router_group.shape[-1]
    n_experts = w_router_expert.shape[-1]
    dims = (w_a, w_b, w_c, ka, kb, kc, n_groups, n_experts)
    alpha = float((2 * depth) ** 0.25)
    assert tp % PROMPT_TILE == 0 and PROMPT_TILE % ROW_CHUNK == 0 and (bs * ts) % ROW_CHUNK == 0
    assert n_groups <= SUBLANES and ROUTER_EXPERT_ROW + n_experts <= LANES
    assert d == SUBLANES * LANES

    w_in_b = w_in.astype(BF16)
    w_out_b = w_out.astype(BF16)
    n_pg, pg = w_pool.shape[1], w_pool.shape[2]
    eye = jnp.eye(n_pg, dtype=F32)
    w_pool_bd = (w_pool[:, :, :, None, :] * eye[None, :, None, :, None]).reshape(depth, n_pg * pg, n_pg * pg).astype(BF16)
    wr = jnp.zeros((depth, d, LANES), F32)
    wr = wr.at[:, :, ROUTER_GROUP_ROW:ROUTER_GROUP_ROW + n_groups].set(w_router_group)
    wr = wr.at[:, :, ROUTER_EXPERT_ROW:ROUTER_EXPERT_ROW + n_experts].set(w_router_expert)
    wr_hi = wr.astype(BF16)
    wr_lo = (wr - wr_hi.astype(F32)).astype(BF16)
    w_router = jnp.concatenate([wr_hi, wr_lo], axis=-1)
    b_router = jnp.zeros((depth, 1, LANES), F32)
    b_router = b_router.at[:, 0, ROUTER_GROUP_ROW:ROUTER_GROUP_ROW + n_groups].set(b_router_group)
    b_router = b_router.at[:, 0, ROUTER_EXPERT_ROW:ROUTER_EXPERT_ROW + n_experts].set(b_router_expert)

    xp = x_prompt.reshape(bp * tp, d)
    xs = x_sample.transpose(1, 0, 2).reshape(ts * bs, d)
    new_states = [[] for _ in range(6)]
    for l in range(depth):
        params = _layer_params(l, w_in_b, b_in, conv_a_w, conv_a_b, ln_a_g, ln_a_b, w_pool_bd,
                               pool_scale, conv_c_w, w_out_b, b_out, ln1_g, ln1_b, w_router, b_router)
        g2, b2 = ln2_g[l][None, :], ln2_b[l][None, :]

        x1, eidx, wts, pa, pb, pc = _prompt_mixer(xp, bp, tp, bs * ts, params, dims, alpha)
        x1, seidx, swts, sa, sb, sc = _sample_mixer(
            x1, xs, state_conv_a[l].reshape(bs, -1), state_pool_b[l].reshape(bs, -1),
            state_conv_c[l].reshape(bs, -1), params, dims, alpha, bs, ts)
        eidx = jnp.concatenate([eidx.transpose(1, 0, 2).reshape(TOP_K, bp * tp), seidx], axis=1)
        wts = jnp.concatenate([wts.transpose(1, 0, 2).reshape(TOP_K, bp * tp), swts], axis=1)

        pos, src, te, nu = _route_tables(eidx, n_experts, MOE_TILE)
        y = _moe(x1, src, te, nu, w_gate, w_up, w_down, l)
        xp, xs = _combine(x1, y, pos, wts, g2, b2, alpha, bp * tp)

        for lst, val in zip(new_states, (pa, sa.reshape(bs, ka - 1, w_a), pb, sb.reshape(bs, kb, w_b),
                                         pc, sc.reshape(bs, kc - 1, w_c))):
            lst.append(val)

    y_prompt = xp.reshape(bp, tp, d)
    y_sample = xs.reshape(ts, bs, d).transpose(1, 0, 2)
    return (y_prompt, y_sample) + tuple(jnp.stack(s) for s in new_states)
```

```python
import functools

import jax
import jax.numpy as jnp
from jax import lax
from jax.experimental import pallas as pl
from jax.experimental.pallas import tpu as pltpu

F32 = jnp.float32
BF16 = jnp.bfloat16
I32 = jnp.int32

POOL_WINDOWS = (2, 4, 8, 16)
TOP_K = 2
LN_EPS = 1e-5
PAST_LEN = 16384

LANES = 128
SUBLANES = 8
SUBLANE_SHIFT = 3
VMEM_LIMIT_BYTES = 56 * 1024 * 1024

ROW_CHUNK = 32
PROMPT_TILE = 256
PROMPT_HIST_PITCH = 2
MOE_TILE = 256
COMBINE_TILE = 256
GATHER_UNROLL = 16
DMA_PRIORITIES = 2
GATHER_PITCH = 9
ROUTER_GROUP_ROW = 0
ROUTER_EXPERT_ROW = 8


def _dot(a, b):
    return jnp.dot(a, b, preferred_element_type=F32)


def _layer_norm(x, g, b):
    mu = jnp.mean(x, axis=-1, keepdims=True)
    xc = x - mu
    var = jnp.mean(xc * xc, axis=-1, keepdims=True)
    return xc * lax.rsqrt(var + LN_EPS) * g + b


def _sigmoid(x):
    return 1.0 / (1.0 + jnp.exp(-x))


def _store_row_tiles(ref, val):
    rows, d = val.shape
    for j in range(d // LANES):
        ref[pl.ds(j, rows, stride=SUBLANES), :] = val[:, j * LANES:(j + 1) * LANES]


def _load_row_tiles(ref, rows, first=0, pitch=SUBLANES):
    return jnp.concatenate(
        [ref[pl.ds(first + j, rows, stride=pitch), :] for j in range(SUBLANES)], axis=-1)


def _hist_load(buf, start, n, pitch):
    parts = []
    for j in range(buf.shape[0]):
        if pitch == 1:
            parts.append(buf[j, pl.ds(start, n), :])
        else:
            parts.append(buf.at[j][pl.ds(pitch * start, n, stride=pitch), :])
    return parts[0] if len(parts) == 1 else jnp.concatenate(parts, axis=-1)


def _hist_store(buf, start, n, pitch, val):
    for j in range(buf.shape[0]):
        piece = val[:, j * LANES:(j + 1) * LANES]
        if pitch == 1:
            buf[j, pl.ds(start, n), :] = piece
        else:
            buf.at[j][pl.ds(pitch * start, n, stride=pitch), :] = piece


def _mixer_rows(x_ref, p, bufs, outs, *, rows, stride, pitch, new_a, new_b, new_c, pos_of_row,
                alpha, n_groups, n_experts):
    (w_in, b_in, wa, ba, lnag, lnab, wbd, pscale, wc, w_out, b_out, ln1g, ln1b,
     wr, br) = p
    proj, buf_a, buf_b, buf_c, dbuf, cat = bufs
    x1_ref, eidx_ref, wts_ref = outs
    ka = wa.shape[0]
    kc = wc.shape[0]
    w_a = wa.shape[1]
    w_b = pscale.shape[1]
    w_c = wc.shape[1]
    o_gate, o_ub, o_cbg, o_ccg, o_ch = w_a, 2 * w_a, 2 * w_a + w_b, 2 * w_a + w_b + w_c, 2 * w_a + w_b + 2 * w_c

    x = x_ref[...]
    proj[...] = _dot(x.astype(BF16), w_in[...]) + b_in[...]

    _hist_store(buf_a, new_a, rows, pitch, proj[:, 0:w_a] * _sigmoid(proj[:, o_gate:o_gate + w_a]))
    _hist_store(buf_b, new_b, rows, pitch, proj[:, o_ub:o_ub + w_b])
    _hist_store(buf_c, new_c, rows, pitch, proj[:, o_ccg:o_ccg + w_c] * proj[:, o_ch:o_ch + w_c])

    lane_b = lax.broadcasted_iota(I32, (ROW_CHUNK, w_b), 1)
    group_b = lane_b // (w_b // len(POOL_WINDOWS))
    win_b = jnp.zeros((ROW_CHUNK, w_b), I32)
    for g, w in enumerate(POOL_WINDOWS):
        win_b = jnp.where(group_b == g, w, win_b)

    def chunk(c, carry):
        r0 = c * ROW_CHUNK if isinstance(c, int) else pl.multiple_of(c * ROW_CHUNK, ROW_CHUNK)
        acc = jnp.zeros((ROW_CHUNK, w_a), F32) + ba[...]
        for k in range(ka):
            src = _hist_load(buf_a, r0 + (new_a - (ka - 1 - k) * stride), ROW_CHUNK, pitch)
            acc = acc + wa[k:k + 1, :] * src
        a = _layer_norm(acc, lnag[...], lnab[...])
        cat[pl.ds(r0, ROW_CHUNK), 0:w_a] = (a * _sigmoid(a)).astype(BF16)
        cur = _hist_load(buf_b, r0 + new_b, ROW_CHUNK, pitch)
        run = cur
        wsum = jnp.zeros((ROW_CHUNK, w_b), F32)
        for j in range(1, max(POOL_WINDOWS) + 1):
            if j in POOL_WINDOWS:
                wsum = jnp.where(win_b == j, run, wsum)
            if j < max(POOL_WINDOWS):
                run = run + _hist_load(buf_b, r0 + (new_b - j * stride), ROW_CHUNK, pitch)
        pos = pos_of_row(r0 + lax.broadcasted_iota(I32, (ROW_CHUNK, w_b), 0))
        cnt = jnp.minimum(pos + 1, win_b).astype(F32)
        dbuf[pl.ds(r0, ROW_CHUNK), :] = (wsum / cnt - cur).astype(BF16)
        accc = jnp.zeros((ROW_CHUNK, w_c), F32)
        for k in range(kc):
            src = _hist_load(buf_c, r0 + (new_c - (kc - 1 - k) * stride), ROW_CHUNK, pitch)
            accc = accc + wc[k:k + 1, :] * src
        cbg = proj[pl.ds(r0, ROW_CHUNK), o_cbg:o_cbg + w_c]
        cat[pl.ds(r0, ROW_CHUNK), w_a + w_b:w_a + w_b + w_c] = (cbg * accc).astype(BF16)
        return carry

    if pitch == 1:
        assert stride % SUBLANES == 0
        lax.fori_loop(0, rows // ROW_CHUNK, chunk, 0)
    else:
        for c in range(rows // ROW_CHUNK):
            chunk(c, 0)

    cat[:, w_a:w_a + w_b] = (_dot(dbuf[...], wbd[...]) * pscale[...]).astype(BF16)

    m = _dot(cat[...], w_out[...]) + b_out[...]
    x1 = _layer_norm(alpha * x + m, ln1g[...], ln1b[...])
    _store_row_tiles(x1_ref, x1)

    x1_hi = x1.astype(BF16)
    x1_lo = (x1 - x1_hi.astype(F32)).astype(BF16)
    p_hi = _dot(x1_hi, wr[...])
    p_lo = _dot(x1_lo, wr[...])
    logits = (p_hi[:, 0:LANES] + p_hi[:, LANES:2 * LANES]) + (p_lo[:, 0:LANES] + p_lo[:, LANES:2 * LANES])
    lt = jnp.transpose(logits + br[...])

    neg = jnp.float32(-jnp.inf)
    gl = lt[ROUTER_GROUP_ROW:ROUTER_GROUP_ROW + SUBLANES, :]
    grow = lax.broadcasted_iota(I32, gl.shape, 0)
    gvalid = grow < n_groups
    glm = jnp.where(gvalid, gl, neg)
    gmax = jnp.max(glm, axis=0, keepdims=True)
    gidx = jnp.min(jnp.where(glm == gmax, grow, SUBLANES), axis=0, keepdims=True)
    gsum = jnp.sum(jnp.where(gvalid, jnp.exp(gl - gmax), 0.0), axis=0, keepdims=True)
    g_p = 1.0 / gsum

    el = lt[ROUTER_EXPERT_ROW:ROUTER_EXPERT_ROW + n_experts, :]
    erow = lax.broadcasted_iota(I32, el.shape, 0)
    v = jnp.where(erow // (n_experts // n_groups) == gidx, el, neg)
    v1 = jnp.max(v, axis=0, keepdims=True)
    i1 = jnp.min(jnp.where(v == v1, erow, n_experts), axis=0, keepdims=True)
    vv = jnp.where(erow == i1, neg, v)
    v2 = jnp.max(vv, axis=0, keepdims=True)
    i2 = jnp.min(jnp.where(vv == v2, erow, n_experts), axis=0, keepdims=True)
    e2 = jnp.exp(v2 - v1)
    den = 1.0 + e2
    eidx_ref[0:1, :] = i1
    eidx_ref[1:2, :] = i2
    wts_ref[0:1, :] = (1.0 / den) * g_p
    wts_ref[1:2, :] = (e2 / den) * g_p


def _prompt_mixer_kernel(x_ref, *refs, n_params, n_seq, extra_tiles, tile, hist, pitch, alpha,
                         n_groups, n_experts):
    p = refs[:n_params]
    x1_ref, eidx_ref, wts_ref, sa_ref, sb_ref, sc_ref = refs[n_params:n_params + 6]
    bufs = refs[n_params + 6:]
    _, buf_a, buf_b, buf_c, _, _ = bufs
    ha, hb, hc = hist
    s = pl.program_id(0)
    t = pl.program_id(1)

    @pl.when(s < n_seq)
    def _():
        hists = ((buf_a, ha), (buf_b, hb), (buf_c, hc))

        @pl.when(t == 0)
        def _():
            for buf, h in hists:
                _hist_store(buf, 0, h, pitch, jnp.zeros((h, buf.shape[0] * LANES), F32))

        _mixer_rows(x_ref, p, bufs, (x1_ref, eidx_ref, wts_ref), rows=tile, stride=1, pitch=pitch,
                    new_a=ha, new_b=hb, new_c=hc,
                    pos_of_row=lambda r: r + t * tile,
                    alpha=alpha, n_groups=n_groups, n_experts=n_experts)

        tails = [_hist_load(buf, tile, h, pitch) for buf, h in hists]
        for (buf, h), tail in zip(hists, tails):
            _hist_store(buf, 0, h, pitch, tail)

        @pl.when(t == pl.num_programs(1) - 1)
        def _():
            for ref, (_, h), tail in zip((sa_ref, sb_ref, sc_ref), hists, tails):
                ref[...] = tail[h - ref.shape[0]:h, :]

    @pl.when(jnp.logical_and(s == n_seq, t < extra_tiles))
    def _():
        x1_ref[...] = jnp.zeros(x1_ref.shape, F32)


def _sample_mixer_kernel(x1_all_hbm, x_ref, sta_ref, stb_ref, stc_ref, *refs, n_params, batch, steps,
                         alpha, n_groups, n_experts):
    del x1_all_hbm
    p = refs[:n_params]
    x1_ref, eidx_ref, wts_ref, sa_ref, sb_ref, sc_ref = refs[n_params:n_params + 6]
    bufs = refs[n_params + 6:]
    _, buf_a, buf_b, buf_c, _, _ = bufs
    rows = batch * steps
    hists = []
    for buf, st_ref, out_ref in ((buf_a, sta_ref, sa_ref), (buf_b, stb_ref, sb_ref),
                                 (buf_c, stc_ref, sc_ref)):
        width = buf.shape[0] * LANES
        hists.append((buf, st_ref, out_ref, width, st_ref.shape[1] // width))

    for buf, st_ref, _, width, n in hists:
        for j in range(n):
            _hist_store(buf, j * batch, batch, 1, st_ref[:, j * width:(j + 1) * width])

    na, nb, nc = (h[4] for h in hists)
    _mixer_rows(x_ref, p, bufs, (x1_ref, eidx_ref, wts_ref), rows=rows, stride=batch, pitch=1,
                new_a=na * batch, new_b=nb * batch, new_c=nc * batch,
                pos_of_row=lambda r: PAST_LEN + r // batch,
                alpha=alpha, n_groups=n_groups, n_experts=n_experts)

    for buf, _, out_ref, width, n in hists:
        for j in range(n):
            out_ref[:, j * width:(j + 1) * width] = _hist_load(buf, (j + steps) * batch, batch, 1)


def _full_spec(a):
    nd = a.ndim
    return pl.BlockSpec(a.shape, lambda *_: (0,) * nd)


def _mixer_scratch(rows, ha, hb, hc, d_in, w_a, w_b, w_c, pitch):
    return [
        pltpu.VMEM((rows, d_in), F32),
        pltpu.VMEM((w_a // LANES, pitch * (ha + rows), LANES), F32),
        pltpu.VMEM((w_b // LANES, pitch * (hb + rows), LANES), F32),
        pltpu.VMEM((w_c // LANES, pitch * (hc + rows), LANES), F32),
        pltpu.VMEM((rows, w_b), BF16),
        pltpu.VMEM((rows, w_a + w_b + w_c), BF16),
    ]


def _round_up(n, m):
    return (n + m - 1) // m * m


def _prompt_mixer(x, b, t, extra_rows, params, dims, alpha):
    d = x.shape[1]
    w_a, w_b, w_c, ka, kb, kc, n_groups, n_experts = dims
    tile = PROMPT_TILE
    nt = t // tile
    ha, hb, hc = _round_up(ka - 1, SUBLANES), _round_up(kb, SUBLANES), _round_up(kc - 1, SUBLANES)
    d_in = params[0].shape[1]
    extra_tiles = extra_rows // tile
    assert extra_rows % tile == 0 and extra_tiles <= nt
    kern = functools.partial(_prompt_mixer_kernel, n_params=len(params), n_seq=b,
                             extra_tiles=extra_tiles, tile=tile, hist=(ha, hb, hc),
                             pitch=PROMPT_HIST_PITCH, alpha=alpha,
                             n_groups=n_groups, n_experts=n_experts)
    last_block = b * nt + extra_tiles - 1

    def seq_tile(i, j):
        return jnp.minimum(i, b - 1), jnp.where(i < b, j, nt - 1)

    def row_block(i, j):
        s, t_ = seq_tile(i, j)
        return s * nt + t_

    out_shape = (
        jax.ShapeDtypeStruct(((b * t + extra_rows) * SUBLANES, LANES), F32),
        jax.ShapeDtypeStruct((b, TOP_K, t), I32),
        jax.ShapeDtypeStruct((b, TOP_K, t), F32),
        jax.ShapeDtypeStruct((b, ka - 1, w_a), F32),
        jax.ShapeDtypeStruct((b, kb, w_b), F32),
        jax.ShapeDtypeStruct((b, kc - 1, w_c), F32),
    )
    out_specs = (
        pl.BlockSpec((tile * SUBLANES, LANES), lambda i, j: (jnp.minimum(i * nt + j, last_block), 0)),
        pl.BlockSpec((None, TOP_K, tile), lambda i, j: (seq_tile(i, j)[0], 0, seq_tile(i, j)[1])),
        pl.BlockSpec((None, TOP_K, tile), lambda i, j: (seq_tile(i, j)[0], 0, seq_tile(i, j)[1])),
        pl.BlockSpec((None, ka - 1, w_a), lambda i, j: (jnp.minimum(i, b - 1), 0, 0)),
        pl.BlockSpec((None, kb, w_b), lambda i, j: (jnp.minimum(i, b - 1), 0, 0)),
        pl.BlockSpec((None, kc - 1, w_c), lambda i, j: (jnp.minimum(i, b - 1), 0, 0)),
    )
    in_specs = [pl.BlockSpec((tile, d), lambda i, j: (row_block(i, j), 0))] + [_full_spec(a) for a in params]
    return pl.pallas_call(
        kern,
        grid=(b + 1, nt),
        in_specs=in_specs,
        out_specs=out_specs,
        out_shape=out_shape,
        scratch_shapes=_mixer_scratch(tile, ha, hb, hc, d_in, w_a, w_b, w_c, PROMPT_HIST_PITCH),
        compiler_params=pltpu.CompilerParams(
            dimension_semantics=("arbitrary", "arbitrary"),
            vmem_limit_bytes=VMEM_LIMIT_BYTES),
        name="prompt_mixer",
    )(x, *params)


def _sample_mixer(x1_all, x_tm, st_a, st_b, st_c, params, dims, alpha, batch, steps):
    rows, d = x_tm.shape
    w_a, w_b, w_c, ka, kb, kc, n_groups, n_experts = dims
    d_in = params[0].shape[1]
    first = x1_all.shape[0] // SUBLANES - rows
    assert first % rows == 0
    kern = functools.partial(_sample_mixer_kernel, n_params=len(params), batch=batch,
                             steps=steps, alpha=alpha, n_groups=n_groups, n_experts=n_experts)
    out_shape = (
        jax.ShapeDtypeStruct(x1_all.shape, F32),
        jax.ShapeDtypeStruct((TOP_K, rows), I32),
        jax.ShapeDtypeStruct((TOP_K, rows), F32),
        jax.ShapeDtypeStruct(st_a.shape, F32),
        jax.ShapeDtypeStruct(st_b.shape, F32),
        jax.ShapeDtypeStruct(st_c.shape, F32),
    )
    args = (x_tm, st_a, st_b, st_c) + tuple(params)
    out_specs = (pl.BlockSpec((rows * SUBLANES, LANES), lambda i: (first // rows, 0)),) + tuple(
        pl.BlockSpec(s.shape, lambda i, n=len(s.shape): (0,) * n) for s in out_shape[1:])
    return pl.pallas_call(
        kern,
        grid=(1,),
        in_specs=[pl.BlockSpec(memory_space=pl.ANY)] + [_full_spec(a) for a in args],
        out_specs=out_specs,
        out_shape=out_shape,
        input_output_aliases={0: 0},
        scratch_shapes=_mixer_scratch(rows, (ka - 1) * batch, kb * batch, (kc - 1) * batch,
                                      d_in, w_a, w_b, w_c, 1),
        compiler_params=pltpu.CompilerParams(
            dimension_semantics=("arbitrary",),
            vmem_limit_bytes=VMEM_LIMIT_BYTES),
        name="sample_mixer",
    )(x1_all, *args)


def _start_row_gather(src_hbm, idx_ref, dst_vmem, n, sem):
    def body(c, carry):
        for u in range(GATHER_UNROLL):
            s = c * GATHER_UNROLL + u
            row = idx_ref[0, 0, s]
            pltpu.make_async_copy(
                src_hbm.at[pl.ds(pl.multiple_of(row * SUBLANES, SUBLANES), SUBLANES)],
                dst_vmem.at[pl.ds(s * GATHER_PITCH, SUBLANES)],
                sem).start(priority=u % DMA_PRIORITIES)
        return carry
    lax.fori_loop(0, n // GATHER_UNROLL, body, 0)


def _wait_row_gather(src_hbm, dst_vmem, n, sem):
    pltpu.make_async_copy(src_hbm.at[pl.ds(0, n * SUBLANES)], dst_vmem.at[pl.ds(0, n * SUBLANES)],
                          sem).wait()


def _moe_kernel(te_ref, nu_ref, src0_ref, src1_ref, x_hbm, wg_ref, wu_ref, wd_ref, y_ref,
                xbuf, wg_b, wu_b, wd_b, sem, *, tile):
    i = pl.program_id(0)
    n_used = nu_ref[0]
    slot = i % 2

    @pl.when(i == 0)
    def _():
        _start_row_gather(x_hbm, src0_ref, xbuf.at[0], tile, sem.at[0])

    @pl.when(i + 1 < n_used)
    def _():
        _start_row_gather(x_hbm, src1_ref, xbuf.at[1 - slot], tile, sem.at[1 - slot])

    @pl.when(i < n_used)
    def _():
        @pl.when(jnp.logical_or(i == 0, te_ref[i] != te_ref[jnp.maximum(i - 1, 0)]))
        def _():
            wg_b[...] = wg_ref[...].astype(BF16)
            wu_b[...] = wu_ref[...].astype(BF16)
            wd_b[...] = wd_ref[...].astype(BF16)

        _wait_row_gather(x_hbm, xbuf.at[slot], tile, sem.at[slot])
        xb = _load_row_tiles(xbuf.at[slot], tile, pitch=GATHER_PITCH).astype(BF16)
        hg = _dot(xb, wg_b[...])
        hu = _dot(xb, wu_b[...])
        h = hg * _sigmoid(hg) * hu
        _store_row_tiles(y_ref, _dot(h.astype(BF16), wd_b[...]))

    @pl.when(i >= n_used)
    def _():
        y_ref[...] = jnp.zeros(y_ref.shape, F32)


def _moe(x1, src, tile_expert, n_used, w_gate, w_up, w_down, layer):
    d = w_gate.shape[-2]
    tile = MOE_TILE
    n_tiles = src.shape[0] // tile
    d_e = w_gate.shape[-1]
    grid_spec = pltpu.PrefetchScalarGridSpec(
        num_scalar_prefetch=2,
        grid=(n_tiles,),
        in_specs=[
            pl.BlockSpec((1, 1, tile), lambda i, te, nu: (i, 0, 0), memory_space=pltpu.SMEM),
            pl.BlockSpec((1, 1, tile), lambda i, te, nu: (jnp.minimum(i + 1, n_tiles - 1), 0, 0),
                         memory_space=pltpu.SMEM),
            pl.BlockSpec(memory_space=pl.ANY),
            pl.BlockSpec((None, None, d, d_e), lambda i, te, nu: (layer, te[i], 0, 0)),
            pl.BlockSpec((None, None, d, d_e), lambda i, te, nu: (layer, te[i], 0, 0)),
            pl.BlockSpec((None, None, d_e, d), lambda i, te, nu: (layer, te[i], 0, 0)),
        ],
        out_specs=pl.BlockSpec((tile * SUBLANES, LANES), lambda i, te, nu: (i, 0)),
        scratch_shapes=[
            pltpu.VMEM((2, tile * GATHER_PITCH, LANES), F32),
            pltpu.VMEM((d, d_e), BF16),
            pltpu.VMEM((d, d_e), BF16),
            pltpu.VMEM((d_e, d), BF16),
            pltpu.SemaphoreType.DMA((2,)),
        ],
    )
    src3 = src.reshape(n_tiles, 1, tile)
    return pl.pallas_call(
        functools.partial(_moe_kernel, tile=tile),
        grid_spec=grid_spec,
        out_shape=jax.ShapeDtypeStruct((n_tiles * tile * SUBLANES, LANES), F32),
        compiler_params=pltpu.CompilerParams(
            dimension_semantics=("arbitrary",),
            vmem_limit_bytes=VMEM_LIMIT_BYTES),
        name="moe",
    )(tile_expert, n_used, src3, src3, x1, w_gate, w_up, w_down)


def _combine_kernel(pos0_ref, pos1_ref, x1_ref, w_ref, g_ref, b_ref, y_hbm, op_ref, os_ref, ybuf, sem,
                    *, tile, alpha, prompt_tiles):
    i = pl.program_id(0)
    rows = TOP_K * tile
    slot = i % 2

    @pl.when(i == 0)
    def _():
        _start_row_gather(y_hbm, pos0_ref, ybuf.at[0], rows, sem.at[0])

    @pl.when(i + 1 < pl.num_programs(0))
    def _():
        _start_row_gather(y_hbm, pos1_ref, ybuf.at[1 - slot], rows, sem.at[1 - slot])

    _wait_row_gather(y_hbm, ybuf.at[slot], rows, sem.at[slot])
    w = w_ref[...]
    y_a = _load_row_tiles(ybuf.at[slot], tile, pitch=GATHER_PITCH)
    y_b = _load_row_tiles(ybuf.at[slot], tile, first=tile * GATHER_PITCH, pitch=GATHER_PITCH)
    moe = w[:, 0:1] * y_a + w[:, 1:2] * y_b
    x1 = _load_row_tiles(x1_ref, tile)
    out = _layer_norm(alpha * x1 + moe, g_ref[...], b_ref[...])

    @pl.when(i < prompt_tiles)
    def _():
        op_ref[...] = out

    @pl.when(i >= prompt_tiles)
    def _():
        os_ref[...] = out


def _combine(x1, y, pos, wts, g, b, alpha, n_prompt):
    n, d = x1.shape[0] // SUBLANES, g.shape[-1]
    tile = COMBINE_TILE
    n_tiles = n // tile
    prompt_tiles = n_prompt // tile
    assert n % tile == 0 and n_prompt % tile == 0 and 0 < prompt_tiles < n_tiles
    pos_t = pos.reshape(TOP_K, n_tiles, tile).transpose(1, 0, 2).reshape(n_tiles, 1, TOP_K * tile)
    return pl.pallas_call(
        functools.partial(_combine_kernel, tile=tile, alpha=alpha, prompt_tiles=prompt_tiles),
        grid=(n_tiles,),
        in_specs=[
            pl.BlockSpec((1, 1, TOP_K * tile), lambda i: (i, 0, 0), memory_space=pltpu.SMEM),
            pl.BlockSpec((1, 1, TOP_K * tile), lambda i: (jnp.minimum(i + 1, n_tiles - 1), 0, 0),
                         memory_space=pltpu.SMEM),
            pl.BlockSpec((tile * SUBLANES, LANES), lambda i: (i, 0)),
            pl.BlockSpec((tile, TOP_K), lambda i: (i, 0)),
            _full_spec(g),
            _full_spec(b),
            pl.BlockSpec(memory_space=pl.ANY),
        ],
        out_specs=(
            pl.BlockSpec((tile, d), lambda i: (jnp.minimum(i, prompt_tiles - 1), 0)),
            pl.BlockSpec((tile, d), lambda i: (jnp.maximum(i - prompt_tiles, 0), 0)),
        ),
        out_shape=(jax.ShapeDtypeStruct((n_prompt, d), F32),
                   jax.ShapeDtypeStruct((n - n_prompt, d), F32)),
        scratch_shapes=[pltpu.VMEM((2, TOP_K * tile * GATHER_PITCH, LANES), F32),
                        pltpu.SemaphoreType.DMA((2,))],
        compiler_params=pltpu.CompilerParams(
            dimension_semantics=("arbitrary",),
            vmem_limit_bytes=VMEM_LIMIT_BYTES),
        name="combine",
    )(pos_t, pos_t, x1, wts.T, g, b, y)


def _route_tables(eidx, n_experts, tile):
    k, n = eidx.shape
    pairs = k * n
    n_tiles = pairs // tile + n_experts
    e = eidx.reshape(pairs)
    onehot = (e[:, None] == jnp.arange(n_experts, dtype=I32)[None, :]).astype(I32)
    csum = jnp.cumsum(onehot, axis=0)
    rank = jnp.sum(onehot * csum, axis=1) - 1
    counts = csum[-1]
    ptiles = (counts + tile - 1) // tile
    tile_end = jnp.cumsum(ptiles)
    offs = (tile_end - ptiles) * tile
    pos = jnp.sum(onehot * offs[None, :], axis=1) + rank
    n_used = tile_end[-1]
    j = jnp.arange(n_tiles, dtype=I32)
    te = jnp.sum((j[:, None] >= tile_end[None, :]).astype(I32), axis=1)
    te_last = jnp.sum((n_used - 1 >= tile_end).astype(I32))
    te = jnp.where(j < n_used, te, te_last).astype(I32)
    tok = jnp.tile(jnp.arange(n, dtype=I32), k)
    src = jnp.zeros((n_tiles * tile,), I32).at[pos].set(tok, unique_indices=True)
    return pos.reshape(k, n), src, te, n_used.reshape(1).astype(I32)


def _layer_params(l, w_in_b, b_in, conv_a_w, conv_a_b, ln_a_g, ln_a_b, w_pool_bd, pool_scale,
                  conv_c_w, w_out_b, b_out, ln1_g, ln1_b, w_router, b_router):
    row = lambda a: a[l][None, :]
    return (w_in_b[l], row(b_in), conv_a_w[l], row(conv_a_b), row(ln_a_g), row(ln_a_b),
            w_pool_bd[l], row(pool_scale), conv_c_w[l], w_out_b[l], row(b_out), row(ln1_g),
            row(ln1_b), w_router[l], b_router[l])


def kernel(x_prompt, x_sample, state_conv_a, state_pool_b, state_conv_c, w_in, b_in, conv_a_w, conv_a_b, ln_a_g, ln_a_b, w_pool, pool_scale, conv_c_w, w_out, b_out, ln1_g, ln1_b, w_router_group, b_router_group, w_router_expert, b_router_expert, w_gate, w_up, w_down, ln2_g, ln2_b):
    depth = w_in.shape[0]
    bp, tp, d = x_prompt.shape
    bs, ts, _ = x_sample.shape
    ka, w_a = conv_a_w.shape[1:]
    kb, w_b = state_pool_b.shape[2:]
    kc, w_c = conv_c_w.shape[1:]
    n_groups = w_router_group.shape[-1]
    n_experts = w_router_expert.shape[-1]
    dims = (w_a, w_b, w_c, ka, kb, kc, n_groups, n_experts)
    alpha = float((2 * depth) ** 0.25)
    assert tp % PROMPT_TILE == 0 and PROMPT_TILE % ROW_CHUNK == 0 and (bs * ts) % ROW_CHUNK == 0
    assert n_groups <= SUBLANES and ROUTER_EXPERT_ROW + n_experts <= LANES
    assert d == SUBLANES * LANES

    w_in_b = w_in.astype(BF16)
    w_out_b = w_out.astype(BF16)
    n_pg, pg = w_pool.shape[1], w_pool.shape[2]
    eye = jnp.eye(n_pg, dtype=F32)
    w_pool_bd = (w_pool[:, :, :, None, :] * eye[None, :, None, :, None]).reshape(depth, n_pg * pg, n_pg * pg).astype(BF16)
    wr = jnp.zeros((depth, d, LANES), F32)
    wr = wr.at[:, :, ROUTER_GROUP_ROW:ROUTER_GROUP_ROW + n_groups].set(w_router_group)
    wr = wr.at[:, :, ROUTER_EXPERT_ROW:ROUTER_EXPERT_ROW + n_experts].set(w_router_expert)
    wr_hi = wr.astype(BF16)
    wr_lo = (wr - wr_hi.astype(F32)).astype(BF16)
    w_router = jnp.concatenate([wr_hi, wr_lo], axis=-1)
    b_router = jnp.zeros((depth, 1, LANES), F32)
    b_router = b_router.at[:, 0, ROUTER_GROUP_ROW:ROUTER_GROUP_ROW + n_groups].set(b_router_group)
    b_router = b_router.at[:, 0, ROUTER_EXPERT_ROW:ROUTER_EXPERT_ROW + n_experts].set(b_router_expert)

    xp = x_prompt.reshape(bp * tp, d)
    xs = x_sample.transpose(1, 0, 2).reshape(ts * bs, d)
    new_states = [[] for _ in range(6)]
    for l in range(depth):
        params = _layer_params(l, w_in_b, b_in, conv_a_w, conv_a_b, ln_a_g, ln_a_b, w_pool_bd,
                               pool_scale, conv_c_w, w_out_b, b_out, ln1_g, ln1_b, w_router, b_router)
        g2, b2 = ln2_g[l][None, :], ln2_b[l][None, :]

        x1, eidx, wts, pa, pb, pc = _prompt_mixer(xp, bp, tp, bs * ts, params, dims, alpha)
        x1, seidx, swts, sa, sb, sc = _sample_mixer(
            x1, xs, state_conv_a[l].reshape(bs, -1), state_pool_b[l].reshape(bs, -1),
            state_conv_c[l].reshape(bs, -1), params, dims, alpha, bs, ts)
        eidx = jnp.concatenate([eidx.transpose(1, 0, 2).reshape(TOP_K, bp * tp), seidx], axis=1)
        wts = jnp.concatenate([wts.transpose(1, 0, 2).reshape(TOP_K, bp * tp), swts], axis=1)

        pos, src, te, nu = _route_tables(eidx, n_experts, MOE_TILE)
        y = _moe(x1, src, te, nu, w_gate, w_up, w_down, l)
        xp, xs = _combine(x1, y, pos, wts, g2, b2, alpha, bp * tp)

        for lst, val in zip(new_states, (pa, sa.reshape(bs, ka - 1, w_a), pb, sb.reshape(bs, kb, w_b),
                                         pc, sc.reshape(bs, kc - 1, w_c))):
            lst.append(val)

    y_prompt = xp.reshape(bp, tp, d)
    y_sample = xs.reshape(ts, bs, d).transpose(1, 0, 2)
    return (y_prompt, y_sample) + tuple(jnp.stack(s) for s in new_states)
```

```python
import functools

import jax
import jax.numpy as jnp
from jax import lax
from jax.experimental import pallas as pl
from jax.experimental.pallas import tpu as pltpu

F32 = jnp.float32
BF16 = jnp.bfloat16
I32 = jnp.int32

POOL_WINDOWS = (2, 4, 8, 16)
TOP_K = 2
LN_EPS = 1e-5
PAST_LEN = 16384

LANES = 128
SUBLANES = 8
SUBLANE_SHIFT = 3
VMEM_LIMIT_BYTES = 56 * 1024 * 1024

ROW_CHUNK = 32
PROMPT_TILE = 512
PROMPT_HIST_PITCH = 2
MOE_TILE = 1024
MOE_SUB = 256
COMBINE_TILE = 256
GATHER_UNROLL = 16
DMA_PRIORITIES = 2
GATHER_PITCH = 9
ROUTER_GROUP_ROW = 0
ROUTER_EXPERT_ROW = 8


def _dot(a, b):
    return jnp.dot(a, b, preferred_element_type=F32)


def _layer_norm(x, g, b):
    mu = jnp.mean(x, axis=-1, keepdims=True)
    xc = x - mu
    var = jnp.mean(xc * xc, axis=-1, keepdims=True)
    return xc * lax.rsqrt(var + LN_EPS) * g + b


def _sigmoid(x):
    return 1.0 / (1.0 + jnp.exp(-x))


def _store_row_tiles(ref, val):
    rows, d = val.shape
    for j in range(d // LANES):
        ref[pl.ds(j, rows, stride=SUBLANES), :] = val[:, j * LANES:(j + 1) * LANES]


def _load_row_tiles(ref, rows, first=0, pitch=SUBLANES):
    return jnp.concatenate(
        [ref[pl.ds(first + j, rows, stride=pitch), :] for j in range(SUBLANES)], axis=-1)


def _hist_load(buf, start, n, pitch):
    parts = []
    for j in range(buf.shape[0]):
        if pitch == 1:
            parts.append(buf[j, pl.ds(start, n), :])
        else:
            parts.append(buf.at[j][pl.ds(pitch * start, n, stride=pitch), :])
    return parts[0] if len(parts) == 1 else jnp.concatenate(parts, axis=-1)


def _hist_store(buf, start, n, pitch, val):
    for j in range(buf.shape[0]):
        piece = val[:, j * LANES:(j + 1) * LANES]
        if pitch == 1:
            buf[j, pl.ds(start, n), :] = piece
        else:
            buf.at[j][pl.ds(pitch * start, n, stride=pitch), :] = piece


def _mixer_rows(x_ref, p, bufs, outs, *, rows, stride, pitch, new_a, new_b, new_c, pos_of_row,
                alpha, n_groups, n_experts):
    (w_in, b_in, wa, ba, lnag, lnab, wbd, pscale, wc, w_out, b_out, ln1g, ln1b,
     wr, br) = p
    proj, buf_a, buf_b, buf_c, dbuf, cat = bufs
    x1_ref, eidx_ref, wts_ref = outs
    ka = wa.shape[0]
    kc = wc.shape[0]
    w_a = wa.shape[1]
    w_b = pscale.shape[1]
    w_c = wc.shape[1]
    o_gate, o_ub, o_cbg, o_ccg, o_ch = w_a, 2 * w_a, 2 * w_a + w_b, 2 * w_a + w_b + w_c, 2 * w_a + w_b + 2 * w_c

    x = x_ref[...]
    proj[...] = _dot(x.astype(BF16), w_in[...]) + b_in[...]

    _hist_store(buf_a, new_a, rows, pitch, proj[:, 0:w_a] * _sigmoid(proj[:, o_gate:o_gate + w_a]))
    _hist_store(buf_b, new_b, rows, pitch, proj[:, o_ub:o_ub + w_b])
    _hist_store(buf_c, new_c, rows, pitch, proj[:, o_ccg:o_ccg + w_c] * proj[:, o_ch:o_ch + w_c])

    lane_b = lax.broadcasted_iota(I32, (ROW_CHUNK, w_b), 1)
    group_b = lane_b // (w_b // len(POOL_WINDOWS))
    win_b = jnp.zeros((ROW_CHUNK, w_b), I32)
    for g, w in enumerate(POOL_WINDOWS):
        win_b = jnp.where(group_b == g, w, win_b)

    def chunk(c, carry):
        r0 = c * ROW_CHUNK if isinstance(c, int) else pl.multiple_of(c * ROW_CHUNK, ROW_CHUNK)
        acc = jnp.zeros((ROW_CHUNK, w_a), F32) + ba[...]
        for k in range(ka):
            src = _hist_load(buf_a, r0 + (new_a - (ka - 1 - k) * stride), ROW_CHUNK, pitch)
            acc = acc + wa[k:k + 1, :] * src
        a = _layer_norm(acc, lnag[...], lnab[...])
        cat[pl.ds(r0, ROW_CHUNK), 0:w_a] = (a * _sigmoid(a)).astype(BF16)
        cur = _hist_load(buf_b, r0 + new_b, ROW_CHUNK, pitch)
        run = cur
        wsum = jnp.zeros((ROW_CHUNK, w_b), F32)
        for j in range(1, max(POOL_WINDOWS) + 1):
            if j in POOL_WINDOWS:
                wsum = jnp.where(win_b == j, run, wsum)
            if j < max(POOL_WINDOWS):
                run = run + _hist_load(buf_b, r0 + (new_b - j * stride), ROW_CHUNK, pitch)
        pos = pos_of_row(r0 + lax.broadcasted_iota(I32, (ROW_CHUNK, w_b), 0))
        cnt = jnp.minimum(pos + 1, win_b).astype(F32)
        dbuf[pl.ds(r0, ROW_CHUNK), :] = (wsum / cnt - cur).astype(BF16)
        accc = jnp.zeros((ROW_CHUNK, w_c), F32)
        for k in range(kc):
            src = _hist_load(buf_c, r0 + (new_c - (kc - 1 - k) * stride), ROW_CHUNK, pitch)
            accc = accc + wc[k:k + 1, :] * src
        cbg = proj[pl.ds(r0, ROW_CHUNK), o_cbg:o_cbg + w_c]
        cat[pl.ds(r0, ROW_CHUNK), w_a + w_b:w_a + w_b + w_c] = (cbg * accc).astype(BF16)
        return carry

    if pitch == 1:
        assert stride % SUBLANES == 0
        lax.fori_loop(0, rows // ROW_CHUNK, chunk, 0)
    else:
        for c in range(rows // ROW_CHUNK):
            chunk(c, 0)

    cat[:, w_a:w_a + w_b] = (_dot(dbuf[...], wbd[...]) * pscale[...]).astype(BF16)

    m = _dot(cat[...], w_out[...]) + b_out[...]
    x1 = _layer_norm(alpha * x + m, ln1g[...], ln1b[...])
    _store_row_tiles(x1_ref, x1)

    x1_hi = x1.astype(BF16)
    x1_lo = (x1 - x1_hi.astype(F32)).astype(BF16)
    p_hi = _dot(x1_hi, wr[...])
    p_lo = _dot(x1_lo, wr[...])
    logits = (p_hi[:, 0:LANES] + p_hi[:, LANES:2 * LANES]) + (p_lo[:, 0:LANES] + p_lo[:, LANES:2 * LANES])
    lt = jnp.transpose(logits + br[...])

    neg = jnp.float32(-jnp.inf)
    gl = lt[ROUTER_GROUP_ROW:ROUTER_GROUP_ROW + SUBLANES, :]
    grow = lax.broadcasted_iota(I32, gl.shape, 0)
    gvalid = grow < n_groups
    glm = jnp.where(gvalid, gl, neg)
    gmax = jnp.max(glm, axis=0, keepdims=True)
    gidx = jnp.min(jnp.where(glm == gmax, grow, SUBLANES), axis=0, keepdims=True)
    gsum = jnp.sum(jnp.where(gvalid, jnp.exp(gl - gmax), 0.0), axis=0, keepdims=True)
    g_p = 1.0 / gsum

    el = lt[ROUTER_EXPERT_ROW:ROUTER_EXPERT_ROW + n_experts, :]
    erow = lax.broadcasted_iota(I32, el.shape, 0)
    v = jnp.where(erow // (n_experts // n_groups) == gidx, el, neg)
    v1 = jnp.max(v, axis=0, keepdims=True)
    i1 = jnp.min(jnp.where(v == v1, erow, n_experts), axis=0, keepdims=True)
    vv = jnp.where(erow == i1, neg, v)
    v2 = jnp.max(vv, axis=0, keepdims=True)
    i2 = jnp.min(jnp.where(vv == v2, erow, n_experts), axis=0, keepdims=True)
    e2 = jnp.exp(v2 - v1)
    den = 1.0 + e2
    eidx_ref[0:1, :] = i1
    eidx_ref[1:2, :] = i2
    wts_ref[0:1, :] = (1.0 / den) * g_p
    wts_ref[1:2, :] = (e2 / den) * g_p


def _prompt_mixer_kernel(x_ref, *refs, n_params, n_seq, extra_tiles, tile, hist, pitch, alpha,
                         n_groups, n_experts):
    p = refs[:n_params]
    x1_ref, eidx_ref, wts_ref, sa_ref, sb_ref, sc_ref = refs[n_params:n_params + 6]
    bufs = refs[n_params + 6:]
    _, buf_a, buf_b, buf_c, _, _ = bufs
    ha, hb, hc = hist
    s = pl.program_id(0)
    t = pl.program_id(1)

    @pl.when(s < n_seq)
    def _():
        hists = ((buf_a, ha), (buf_b, hb), (buf_c, hc))

        @pl.when(t == 0)
        def _():
            for buf, h in hists:
                _hist_store(buf, 0, h, pitch, jnp.zeros((h, buf.shape[0] * LANES), F32))

        _mixer_rows(x_ref, p, bufs, (x1_ref, eidx_ref, wts_ref), rows=tile, stride=1, pitch=pitch,
                    new_a=ha, new_b=hb, new_c=hc,
                    pos_of_row=lambda r: r + t * tile,
                    alpha=alpha, n_groups=n_groups, n_experts=n_experts)

        tails = [_hist_load(buf, tile, h, pitch) for buf, h in hists]
        for (buf, h), tail in zip(hists, tails):
            _hist_store(buf, 0, h, pitch, tail)

        @pl.when(t == pl.num_programs(1) - 1)
        def _():
            for ref, (_, h), tail in zip((sa_ref, sb_ref, sc_ref), hists, tails):
                ref[...] = tail[h - ref.shape[0]:h, :]

    @pl.when(jnp.logical_and(s == n_seq, t < extra_tiles))
    def _():
        x1_ref[...] = jnp.zeros(x1_ref.shape, F32)


def _sample_mixer_kernel(x1_all_hbm, x_ref, sta_ref, stb_ref, stc_ref, *refs, n_params, batch, steps,
                         alpha, n_groups, n_experts):
    del x1_all_hbm
    p = refs[:n_params]
    x1_ref, eidx_ref, wts_ref, sa_ref, sb_ref, sc_ref = refs[n_params:n_params + 6]
    bufs = refs[n_params + 6:]
    _, buf_a, buf_b, buf_c, _, _ = bufs
    rows = batch * steps
    hists = []
    for buf, st_ref, out_ref in ((buf_a, sta_ref, sa_ref), (buf_b, stb_ref, sb_ref),
                                 (buf_c, stc_ref, sc_ref)):
        width = buf.shape[0] * LANES
        hists.append((buf, st_ref, out_ref, width, st_ref.shape[1] // width))

    for buf, st_ref, _, width, n in hists:
        for j in range(n):
            _hist_store(buf, j * batch, batch, 1, st_ref[:, j * width:(j + 1) * width])

    na, nb, nc = (h[4] for h in hists)
    _mixer_rows(x_ref, p, bufs, (x1_ref, eidx_ref, wts_ref), rows=rows, stride=batch, pitch=1,
                new_a=na * batch, new_b=nb * batch, new_c=nc * batch,
                pos_of_row=lambda r: PAST_LEN + r // batch,
                alpha=alpha, n_groups=n_groups, n_experts=n_experts)

    for buf, _, out_ref, width, n in hists:
        for j in range(n):
            out_ref[:, j * width:(j + 1) * width] = _hist_load(buf, (j + steps) * batch, batch, 1)


def _full_spec(a):
    nd = a.ndim
    return pl.BlockSpec(a.shape, lambda *_: (0,) * nd)


def _mixer_scratch(rows, ha, hb, hc, d_in, w_a, w_b, w_c, pitch):
    return [
        pltpu.VMEM((rows, d_in), F32),
        pltpu.VMEM((w_a // LANES, pitch * (ha + rows), LANES), F32),
        pltpu.VMEM((w_b // LANES, pitch * (hb + rows), LANES), F32),
        pltpu.VMEM((w_c // LANES, pitch * (hc + rows), LANES), F32),
        pltpu.VMEM((rows, w_b), BF16),
        pltpu.VMEM((rows, w_a + w_b + w_c), BF16),
    ]


def _round_up(n, m):
    return (n + m - 1) // m * m


def _prompt_mixer(x, b, t, extra_rows, params, dims, alpha):
    d = x.shape[1]
    w_a, w_b, w_c, ka, kb, kc, n_groups, n_experts = dims
    tile = PROMPT_TILE
    nt = t // tile
    ha, hb, hc = _round_up(ka - 1, SUBLANES), _round_up(kb, SUBLANES), _round_up(kc - 1, SUBLANES)
    d_in = params[0].shape[1]
    extra_tiles = extra_rows // tile
    assert extra_rows % tile == 0 and extra_tiles <= nt
    kern = functools.partial(_prompt_mixer_kernel, n_params=len(params), n_seq=b,
                             extra_tiles=extra_tiles, tile=tile, hist=(ha, hb, hc),
                             pitch=PROMPT_HIST_PITCH, alpha=alpha,
                             n_groups=n_groups, n_experts=n_experts)
    last_block = b * nt + extra_tiles - 1

    def seq_tile(i, j):
        return jnp.minimum(i, b - 1), jnp.where(i < b, j, nt - 1)

    def row_block(i, j):
        s, t_ = seq_tile(i, j)
        return s * nt + t_

    out_shape = (
        jax.ShapeDtypeStruct(((b * t + extra_rows) * SUBLANES, LANES), F32),
        jax.ShapeDtypeStruct((b, TOP_K, t), I32),
        jax.ShapeDtypeStruct((b, TOP_K, t), F32),
        jax.ShapeDtypeStruct((b, ka - 1, w_a), F32),
        jax.ShapeDtypeStruct((b, kb, w_b), F32),
        jax.ShapeDtypeStruct((b, kc - 1, w_c), F32),
    )
    out_specs = (
        pl.BlockSpec((tile * SUBLANES, LANES), lambda i, j: (jnp.minimum(i * nt + j, last_block), 0)),
        pl.BlockSpec((None, TOP_K, tile), lambda i, j: (seq_tile(i, j)[0], 0, seq_tile(i, j)[1])),
        pl.BlockSpec((None, TOP_K, tile), lambda i, j: (seq_tile(i, j)[0], 0, seq_tile(i, j)[1])),
        pl.BlockSpec((None, ka - 1, w_a), lambda i, j: (jnp.minimum(i, b - 1), 0, 0)),
        pl.BlockSpec((None, kb, w_b), lambda i, j: (jnp.minimum(i, b - 1), 0, 0)),
        pl.BlockSpec((None, kc - 1, w_c), lambda i, j: (jnp.minimum(i, b - 1), 0, 0)),
    )
    in_specs = [pl.BlockSpec((tile, d), lambda i, j: (row_block(i, j), 0))] + [_full_spec(a) for a in params]
    return pl.pallas_call(
        kern,
        grid=(b + 1, nt),
        in_specs=in_specs,
        out_specs=out_specs,
        out_shape=out_shape,
        scratch_shapes=_mixer_scratch(tile, ha, hb, hc, d_in, w_a, w_b, w_c, PROMPT_HIST_PITCH),
        compiler_params=pltpu.CompilerParams(
            dimension_semantics=("arbitrary", "arbitrary"),
            vmem_limit_bytes=VMEM_LIMIT_BYTES),
        name="prompt_mixer",
    )(x, *params)


def _sample_mixer(x1_all, x_tm, st_a, st_b, st_c, params, dims, alpha, batch, steps):
    rows, d = x_tm.shape
    w_a, w_b, w_c, ka, kb, kc, n_groups, n_experts = dims
    d_in = params[0].shape[1]
    first = x1_all.shape[0] // SUBLANES - rows
    assert first % rows == 0
    kern = functools.partial(_sample_mixer_kernel, n_params=len(params), batch=batch,
                             steps=steps, alpha=alpha, n_groups=n_groups, n_experts=n_experts)
    out_shape = (
        jax.ShapeDtypeStruct(x1_all.shape, F32),
        jax.ShapeDtypeStruct((TOP_K, rows), I32),
        jax.ShapeDtypeStruct((TOP_K, rows), F32),
        jax.ShapeDtypeStruct(st_a.shape, F32),
        jax.ShapeDtypeStruct(st_b.shape, F32),
        jax.ShapeDtypeStruct(st_c.shape, F32),
    )
    args = (x_tm, st_a, st_b, st_c) + tuple(params)
    out_specs = (pl.BlockSpec((rows * SUBLANES, LANES), lambda i: (first // rows, 0)),) + tuple(
        pl.BlockSpec(s.shape, lambda i, n=len(s.shape): (0,) * n) for s in out_shape[1:])
    return pl.pallas_call(
        kern,
        grid=(1,),
        in_specs=[pl.BlockSpec(memory_space=pl.ANY)] + [_full_spec(a) for a in args],
        out_specs=out_specs,
        out_shape=out_shape,
        input_output_aliases={0: 0},
        scratch_shapes=_mixer_scratch(rows, (ka - 1) * batch, kb * batch, (kc - 1) * batch,
                                      d_in, w_a, w_b, w_c, 1),
        compiler_params=pltpu.CompilerParams(
            dimension_semantics=("arbitrary",),
            vmem_limit_bytes=VMEM_LIMIT_BYTES),
        name="sample_mixer",
    )(x1_all, *args)


def _start_row_gather(src_hbm, idx_ref, dst_vmem, n, sem):
    def body(c, carry):
        for u in range(GATHER_UNROLL):
            s = c * GATHER_UNROLL + u
            row = idx_ref[0, 0, s]
            pltpu.make_async_copy(
                src_hbm.at[pl.ds(pl.multiple_of(row * SUBLANES, SUBLANES), SUBLANES)],
                dst_vmem.at[pl.ds(s * GATHER_PITCH, SUBLANES)],
                sem).start(priority=u % DMA_PRIORITIES)
        return carry
    lax.fori_loop(0, n // GATHER_UNROLL, body, 0)


def _wait_row_gather(src_hbm, dst_vmem, n, sem):
    pltpu.make_async_copy(src_hbm.at[pl.ds(0, n * SUBLANES)], dst_vmem.at[pl.ds(0, n * SUBLANES)],
                          sem).wait()


def _moe_kernel(te_ref, cnt_ref, nu_ref, src0_ref, src1_ref, x_hbm, wg_ref, wu_ref, wd_ref, y_ref,
                xbuf, wg_b, wu_b, wd_b, sem, *, tile, sub):
    i = pl.program_id(0)
    n_used = nu_ref[0]
    slot = i % 2
    nxt = jnp.minimum(i + 1, pl.num_programs(0) - 1)

    @pl.when(i == 0)
    def _():
        _start_row_gather(x_hbm, src0_ref, xbuf.at[0], cnt_ref[0] * sub, sem.at[0])

    @pl.when(i + 1 < n_used)
    def _():
        _start_row_gather(x_hbm, src1_ref, xbuf.at[1 - slot], cnt_ref[nxt] * sub, sem.at[1 - slot])

    @pl.when(i < n_used)
    def _():
        @pl.when(jnp.logical_or(i == 0, te_ref[i] != te_ref[jnp.maximum(i - 1, 0)]))
        def _():
            wg_b[...] = wg_ref[...].astype(BF16)
            wu_b[...] = wu_ref[...].astype(BF16)
            wd_b[...] = wd_ref[...].astype(BF16)

        _wait_row_gather(x_hbm, xbuf.at[slot], cnt_ref[i] * sub, sem.at[slot])

        for c in range(1, tile // sub + 1):
            @pl.when(cnt_ref[i] == c)
            def _(rows=c * sub):
                xb = _load_row_tiles(xbuf.at[slot], rows, pitch=GATHER_PITCH).astype(BF16)
                hg = _dot(xb, wg_b[...])
                hu = _dot(xb, wu_b[...])
                h = hg * _sigmoid(hg) * hu
                _store_row_tiles(y_ref, _dot(h.astype(BF16), wd_b[...]))
                if rows < tile:
                    y_ref[pl.ds(rows * SUBLANES, (tile - rows) * SUBLANES), :] = jnp.zeros(
                        ((tile - rows) * SUBLANES, LANES), F32)

    @pl.when(i >= n_used)
    def _():
        y_ref[...] = jnp.zeros(y_ref.shape, F32)


def _moe(x1, src, tile_expert, tile_subs, n_used, w_gate, w_up, w_down, layer):
    d = w_gate.shape[-2]
    tile = MOE_TILE
    n_tiles = src.shape[0] // tile
    d_e = w_gate.shape[-1]
    grid_spec = pltpu.PrefetchScalarGridSpec(
        num_scalar_prefetch=3,
        grid=(n_tiles,),
        in_specs=[
            pl.BlockSpec((1, 1, tile), lambda i, te, cnt, nu: (i, 0, 0), memory_space=pltpu.SMEM),
            pl.BlockSpec((1, 1, tile), lambda i, te, cnt, nu: (jnp.minimum(i + 1, n_tiles - 1), 0, 0),
                         memory_space=pltpu.SMEM),
            pl.BlockSpec(memory_space=pl.ANY),
            pl.BlockSpec((None, None, d, d_e), lambda i, te, cnt, nu: (layer, te[i], 0, 0)),
            pl.BlockSpec((None, None, d, d_e), lambda i, te, cnt, nu: (layer, te[i], 0, 0)),
            pl.BlockSpec((None, None, d_e, d), lambda i, te, cnt, nu: (layer, te[i], 0, 0)),
        ],
        out_specs=pl.BlockSpec((tile * SUBLANES, LANES), lambda i, te, cnt, nu: (i, 0)),
        scratch_shapes=[
            pltpu.VMEM((2, tile * GATHER_PITCH, LANES), F32),
            pltpu.VMEM((d, d_e), BF16),
            pltpu.VMEM((d, d_e), BF16),
            pltpu.VMEM((d_e, d), BF16),
            pltpu.SemaphoreType.DMA((2,)),
        ],
    )
    src3 = src.reshape(n_tiles, 1, tile)
    return pl.pallas_call(
        functools.partial(_moe_kernel, tile=tile, sub=MOE_SUB),
        grid_spec=grid_spec,
        out_shape=jax.ShapeDtypeStruct((n_tiles * tile * SUBLANES, LANES), F32),
        compiler_params=pltpu.CompilerParams(
            dimension_semantics=("arbitrary",),
            vmem_limit_bytes=VMEM_LIMIT_BYTES),
        name="moe",
    )(tile_expert, tile_subs, n_used, src3, src3, x1, w_gate, w_up, w_down)


def _combine_kernel(pos0_ref, pos1_ref, x1_ref, w_ref, g_ref, b_ref, y_hbm, op_ref, os_ref, ybuf, sem,
                    *, tile, alpha, prompt_tiles):
    i = pl.program_id(0)
    rows = TOP_K * tile
    slot = i % 2

    @pl.when(i == 0)
    def _():
        _start_row_gather(y_hbm, pos0_ref, ybuf.at[0], rows, sem.at[0])

    @pl.when(i + 1 < pl.num_programs(0))
    def _():
        _start_row_gather(y_hbm, pos1_ref, ybuf.at[1 - slot], rows, sem.at[1 - slot])

    _wait_row_gather(y_hbm, ybuf.at[slot], rows, sem.at[slot])
    w = w_ref[...]
    y_a = _load_row_tiles(ybuf.at[slot], tile, pitch=GATHER_PITCH)
    y_b = _load_row_tiles(ybuf.at[slot], tile, first=tile * GATHER_PITCH, pitch=GATHER_PITCH)
    moe = w[:, 0:1] * y_a + w[:, 1:2] * y_b
    x1 = _load_row_tiles(x1_ref, tile)
    out = _layer_norm(alpha * x1 + moe, g_ref[...], b_ref[...])

    @pl.when(i < prompt_tiles)
    def _():
        op_ref[...] = out

    @pl.when(i >= prompt_tiles)
    def _():
        os_ref[...] = out


def _combine(x1, y, pos, wts, g, b, alpha, n_prompt):
    n, d = x1.shape[0] // SUBLANES, g.shape[-1]
    tile = COMBINE_TILE
    n_tiles = n // tile
    prompt_tiles = n_prompt // tile
    assert n % tile == 0 and n_prompt % tile == 0 and 0 < prompt_tiles < n_tiles
    pos_t = pos.reshape(TOP_K, n_tiles, tile).transpose(1, 0, 2).reshape(n_tiles, 1, TOP_K * tile)
    return pl.pallas_call(
        functools.partial(_combine_kernel, tile=tile, alpha=alpha, prompt_tiles=prompt_tiles),
        grid=(n_tiles,),
        in_specs=[
            pl.BlockSpec((1, 1, TOP_K * tile), lambda i: (i, 0, 0), memory_space=pltpu.SMEM),
            pl.BlockSpec((1, 1, TOP_K * tile), lambda i: (jnp.minimum(i + 1, n_tiles - 1), 0, 0),
                         memory_space=pltpu.SMEM),
            pl.BlockSpec((tile * SUBLANES, LANES), lambda i: (i, 0)),
            pl.BlockSpec((tile, TOP_K), lambda i: (i, 0)),
            _full_spec(g),
            _full_spec(b),
            pl.BlockSpec(memory_space=pl.ANY),
        ],
        out_specs=(
            pl.BlockSpec((tile, d), lambda i: (jnp.minimum(i, prompt_tiles - 1), 0)),
            pl.BlockSpec((tile, d), lambda i: (jnp.maximum(i - prompt_tiles, 0), 0)),
        ),
        out_shape=(jax.ShapeDtypeStruct((n_prompt, d), F32),
                   jax.ShapeDtypeStruct((n - n_prompt, d), F32)),
        scratch_shapes=[pltpu.VMEM((2, TOP_K * tile * GATHER_PITCH, LANES), F32),
                        pltpu.SemaphoreType.DMA((2,))],
        compiler_params=pltpu.CompilerParams(
            dimension_semantics=("arbitrary",),
            vmem_limit_bytes=VMEM_LIMIT_BYTES),
        name="combine",
    )(pos_t, pos_t, x1, wts.T, g, b, y)


def _route_tables(eidx, n_experts, tile, sub):
    k, n = eidx.shape
    pairs = k * n
    n_tiles = pairs // tile + n_experts
    e = eidx.reshape(pairs)
    onehot = (e[:, None] == jnp.arange(n_experts, dtype=I32)[None, :]).astype(I32)
    csum = jnp.cumsum(onehot, axis=0)
    rank = jnp.sum(onehot * csum, axis=1) - 1
    counts = csum[-1]
    ptiles = (counts + tile - 1) // tile
    tile_end = jnp.cumsum(ptiles)
    offs = (tile_end - ptiles) * tile
    pos = jnp.sum(onehot * offs[None, :], axis=1) + rank
    n_used = tile_end[-1]
    j = jnp.arange(n_tiles, dtype=I32)
    te = jnp.sum((j[:, None] >= tile_end[None, :]).astype(I32), axis=1)
    te_last = jnp.sum((n_used - 1 >= tile_end).astype(I32))
    te = jnp.where(j < n_used, te, te_last).astype(I32)
    subs_per_tile = tile // sub
    first_tile = (tile_end - ptiles)[te]
    real_subs = ((counts + sub - 1) // sub)[te]
    tile_subs = jnp.clip(real_subs - (j - first_tile) * subs_per_tile, 0, subs_per_tile)
    tile_subs = jnp.where(j < n_used, tile_subs, 0).astype(I32)
    tok = jnp.tile(jnp.arange(n, dtype=I32), k)
    src = jnp.zeros((n_tiles * tile,), I32).at[pos].set(tok, unique_indices=True)
    return pos.reshape(k, n), src, te, tile_subs, n_used.reshape(1).astype(I32)


def _layer_params(l, w_in_b, b_in, conv_a_w, conv_a_b, ln_a_g, ln_a_b, w_pool_bd, pool_scale,
                  conv_c_w, w_out_b, b_out, ln1_g, ln1_b, w_router, b_router):
    row = lambda a: a[l][None, :]
    return (w_in_b[l], row(b_in), conv_a_w[l], row(conv_a_b), row(ln_a_g), row(ln_a_b),
            w_pool_bd[l], row(pool_scale), conv_c_w[l], w_out_b[l], row(b_out), row(ln1_g),
            row(ln1_b), w_router[l], b_router[l])


def kernel(x_prompt, x_sample, state_conv_a, state_pool_b, state_conv_c, w_in, b_in, conv_a_w, conv_a_b, ln_a_g, ln_a_b, w_pool, pool_scale, conv_c_w, w_out, b_out, ln1_g, ln1_b, w_router_group, b_router_group, w_router_expert, b_router_expert, w_gate, w_up, w_down, ln2_g, ln2_b):
    depth = w_in.shape[0]
    bp, tp, d = x_prompt.shape
    bs, ts, _ = x_sample.shape
    ka, w_a = conv_a_w.shape[1:]
    kb, w_b = state_pool_b.shape[2:]
    kc, w_c = conv_c_w.shape[1:]
    n_groups = w_router_group.shape[-1]
    n_experts = w_router_expert.shape[-1]
    dims = (w_a, w_b, w_c, ka, kb, kc, n_groups, n_experts)
    alpha = float((2 * depth) ** 0.25)
    assert tp % PROMPT_TILE == 0 and PROMPT_TILE % ROW_CHUNK == 0 and (bs * ts) % ROW_CHUNK == 0
    assert n_groups <= SUBLANES and ROUTER_EXPERT_ROW + n_experts <= LANES
    assert d == SUBLANES * LANES

    w_in_b = w_in.astype(BF16)
    w_out_b = w_out.astype(BF16)
    n_pg, pg = w_pool.shape[1], w_pool.shape[2]
    eye = jnp.eye(n_pg, dtype=F32)
    w_pool_bd = (w_pool[:, :, :, None, :] * eye[None, :, None, :, None]).reshape(depth, n_pg * pg, n_pg * pg).astype(BF16)
    wr = jnp.zeros((depth, d, LANES), F32)
    wr = wr.at[:, :, ROUTER_GROUP_ROW:ROUTER_GROUP_ROW + n_groups].set(w_router_group)
    wr = wr.at[:, :, ROUTER_EXPERT_ROW:ROUTER_EXPERT_ROW + n_experts].set(w_router_expert)
    wr_hi = wr.astype(BF16)
    wr_lo = (wr - wr_hi.astype(F32)).astype(BF16)
    w_router = jnp.concatenate([wr_hi, wr_lo], axis=-1)
    b_router = jnp.zeros((depth, 1, LANES), F32)
    b_router = b_router.at[:, 0, ROUTER_GROUP_ROW:ROUTER_GROUP_ROW + n_groups].set(b_router_group)
    b_router = b_router.at[:, 0, ROUTER_EXPERT_ROW:ROUTER_EXPERT_ROW + n_experts].set(b_router_expert)

    xp = x_prompt.reshape(bp * tp, d)
    xs = x_sample.transpose(1, 0, 2).reshape(ts * bs, d)
    new_states = [[] for _ in range(6)]
    for l in range(depth):
        params = _layer_params(l, w_in_b, b_in, conv_a_w, conv_a_b, ln_a_g, ln_a_b, w_pool_bd,
                               pool_scale, conv_c_w, w_out_b, b_out, ln1_g, ln1_b, w_router, b_router)
        g2, b2 = ln2_g[l][None, :], ln2_b[l][None, :]

        x1, eidx, wts, pa, pb, pc = _prompt_mixer(xp, bp, tp, bs * ts, params, dims, alpha)
        x1, seidx, swts, sa, sb, sc = _sample_mixer(
            x1, xs, state_conv_a[l].reshape(bs, -1), state_pool_b[l].reshape(bs, -1),
            state_conv_c[l].reshape(bs, -1), params, dims, alpha, bs, ts)
        eidx = jnp.concatenate([eidx.transpose(1, 0, 2).reshape(TOP_K, bp * tp), seidx], axis=1)
        wts = jnp.concatenate([wts.transpose(1, 0, 2).reshape(TOP_K, bp * tp), swts], axis=1)

        pos, src, te, subs, nu = _route_tables(eidx, n_experts, MOE_TILE, MOE_SUB)
        y = _moe(x1, src, te, subs, nu, w_gate, w_up, w_down, l)
        xp, xs = _combine(x1, y, pos, wts, g2, b2, alpha, bp * tp)

        for lst, val in zip(new_states, (pa, sa.reshape(bs, ka - 1, w_a), pb, sb.reshape(bs, kb, w_b),
                                         pc, sc.reshape(bs, kc - 1, w_c))):
            lst.append(val)

    y_prompt = xp.reshape(bp, tp, d)
    y_sample = xs.reshape(ts, bs, d).transpose(1, 0, 2)
    return (y_prompt, y_sample) + tuple(jnp.stack(s) for s in new_states)
```

```python
import functools

import jax
import jax.numpy as jnp
from jax import lax
from jax.experimental import pallas as pl
from jax.experimental.pallas import tpu as pltpu

F32 = jnp.float32
BF16 = jnp.bfloat16
I32 = jnp.int32

POOL_WINDOWS = (2, 4, 8, 16)
TOP_K = 2
LN_EPS = 1e-5
PAST_LEN = 16384

LANES = 128
SUBLANES = 8
SUBLANE_SHIFT = 3
LANE_SHIFT = 7
VMEM_LIMIT_BYTES = 56 * 1024 * 1024

ROW_CHUNK = 32
PROMPT_TILE = 512
PROMPT_HIST_PITCH = 2
MOE_TILE = 1024
MOE_SUB = 256
COMBINE_TILE = 256
GATHER_UNROLL = 16
DMA_PRIORITIES = 2
INVERT_CHUNK = 4224
INVERT_UNROLL = 16
GATHER_PITCH = 9
ROUTER_GROUP_ROW = 0
ROUTER_EXPERT_ROW = 8


def _dot(a, b):
    return jnp.dot(a, b, preferred_element_type=F32)


def _layer_norm(x, g, b):
    mu = jnp.mean(x, axis=-1, keepdims=True)
    xc = x - mu
    var = jnp.mean(xc * xc, axis=-1, keepdims=True)
    return xc * lax.rsqrt(var + LN_EPS) * g + b


def _sigmoid(x):
    return 1.0 / (1.0 + jnp.exp(-x))


def _store_row_tiles(ref, val):
    rows, d = val.shape
    for j in range(d // LANES):
        ref[pl.ds(j, rows, stride=SUBLANES), :] = val[:, j * LANES:(j + 1) * LANES]


def _load_row_tiles(ref, rows, first=0, pitch=SUBLANES):
    return jnp.concatenate(
        [ref[pl.ds(first + j, rows, stride=pitch), :] for j in range(SUBLANES)], axis=-1)


def _hist_load(buf, start, n, pitch):
    parts = []
    for j in range(buf.shape[0]):
        if pitch == 1:
            parts.append(buf[j, pl.ds(start, n), :])
        else:
            parts.append(buf.at[j][pl.ds(pitch * start, n, stride=pitch), :])
    return parts[0] if len(parts) == 1 else jnp.concatenate(parts, axis=-1)


def _hist_store(buf, start, n, pitch, val):
    for j in range(buf.shape[0]):
        piece = val[:, j * LANES:(j + 1) * LANES]
        if pitch == 1:
            buf[j, pl.ds(start, n), :] = piece
        else:
            buf.at[j][pl.ds(pitch * start, n, stride=pitch), :] = piece


def _mixer_rows(x_ref, p, bufs, outs, *, rows, stride, pitch, new_a, new_b, new_c, pos_of_row,
                alpha, n_groups, n_experts):
    (w_in, b_in, wa, ba, lnag, lnab, wbd, pscale, wc, w_out, b_out, ln1g, ln1b,
     wr, br) = p
    proj, buf_a, buf_b, buf_c, dbuf, cat = bufs
    x1_ref, eidx_ref, wts_ref = outs
    ka = wa.shape[0]
    kc = wc.shape[0]
    w_a = wa.shape[1]
    w_b = pscale.shape[1]
    w_c = wc.shape[1]
    o_gate, o_ub, o_cbg, o_ccg, o_ch = w_a, 2 * w_a, 2 * w_a + w_b, 2 * w_a + w_b + w_c, 2 * w_a + w_b + 2 * w_c

    x = x_ref[...]
    proj[...] = _dot(x.astype(BF16), w_in[...]) + b_in[...]

    _hist_store(buf_a, new_a, rows, pitch, proj[:, 0:w_a] * _sigmoid(proj[:, o_gate:o_gate + w_a]))
    _hist_store(buf_b, new_b, rows, pitch, proj[:, o_ub:o_ub + w_b])
    _hist_store(buf_c, new_c, rows, pitch, proj[:, o_ccg:o_ccg + w_c] * proj[:, o_ch:o_ch + w_c])

    lane_b = lax.broadcasted_iota(I32, (ROW_CHUNK, w_b), 1)
    group_b = lane_b // (w_b // len(POOL_WINDOWS))
    win_b = jnp.zeros((ROW_CHUNK, w_b), I32)
    for g, w in enumerate(POOL_WINDOWS):
        win_b = jnp.where(group_b == g, w, win_b)

    def chunk(c, carry):
        r0 = c * ROW_CHUNK if isinstance(c, int) else pl.multiple_of(c * ROW_CHUNK, ROW_CHUNK)
        acc = jnp.zeros((ROW_CHUNK, w_a), F32) + ba[...]
        for k in range(ka):
            src = _hist_load(buf_a, r0 + (new_a - (ka - 1 - k) * stride), ROW_CHUNK, pitch)
            acc = acc + wa[k:k + 1, :] * src
        a = _layer_norm(acc, lnag[...], lnab[...])
        cat[pl.ds(r0, ROW_CHUNK), 0:w_a] = (a * _sigmoid(a)).astype(BF16)
        cur = _hist_load(buf_b, r0 + new_b, ROW_CHUNK, pitch)
        run = cur
        wsum = jnp.zeros((ROW_CHUNK, w_b), F32)
        for j in range(1, max(POOL_WINDOWS) + 1):
            if j in POOL_WINDOWS:
                wsum = jnp.where(win_b == j, run, wsum)
            if j < max(POOL_WINDOWS):
                run = run + _hist_load(buf_b, r0 + (new_b - j * stride), ROW_CHUNK, pitch)
        pos = pos_of_row(r0 + lax.broadcasted_iota(I32, (ROW_CHUNK, w_b), 0))
        cnt = jnp.minimum(pos + 1, win_b).astype(F32)
        dbuf[pl.ds(r0, ROW_CHUNK), :] = (wsum / cnt - cur).astype(BF16)
        accc = jnp.zeros((ROW_CHUNK, w_c), F32)
        for k in range(kc):
            src = _hist_load(buf_c, r0 + (new_c - (kc - 1 - k) * stride), ROW_CHUNK, pitch)
            accc = accc + wc[k:k + 1, :] * src
        cbg = proj[pl.ds(r0, ROW_CHUNK), o_cbg:o_cbg + w_c]
        cat[pl.ds(r0, ROW_CHUNK), w_a + w_b:w_a + w_b + w_c] = (cbg * accc).astype(BF16)
        return carry

    if pitch == 1:
        assert stride % SUBLANES == 0
        lax.fori_loop(0, rows // ROW_CHUNK, chunk, 0)
    else:
        for c in range(rows // ROW_CHUNK):
            chunk(c, 0)

    cat[:, w_a:w_a + w_b] = (_dot(dbuf[...], wbd[...]) * pscale[...]).astype(BF16)

    m = _dot(cat[...], w_out[...]) + b_out[...]
    x1 = _layer_norm(alpha * x + m, ln1g[...], ln1b[...])
    _store_row_tiles(x1_ref, x1)

    x1_hi = x1.astype(BF16)
    x1_lo = (x1 - x1_hi.astype(F32)).astype(BF16)
    p_hi = _dot(x1_hi, wr[...])
    p_lo = _dot(x1_lo, wr[...])
    logits = (p_hi[:, 0:LANES] + p_hi[:, LANES:2 * LANES]) + (p_lo[:, 0:LANES] + p_lo[:, LANES:2 * LANES])
    lt = jnp.transpose(logits + br[...])

    neg = jnp.float32(-jnp.inf)
    gl = lt[ROUTER_GROUP_ROW:ROUTER_GROUP_ROW + SUBLANES, :]
    grow = lax.broadcasted_iota(I32, gl.shape, 0)
    gvalid = grow < n_groups
    glm = jnp.where(gvalid, gl, neg)
    gmax = jnp.max(glm, axis=0, keepdims=True)
    gidx = jnp.min(jnp.where(glm == gmax, grow, SUBLANES), axis=0, keepdims=True)
    gsum = jnp.sum(jnp.where(gvalid, jnp.exp(gl - gmax), 0.0), axis=0, keepdims=True)
    g_p = 1.0 / gsum

    el = lt[ROUTER_EXPERT_ROW:ROUTER_EXPERT_ROW + n_experts, :]
    erow = lax.broadcasted_iota(I32, el.shape, 0)
    v = jnp.where(erow // (n_experts // n_groups) == gidx, el, neg)
    v1 = jnp.max(v, axis=0, keepdims=True)
    i1 = jnp.min(jnp.where(v == v1, erow, n_experts), axis=0, keepdims=True)
    vv = jnp.where(erow == i1, neg, v)
    v2 = jnp.max(vv, axis=0, keepdims=True)
    i2 = jnp.min(jnp.where(vv == v2, erow, n_experts), axis=0, keepdims=True)
    e2 = jnp.exp(v2 - v1)
    den = 1.0 + e2
    eidx_ref[0:1, :] = i1
    eidx_ref[1:2, :] = i2
    wts_ref[0:1, :] = (1.0 / den) * g_p
    wts_ref[1:2, :] = (e2 / den) * g_p


def _prompt_mixer_kernel(x_ref, *refs, n_params, n_seq, extra_tiles, tile, hist, pitch, alpha,
                         n_groups, n_experts):
    p = refs[:n_params]
    x1_ref, eidx_ref, wts_ref, sa_ref, sb_ref, sc_ref = refs[n_params:n_params + 6]
    bufs = refs[n_params + 6:]
    _, buf_a, buf_b, buf_c, _, _ = bufs
    ha, hb, hc = hist
    s = pl.program_id(0)
    t = pl.program_id(1)

    @pl.when(s < n_seq)
    def _():
        hists = ((buf_a, ha), (buf_b, hb), (buf_c, hc))

        @pl.when(t == 0)
        def _():
            for buf, h in hists:
                _hist_store(buf, 0, h, pitch, jnp.zeros((h, buf.shape[0] * LANES), F32))

        _mixer_rows(x_ref, p, bufs, (x1_ref, eidx_ref, wts_ref), rows=tile, stride=1, pitch=pitch,
                    new_a=ha, new_b=hb, new_c=hc,
                    pos_of_row=lambda r: r + t * tile,
                    alpha=alpha, n_groups=n_groups, n_experts=n_experts)

        tails = [_hist_load(buf, tile, h, pitch) for buf, h in hists]
        for (buf, h), tail in zip(hists, tails):
            _hist_store(buf, 0, h, pitch, tail)

        @pl.when(t == pl.num_programs(1) - 1)
        def _():
            for ref, (_, h), tail in zip((sa_ref, sb_ref, sc_ref), hists, tails):
                ref[...] = tail[h - ref.shape[0]:h, :]

    @pl.when(jnp.logical_and(s == n_seq, t < extra_tiles))
    def _():
        x1_ref[...] = jnp.zeros(x1_ref.shape, F32)


def _sample_mixer_kernel(x1_all_hbm, x_ref, sta_ref, stb_ref, stc_ref, *refs, n_params, batch, steps,
                         alpha, n_groups, n_experts):
    del x1_all_hbm
    p = refs[:n_params]
    x1_ref, eidx_ref, wts_ref, sa_ref, sb_ref, sc_ref = refs[n_params:n_params + 6]
    bufs = refs[n_params + 6:]
    _, buf_a, buf_b, buf_c, _, _ = bufs
    rows = batch * steps
    hists = []
    for buf, st_ref, out_ref in ((buf_a, sta_ref, sa_ref), (buf_b, stb_ref, sb_ref),
                                 (buf_c, stc_ref, sc_ref)):
        width = buf.shape[0] * LANES
        hists.append((buf, st_ref, out_ref, width, st_ref.shape[1] // width))

    for buf, st_ref, _, width, n in hists:
        for j in range(n):
            _hist_store(buf, j * batch, batch, 1, st_ref[:, j * width:(j + 1) * width])

    na, nb, nc = (h[4] for h in hists)
    _mixer_rows(x_ref, p, bufs, (x1_ref, eidx_ref, wts_ref), rows=rows, stride=batch, pitch=1,
                new_a=na * batch, new_b=nb * batch, new_c=nc * batch,
                pos_of_row=lambda r: PAST_LEN + r // batch,
                alpha=alpha, n_groups=n_groups, n_experts=n_experts)

    for buf, _, out_ref, width, n in hists:
        for j in range(n):
            out_ref[:, j * width:(j + 1) * width] = _hist_load(buf, (j + steps) * batch, batch, 1)


def _full_spec(a):
    nd = a.ndim
    return pl.BlockSpec(a.shape, lambda *_: (0,) * nd)


def _mixer_scratch(rows, ha, hb, hc, d_in, w_a, w_b, w_c, pitch):
    return [
        pltpu.VMEM((rows, d_in), F32),
        pltpu.VMEM((w_a // LANES, pitch * (ha + rows), LANES), F32),
        pltpu.VMEM((w_b // LANES, pitch * (hb + rows), LANES), F32),
        pltpu.VMEM((w_c // LANES, pitch * (hc + rows), LANES), F32),
        pltpu.VMEM((rows, w_b), BF16),
        pltpu.VMEM((rows, w_a + w_b + w_c), BF16),
    ]


def _round_up(n, m):
    return (n + m - 1) // m * m


def _prompt_mixer(x, b, t, extra_rows, params, dims, alpha):
    d = x.shape[1]
    w_a, w_b, w_c, ka, kb, kc, n_groups, n_experts = dims
    tile = PROMPT_TILE
    nt = t // tile
    ha, hb, hc = _round_up(ka - 1, SUBLANES), _round_up(kb, SUBLANES), _round_up(kc - 1, SUBLANES)
    d_in = params[0].shape[1]
    extra_tiles = extra_rows // tile
    assert extra_rows % tile == 0 and extra_tiles <= nt
    kern = functools.partial(_prompt_mixer_kernel, n_params=len(params), n_seq=b,
                             extra_tiles=extra_tiles, tile=tile, hist=(ha, hb, hc),
                             pitch=PROMPT_HIST_PITCH, alpha=alpha,
                             n_groups=n_groups, n_experts=n_experts)
    last_block = b * nt + extra_tiles - 1

    def seq_tile(i, j):
        return jnp.minimum(i, b - 1), jnp.where(i < b, j, nt - 1)

    def row_block(i, j):
        s, t_ = seq_tile(i, j)
        return s * nt + t_

    out_shape = (
        jax.ShapeDtypeStruct(((b * t + extra_rows) * SUBLANES, LANES), F32),
        jax.ShapeDtypeStruct((b, TOP_K, t), I32),
        jax.ShapeDtypeStruct((b, TOP_K, t), F32),
        jax.ShapeDtypeStruct((b, ka - 1, w_a), F32),
        jax.ShapeDtypeStruct((b, kb, w_b), F32),
        jax.ShapeDtypeStruct((b, kc - 1, w_c), F32),
    )
    out_specs = (
        pl.BlockSpec((tile * SUBLANES, LANES), lambda i, j: (jnp.minimum(i * nt + j, last_block), 0)),
        pl.BlockSpec((None, TOP_K, tile), lambda i, j: (seq_tile(i, j)[0], 0, seq_tile(i, j)[1])),
        pl.BlockSpec((None, TOP_K, tile), lambda i, j: (seq_tile(i, j)[0], 0, seq_tile(i, j)[1])),
        pl.BlockSpec((None, ka - 1, w_a), lambda i, j: (jnp.minimum(i, b - 1), 0, 0)),
        pl.BlockSpec((None, kb, w_b), lambda i, j: (jnp.minimum(i, b - 1), 0, 0)),
        pl.BlockSpec((None, kc - 1, w_c), lambda i, j: (jnp.minimum(i, b - 1), 0, 0)),
    )
    in_specs = [pl.BlockSpec((tile, d), lambda i, j: (row_block(i, j), 0))] + [_full_spec(a) for a in params]
    return pl.pallas_call(
        kern,
        grid=(b + 1, nt),
        in_specs=in_specs,
        out_specs=out_specs,
        out_shape=out_shape,
        scratch_shapes=_mixer_scratch(tile, ha, hb, hc, d_in, w_a, w_b, w_c, PROMPT_HIST_PITCH),
        compiler_params=pltpu.CompilerParams(
            dimension_semantics=("arbitrary", "arbitrary"),
            vmem_limit_bytes=VMEM_LIMIT_BYTES),
        name="prompt_mixer",
    )(x, *params)


def _sample_mixer(x1_all, x_tm, st_a, st_b, st_c, params, dims, alpha, batch, steps):
    rows, d = x_tm.shape
    w_a, w_b, w_c, ka, kb, kc, n_groups, n_experts = dims
    d_in = params[0].shape[1]
    first = x1_all.shape[0] // SUBLANES - rows
    assert first % rows == 0
    kern = functools.partial(_sample_mixer_kernel, n_params=len(params), batch=batch,
                             steps=steps, alpha=alpha, n_groups=n_groups, n_experts=n_experts)
    out_shape = (
        jax.ShapeDtypeStruct(x1_all.shape, F32),
        jax.ShapeDtypeStruct((TOP_K, rows), I32),
        jax.ShapeDtypeStruct((TOP_K, rows), F32),
        jax.ShapeDtypeStruct(st_a.shape, F32),
        jax.ShapeDtypeStruct(st_b.shape, F32),
        jax.ShapeDtypeStruct(st_c.shape, F32),
    )
    args = (x_tm, st_a, st_b, st_c) + tuple(params)
    out_specs = (pl.BlockSpec((rows * SUBLANES, LANES), lambda i: (first // rows, 0)),) + tuple(
        pl.BlockSpec(s.shape, lambda i, n=len(s.shape): (0,) * n) for s in out_shape[1:])
    return pl.pallas_call(
        kern,
        grid=(1,),
        in_specs=[pl.BlockSpec(memory_space=pl.ANY)] + [_full_spec(a) for a in args],
        out_specs=out_specs,
        out_shape=out_shape,
        input_output_aliases={0: 0},
        scratch_shapes=_mixer_scratch(rows, (ka - 1) * batch, kb * batch, (kc - 1) * batch,
                                      d_in, w_a, w_b, w_c, 1),
        compiler_params=pltpu.CompilerParams(
            dimension_semantics=("arbitrary",),
            vmem_limit_bytes=VMEM_LIMIT_BYTES),
        name="sample_mixer",
    )(x1_all, *args)


def _start_row_gather(src_hbm, idx_ref, dst_vmem, n, sem):
    def body(c, carry):
        for u in range(GATHER_UNROLL):
            s = c * GATHER_UNROLL + u
            row = idx_ref[0, 0, s]
            pltpu.make_async_copy(
                src_hbm.at[pl.ds(pl.multiple_of(row * SUBLANES, SUBLANES), SUBLANES)],
                dst_vmem.at[pl.ds(s * GATHER_PITCH, SUBLANES)],
                sem).start(priority=u % DMA_PRIORITIES)
        return carry
    lax.fori_loop(0, n // GATHER_UNROLL, body, 0)


def _wait_row_gather(src_hbm, dst_vmem, n, sem):
    pltpu.make_async_copy(src_hbm.at[pl.ds(0, n * SUBLANES)], dst_vmem.at[pl.ds(0, n * SUBLANES)],
                          sem).wait()


def _moe_kernel(te_ref, cnt_ref, nu_ref, src0_ref, src1_ref, x_hbm, wg_ref, wu_ref, wd_ref, y_ref,
                xbuf, wg_b, wu_b, wd_b, sem, *, tile, sub):
    i = pl.program_id(0)
    n_used = nu_ref[0]
    slot = i % 2
    nxt = jnp.minimum(i + 1, pl.num_programs(0) - 1)

    @pl.when(i == 0)
    def _():
        _start_row_gather(x_hbm, src0_ref, xbuf.at[0], cnt_ref[0] * sub, sem.at[0])

    @pl.when(i + 1 < n_used)
    def _():
        _start_row_gather(x_hbm, src1_ref, xbuf.at[1 - slot], cnt_ref[nxt] * sub, sem.at[1 - slot])

    @pl.when(i < n_used)
    def _():
        @pl.when(jnp.logical_or(i == 0, te_ref[i] != te_ref[jnp.maximum(i - 1, 0)]))
        def _():
            wg_b[...] = wg_ref[...].astype(BF16)
            wu_b[...] = wu_ref[...].astype(BF16)
            wd_b[...] = wd_ref[...].astype(BF16)

        _wait_row_gather(x_hbm, xbuf.at[slot], cnt_ref[i] * sub, sem.at[slot])

        for c in range(1, tile // sub + 1):
            @pl.when(cnt_ref[i] == c)
            def _(rows=c * sub):
                xb = _load_row_tiles(xbuf.at[slot], rows, pitch=GATHER_PITCH).astype(BF16)
                hg = _dot(xb, wg_b[...])
                hu = _dot(xb, wu_b[...])
                h = hg * _sigmoid(hg) * hu
                _store_row_tiles(y_ref, _dot(h.astype(BF16), wd_b[...]))
                if rows < tile:
                    y_ref[pl.ds(rows * SUBLANES, (tile - rows) * SUBLANES), :] = jnp.zeros(
                        ((tile - rows) * SUBLANES, LANES), F32)

    @pl.when(i >= n_used)
    def _():
        y_ref[...] = jnp.zeros(y_ref.shape, F32)


def _moe(x1, src, tile_expert, tile_subs, n_used, w_gate, w_up, w_down, layer):
    d = w_gate.shape[-2]
    tile = MOE_TILE
    n_tiles = src.shape[0] // tile
    d_e = w_gate.shape[-1]
    grid_spec = pltpu.PrefetchScalarGridSpec(
        num_scalar_prefetch=3,
        grid=(n_tiles,),
        in_specs=[
            pl.BlockSpec((1, 1, tile), lambda i, te, cnt, nu: (i, 0, 0), memory_space=pltpu.SMEM),
            pl.BlockSpec((1, 1, tile), lambda i, te, cnt, nu: (jnp.minimum(i + 1, n_tiles - 1), 0, 0),
                         memory_space=pltpu.SMEM),
            pl.BlockSpec(memory_space=pl.ANY),
            pl.BlockSpec((None, None, d, d_e), lambda i, te, cnt, nu: (layer, te[i], 0, 0)),
            pl.BlockSpec((None, None, d, d_e), lambda i, te, cnt, nu: (layer, te[i], 0, 0)),
            pl.BlockSpec((None, None, d_e, d), lambda i, te, cnt, nu: (layer, te[i], 0, 0)),
        ],
        out_specs=pl.BlockSpec((tile * SUBLANES, LANES), lambda i, te, cnt, nu: (i, 0)),
        scratch_shapes=[
            pltpu.VMEM((2, tile * GATHER_PITCH, LANES), F32),
            pltpu.VMEM((d, d_e), BF16),
            pltpu.VMEM((d, d_e), BF16),
            pltpu.VMEM((d_e, d), BF16),
            pltpu.SemaphoreType.DMA((2,)),
        ],
    )
    src3 = src.reshape(n_tiles, 1, tile)
    return pl.pallas_call(
        functools.partial(_moe_kernel, tile=tile, sub=MOE_SUB),
        grid_spec=grid_spec,
        out_shape=jax.ShapeDtypeStruct((n_tiles * tile * SUBLANES, LANES), F32),
        compiler_params=pltpu.CompilerParams(
            dimension_semantics=("arbitrary",),
            vmem_limit_bytes=VMEM_LIMIT_BYTES),
        name="moe",
    )(tile_expert, tile_subs, n_used, src3, src3, x1, w_gate, w_up, w_down)


def _combine_kernel(pos0_ref, pos1_ref, x1_ref, w_ref, g_ref, b_ref, y_hbm, op_ref, os_ref, ybuf, sem,
                    *, tile, alpha, prompt_tiles):
    i = pl.program_id(0)
    rows = TOP_K * tile
    slot = i % 2

    @pl.when(i == 0)
    def _():
        _start_row_gather(y_hbm, pos0_ref, ybuf.at[0], rows, sem.at[0])

    @pl.when(i + 1 < pl.num_programs(0))
    def _():
        _start_row_gather(y_hbm, pos1_ref, ybuf.at[1 - slot], rows, sem.at[1 - slot])

    _wait_row_gather(y_hbm, ybuf.at[slot], rows, sem.at[slot])
    w = w_ref[...]
    y_a = _load_row_tiles(ybuf.at[slot], tile, pitch=GATHER_PITCH)
    y_b = _load_row_tiles(ybuf.at[slot], tile, first=tile * GATHER_PITCH, pitch=GATHER_PITCH)
    moe = w[:, 0:1] * y_a + w[:, 1:2] * y_b
    x1 = _load_row_tiles(x1_ref, tile)
    out = _layer_norm(alpha * x1 + moe, g_ref[...], b_ref[...])

    @pl.when(i < prompt_tiles)
    def _():
        op_ref[...] = out

    @pl.when(i >= prompt_tiles)
    def _():
        os_ref[...] = out


def _combine(x1, y, pos, wts, g, b, alpha, n_prompt):
    n, d = x1.shape[0] // SUBLANES, g.shape[-1]
    tile = COMBINE_TILE
    n_tiles = n // tile
    prompt_tiles = n_prompt // tile
    assert n % tile == 0 and n_prompt % tile == 0 and 0 < prompt_tiles < n_tiles
    pos_t = pos.reshape(TOP_K, n_tiles, tile).transpose(1, 0, 2).reshape(n_tiles, 1, TOP_K * tile)
    return pl.pallas_call(
        functools.partial(_combine_kernel, tile=tile, alpha=alpha, prompt_tiles=prompt_tiles),
        grid=(n_tiles,),
        in_specs=[
            pl.BlockSpec((1, 1, TOP_K * tile), lambda i: (i, 0, 0), memory_space=pltpu.SMEM),
            pl.BlockSpec((1, 1, TOP_K * tile), lambda i: (jnp.minimum(i + 1, n_tiles - 1), 0, 0),
                         memory_space=pltpu.SMEM),
            pl.BlockSpec((tile * SUBLANES, LANES), lambda i: (i, 0)),
            pl.BlockSpec((tile, TOP_K), lambda i: (i, 0)),
            _full_spec(g),
            _full_spec(b),
            pl.BlockSpec(memory_space=pl.ANY),
        ],
        out_specs=(
            pl.BlockSpec((tile, d), lambda i: (jnp.minimum(i, prompt_tiles - 1), 0)),
            pl.BlockSpec((tile, d), lambda i: (jnp.maximum(i - prompt_tiles, 0), 0)),
        ),
        out_shape=(jax.ShapeDtypeStruct((n_prompt, d), F32),
                   jax.ShapeDtypeStruct((n - n_prompt, d), F32)),
        scratch_shapes=[pltpu.VMEM((2, TOP_K * tile * GATHER_PITCH, LANES), F32),
                        pltpu.SemaphoreType.DMA((2,))],
        compiler_params=pltpu.CompilerParams(
            dimension_semantics=("arbitrary",),
            vmem_limit_bytes=VMEM_LIMIT_BYTES),
        name="combine",
    )(pos_t, pos_t, x1, wts.T, g, b, y)


def _route_tables(eidx, n_experts, tile, sub):
    k, n = eidx.shape
    pairs = k * n
    n_tiles = pairs // tile + n_experts
    e = eidx.reshape(pairs)
    onehot = (e[:, None] == jnp.arange(n_experts, dtype=I32)[None, :]).astype(I32)
    csum = jnp.cumsum(onehot, axis=0)
    rank = jnp.sum(onehot * csum, axis=1) - 1
    counts = csum[-1]
    ptiles = (counts + tile - 1) // tile
    tile_end = jnp.cumsum(ptiles)
    offs = (tile_end - ptiles) * tile
    pos = jnp.sum(onehot * offs[None, :], axis=1) + rank
    n_used = tile_end[-1]
    j = jnp.arange(n_tiles, dtype=I32)
    te = jnp.sum((j[:, None] >= tile_end[None, :]).astype(I32), axis=1)
    te_last = jnp.sum((n_used - 1 >= tile_end).astype(I32))
    te = jnp.where(j < n_used, te, te_last).astype(I32)
    subs_per_tile = tile // sub
    first_tile = (tile_end - ptiles)[te]
    real_subs = ((counts + sub - 1) // sub)[te]
    tile_subs = jnp.clip(real_subs - (j - first_tile) * subs_per_tile, 0, subs_per_tile)
    tile_subs = jnp.where(j < n_used, tile_subs, 0).astype(I32)
    src = _invert_positions(pos, n, n_tiles * tile)
    return pos.reshape(k, n), src, te, tile_subs, n_used.reshape(1).astype(I32)


def _invert_kernel(pos_ref, src_ref, zeros_ref, *, chunk, n):
    i = pl.program_id(0)

    @pl.when(i == 0)
    def _():
        zeros_ref[...] = jnp.zeros(zeros_ref.shape, I32)
        pltpu.sync_copy(zeros_ref, src_ref)

    first_tok = (i * chunk) % n

    def body(c, carry):
        for u in range(INVERT_UNROLL):
            s = c * INVERT_UNROLL + u
            dst = pos_ref[0, 0, s]
            src_ref[dst] = first_tok + s
        return carry
    lax.fori_loop(0, chunk // INVERT_UNROLL, body, 0)


def _invert_positions(pos, n, rows):
    pairs = pos.shape[0]
    chunk = INVERT_CHUNK
    assert pairs % chunk == 0 and n % chunk == 0 and rows % LANES == 0
    src = pl.pallas_call(
        functools.partial(_invert_kernel, chunk=chunk, n=n),
        grid=(pairs // chunk,),
        in_specs=[pl.BlockSpec((1, 1, chunk), lambda i: (i, 0, 0), memory_space=pltpu.SMEM)],
        out_specs=pl.BlockSpec(memory_space=pltpu.SMEM),
        out_shape=jax.ShapeDtypeStruct((rows,), I32),
        scratch_shapes=[pltpu.VMEM((rows,), I32)],
        compiler_params=pltpu.CompilerParams(dimension_semantics=("arbitrary",)),
        name="invert_positions",
    )(pos.reshape(pairs // chunk, 1, chunk))
    return src


def _layer_params(l, w_in_b, b_in, conv_a_w, conv_a_b, ln_a_g, ln_a_b, w_pool_bd, pool_scale,
                  conv_c_w, w_out_b, b_out, ln1_g, ln1_b, w_router, b_router):
    row = lambda a: a[l][None, :]
    return (w_in_b[l], row(b_in), conv_a_w[l], row(conv_a_b), row(ln_a_g), row(ln_a_b),
            w_pool_bd[l], row(pool_scale), conv_c_w[l], w_out_b[l], row(b_out), row(ln1_g),
            row(ln1_b), w_router[l], b_router[l])


def kernel(x_prompt, x_sample, state_conv_a, state_pool_b, state_conv_c, w_in, b_in, conv_a_w, conv_a_b, ln_a_g, ln_a_b, w_pool, pool_scale, conv_c_w, w_out, b_out, ln1_g, ln1_b, w_router_group, b_router_group, w_router_expert, b_router_expert, w_gate, w_up, w_down, ln2_g, ln2_b):
    depth = w_in.shape[0]
    bp, tp, d = x_prompt.shape
    bs, ts, _ = x_sample.shape
    ka, w_a = conv_a_w.shape[1:]
    kb, w_b = state_pool_b.shape[2:]
    kc, w_c = conv_c_w.shape[1:]
    n_groups = w_router_group.shape[-1]
    n_experts = w_router_expert.shape[-1]
    dims = (w_a, w_b, w_c, ka, kb, kc, n_groups, n_experts)
    alpha = float((2 * depth) ** 0.25)
    assert tp % PROMPT_TILE == 0 and PROMPT_TILE % ROW_CHUNK == 0 and (bs * ts) % ROW_CHUNK == 0
    assert n_groups <= SUBLANES and ROUTER_EXPERT_ROW + n_experts <= LANES
    assert d == SUBLANES * LANES

    w_in_b = w_in.astype(BF16)
    w_out_b = w_out.astype(BF16)
    n_pg, pg = w_pool.shape[1], w_pool.shape[2]
    eye = jnp.eye(n_pg, dtype=F32)
    w_pool_bd = (w_pool[:, :, :, None, :] * eye[None, :, None, :, None]).reshape(depth, n_pg * pg, n_pg * pg).astype(BF16)
    wr = jnp.zeros((depth, d, LANES), F32)
    wr = wr.at[:, :, ROUTER_GROUP_ROW:ROUTER_GROUP_ROW + n_groups].set(w_router_group)
    wr = wr.at[:, :, ROUTER_EXPERT_ROW:ROUTER_EXPERT_ROW + n_experts].set(w_router_expert)
    wr_hi = wr.astype(BF16)
    wr_lo = (wr - wr_hi.astype(F32)).astype(BF16)
    w_router = jnp.concatenate([wr_hi, wr_lo], axis=-1)
    b_router = jnp.zeros((depth, 1, LANES), F32)
    b_router = b_router.at[:, 0, ROUTER_GROUP_ROW:ROUTER_GROUP_ROW + n_groups].set(b_router_group)
    b_router = b_router.at[:, 0, ROUTER_EXPERT_ROW:ROUTER_EXPERT_ROW + n_experts].set(b_router_expert)

    xp = x_prompt.reshape(bp * tp, d)
    xs = x_sample.transpose(1, 0, 2).reshape(ts * bs, d)
    new_states = [[] for _ in range(6)]
    for l in range(depth):
        params = _layer_params(l, w_in_b, b_in, conv_a_w, conv_a_b, ln_a_g, ln_a_b, w_pool_bd,
                               pool_scale, conv_c_w, w_out_b, b_out, ln1_g, ln1_b, w_router, b_router)
        g2, b2 = ln2_g[l][None, :], ln2_b[l][None, :]

        x1, eidx, wts, pa, pb, pc = _prompt_mixer(xp, bp, tp, bs * ts, params, dims, alpha)
        x1, seidx, swts, sa, sb, sc = _sample_mixer(
            x1, xs, state_conv_a[l].reshape(bs, -1), state_pool_b[l].reshape(bs, -1),
            state_conv_c[l].reshape(bs, -1), params, dims, alpha, bs, ts)
        eidx = jnp.concatenate([eidx.transpose(1, 0, 2).reshape(TOP_K, bp * tp), seidx], axis=1)
        wts = jnp.concatenate([wts.transpose(1, 0, 2).reshape(TOP_K, bp * tp), swts], axis=1)

        pos, src, te, subs, nu = _route_tables(eidx, n_experts, MOE_TILE, MOE_SUB)
        y = _moe(x1, src, te, subs, nu, w_gate, w_up, w_down, l)
        xp, xs = _combine(x1, y, pos, wts, g2, b2, alpha, bp * tp)

        for lst, val in zip(new_states, (pa, sa.reshape(bs, ka - 1, w_a), pb, sb.reshape(bs, kb, w_b),
                                         pc, sc.reshape(bs, kc - 1, w_c))):
            lst.append(val)

    y_prompt = xp.reshape(bp, tp, d)
    y_sample = xs.reshape(ts, bs, d).transpose(1, 0, 2)
    return (y_prompt, y_sample) + tuple(jnp.stack(s) for s in new_states)
```

```python
import functools

import jax
import jax.numpy as jnp
from jax import lax
from jax.experimental import pallas as pl
from jax.experimental.pallas import tpu as pltpu

F32 = jnp.float32
BF16 = jnp.bfloat16
I32 = jnp.int32

POOL_WINDOWS = (2, 4, 8, 16)
TOP_K = 2
LN_EPS = 1e-5
PAST_LEN = 16384

LANES = 128
SUBLANES = 8
SUBLANE_SHIFT = 3
LANE_SHIFT = 7
VMEM_LIMIT_BYTES = 56 * 1024 * 1024

ROW_CHUNK = 32
PROMPT_TILE = 512
PROMPT_HIST_PITCH = 2
MOE_TILE = 1024
MOE_SUB = 256
COMBINE_TILE = 512
GATHER_UNROLL = 16
DMA_PRIORITIES = 2
INVERT_CHUNK = 4224
INVERT_UNROLL = 16
GATHER_PITCH = 9
ROUTER_GROUP_ROW = 0
ROUTER_EXPERT_ROW = 8


def _dot(a, b):
    return jnp.dot(a, b, preferred_element_type=F32)


def _layer_norm(x, g, b):
    mu = jnp.mean(x, axis=-1, keepdims=True)
    xc = x - mu
    var = jnp.mean(xc * xc, axis=-1, keepdims=True)
    return xc * lax.rsqrt(var + LN_EPS) * g + b


def _sigmoid(x):
    return 1.0 / (1.0 + jnp.exp(-x))


def _store_row_tiles(ref, val):
    rows, d = val.shape
    for j in range(d // LANES):
        ref[pl.ds(j, rows, stride=SUBLANES), :] = val[:, j * LANES:(j + 1) * LANES]


def _load_row_tiles(ref, rows, first=0, pitch=SUBLANES):
    return jnp.concatenate(
        [ref[pl.ds(first + j, rows, stride=pitch), :] for j in range(SUBLANES)], axis=-1)


def _hist_load(buf, start, n, pitch):
    parts = []
    for j in range(buf.shape[0]):
        if pitch == 1:
            parts.append(buf[j, pl.ds(start, n), :])
        else:
            parts.append(buf.at[j][pl.ds(pitch * start, n, stride=pitch), :])
    return parts[0] if len(parts) == 1 else jnp.concatenate(parts, axis=-1)


def _hist_store(buf, start, n, pitch, val):
    for j in range(buf.shape[0]):
        piece = val[:, j * LANES:(j + 1) * LANES]
        if pitch == 1:
            buf[j, pl.ds(start, n), :] = piece
        else:
            buf.at[j][pl.ds(pitch * start, n, stride=pitch), :] = piece


def _mixer_rows(x_ref, p, bufs, outs, *, rows, stride, pitch, new_a, new_b, new_c, pos_of_row,
                alpha, n_groups, n_experts):
    (w_in, b_in, wa, ba, lnag, lnab, wbd, pscale, wc, w_out, b_out, ln1g, ln1b,
     wr, br) = p
    proj, buf_a, buf_b, buf_c, dbuf, cat = bufs
    x1_ref, eidx_ref, wts_ref = outs
    ka = wa.shape[0]
    kc = wc.shape[0]
    w_a = wa.shape[1]
    w_b = pscale.shape[1]
    w_c = wc.shape[1]
    o_gate, o_ub, o_cbg, o_ccg, o_ch = w_a, 2 * w_a, 2 * w_a + w_b, 2 * w_a + w_b + w_c, 2 * w_a + w_b + 2 * w_c

    x = x_ref[...]
    proj[...] = _dot(x.astype(BF16), w_in[...]) + b_in[...]

    _hist_store(buf_a, new_a, rows, pitch, proj[:, 0:w_a] * _sigmoid(proj[:, o_gate:o_gate + w_a]))
    _hist_store(buf_b, new_b, rows, pitch, proj[:, o_ub:o_ub + w_b])
    _hist_store(buf_c, new_c, rows, pitch, proj[:, o_ccg:o_ccg + w_c] * proj[:, o_ch:o_ch + w_c])

    lane_b = lax.broadcasted_iota(I32, (ROW_CHUNK, w_b), 1)
    group_b = lane_b // (w_b // len(POOL_WINDOWS))
    win_b = jnp.zeros((ROW_CHUNK, w_b), I32)
    for g, w in enumerate(POOL_WINDOWS):
        win_b = jnp.where(group_b == g, w, win_b)

    def chunk(c, carry):
        r0 = c * ROW_CHUNK if isinstance(c, int) else pl.multiple_of(c * ROW_CHUNK, ROW_CHUNK)
        acc = jnp.zeros((ROW_CHUNK, w_a), F32) + ba[...]
        for k in range(ka):
            src = _hist_load(buf_a, r0 + (new_a - (ka - 1 - k) * stride), ROW_CHUNK, pitch)
            acc = acc + wa[k:k + 1, :] * src
        a = _layer_norm(acc, lnag[...], lnab[...])
        cat[pl.ds(r0, ROW_CHUNK), 0:w_a] = (a * _sigmoid(a)).astype(BF16)
        cur = _hist_load(buf_b, r0 + new_b, ROW_CHUNK, pitch)
        run = cur
        wsum = jnp.zeros((ROW_CHUNK, w_b), F32)
        for j in range(1, max(POOL_WINDOWS) + 1):
            if j in POOL_WINDOWS:
                wsum = jnp.where(win_b == j, run, wsum)
            if j < max(POOL_WINDOWS):
                run = run + _hist_load(buf_b, r0 + (new_b - j * stride), ROW_CHUNK, pitch)
        pos = pos_of_row(r0 + lax.broadcasted_iota(I32, (ROW_CHUNK, w_b), 0))
        cnt = jnp.minimum(pos + 1, win_b).astype(F32)
        dbuf[pl.ds(r0, ROW_CHUNK), :] = (wsum / cnt - cur).astype(BF16)
        accc = jnp.zeros((ROW_CHUNK, w_c), F32)
        for k in range(kc):
            src = _hist_load(buf_c, r0 + (new_c - (kc - 1 - k) * stride), ROW_CHUNK, pitch)
            accc = accc + wc[k:k + 1, :] * src
        cbg = proj[pl.ds(r0, ROW_CHUNK), o_cbg:o_cbg + w_c]
        cat[pl.ds(r0, ROW_CHUNK), w_a + w_b:w_a + w_b + w_c] = (cbg * accc).astype(BF16)
        return carry

    if pitch == 1:
        assert stride % SUBLANES == 0
        lax.fori_loop(0, rows // ROW_CHUNK, chunk, 0)
    else:
        for c in range(rows // ROW_CHUNK):
            chunk(c, 0)

    cat[:, w_a:w_a + w_b] = (_dot(dbuf[...], wbd[...]) * pscale[...]).astype(BF16)

    m = _dot(cat[...], w_out[...]) + b_out[...]
    x1 = _layer_norm(alpha * x + m, ln1g[...], ln1b[...])
    _store_row_tiles(x1_ref, x1)

    x1_hi = x1.astype(BF16)
    x1_lo = (x1 - x1_hi.astype(F32)).astype(BF16)
    p_hi = _dot(x1_hi, wr[...])
    p_lo = _dot(x1_lo, wr[...])
    logits = (p_hi[:, 0:LANES] + p_hi[:, LANES:2 * LANES]) + (p_lo[:, 0:LANES] + p_lo[:, LANES:2 * LANES])
    lt = jnp.transpose(logits + br[...])

    neg = jnp.float32(-jnp.inf)
    gl = lt[ROUTER_GROUP_ROW:ROUTER_GROUP_ROW + SUBLANES, :]
    grow = lax.broadcasted_iota(I32, gl.shape, 0)
    gvalid = grow < n_groups
    glm = jnp.where(gvalid, gl, neg)
    gmax = jnp.max(glm, axis=0, keepdims=True)
    gidx = jnp.min(jnp.where(glm == gmax, grow, SUBLANES), axis=0, keepdims=True)
    gsum = jnp.sum(jnp.where(gvalid, jnp.exp(gl - gmax), 0.0), axis=0, keepdims=True)
    g_p = 1.0 / gsum

    el = lt[ROUTER_EXPERT_ROW:ROUTER_EXPERT_ROW + n_experts, :]
    erow = lax.broadcasted_iota(I32, el.shape, 0)
    v = jnp.where(erow // (n_experts // n_groups) == gidx, el, neg)
    v1 = jnp.max(v, axis=0, keepdims=True)
    i1 = jnp.min(jnp.where(v == v1, erow, n_experts), axis=0, keepdims=True)
    vv = jnp.where(erow == i1, neg, v)
    v2 = jnp.max(vv, axis=0, keepdims=True)
    i2 = jnp.min(jnp.where(vv == v2, erow, n_experts), axis=0, keepdims=True)
    e2 = jnp.exp(v2 - v1)
    den = 1.0 + e2
    eidx_ref[0:1, :] = i1
    eidx_ref[1:2, :] = i2
    wts_ref[0:1, :] = (1.0 / den) * g_p
    wts_ref[1:2, :] = (e2 / den) * g_p


def _prompt_mixer_kernel(x_ref, *refs, n_params, n_seq, extra_tiles, tile, hist, pitch, alpha,
                         n_groups, n_experts):
    p = refs[:n_params]
    x1_ref, eidx_ref, wts_ref, sa_ref, sb_ref, sc_ref = refs[n_params:n_params + 6]
    bufs = refs[n_params + 6:]
    _, buf_a, buf_b, buf_c, _, _ = bufs
    ha, hb, hc = hist
    s = pl.program_id(0)
    t = pl.program_id(1)

    @pl.when(s < n_seq)
    def _():
        hists = ((buf_a, ha), (buf_b, hb), (buf_c, hc))

        @pl.when(t == 0)
        def _():
            for buf, h in hists:
                _hist_store(buf, 0, h, pitch, jnp.zeros((h, buf.shape[0] * LANES), F32))

        _mixer_rows(x_ref, p, bufs, (x1_ref, eidx_ref, wts_ref), rows=tile, stride=1, pitch=pitch,
                    new_a=ha, new_b=hb, new_c=hc,
                    pos_of_row=lambda r: r + t * tile,
                    alpha=alpha, n_groups=n_groups, n_experts=n_experts)

        tails = [_hist_load(buf, tile, h, pitch) for buf, h in hists]
        for (buf, h), tail in zip(hists, tails):
            _hist_store(buf, 0, h, pitch, tail)

        @pl.when(t == pl.num_programs(1) - 1)
        def _():
            for ref, (_, h), tail in zip((sa_ref, sb_ref, sc_ref), hists, tails):
                ref[...] = tail[h - ref.shape[0]:h, :]

    @pl.when(jnp.logical_and(s == n_seq, t < extra_tiles))
    def _():
        x1_ref[...] = jnp.zeros(x1_ref.shape, F32)


def _sample_mixer_kernel(x1_all_hbm, x_ref, sta_ref, stb_ref, stc_ref, *refs, n_params, batch, steps,
                         alpha, n_groups, n_experts):
    del x1_all_hbm
    p = refs[:n_params]
    x1_ref, eidx_ref, wts_ref, sa_ref, sb_ref, sc_ref = refs[n_params:n_params + 6]
    bufs = refs[n_params + 6:]
    _, buf_a, buf_b, buf_c, _, _ = bufs
    rows = batch * steps
    hists = []
    for buf, st_ref, out_ref in ((buf_a, sta_ref, sa_ref), (buf_b, stb_ref, sb_ref),
                                 (buf_c, stc_ref, sc_ref)):
        width = buf.shape[0] * LANES
        hists.append((buf, st_ref, out_ref, width, st_ref.shape[1] // width))

    for buf, st_ref, _, width, n in hists:
        for j in range(n):
            _hist_store(buf, j * batch, batch, 1, st_ref[:, j * width:(j + 1) * width])

    na, nb, nc = (h[4] for h in hists)
    _mixer_rows(x_ref, p, bufs, (x1_ref, eidx_ref, wts_ref), rows=rows, stride=batch, pitch=1,
                new_a=na * batch, new_b=nb * batch, new_c=nc * batch,
                pos_of_row=lambda r: PAST_LEN + r // batch,
                alpha=alpha, n_groups=n_groups, n_experts=n_experts)

    for buf, _, out_ref, width, n in hists:
        for j in range(n):
            out_ref[:, j * width:(j + 1) * width] = _hist_load(buf, (j + steps) * batch, batch, 1)


def _full_spec(a):
    nd = a.ndim
    return pl.BlockSpec(a.shape, lambda *_: (0,) * nd)


def _mixer_scratch(rows, ha, hb, hc, d_in, w_a, w_b, w_c, pitch):
    return [
        pltpu.VMEM((rows, d_in), F32),
        pltpu.VMEM((w_a // LANES, pitch * (ha + rows), LANES), F32),
        pltpu.VMEM((w_b // LANES, pitch * (hb + rows), LANES), F32),
        pltpu.VMEM((w_c // LANES, pitch * (hc + rows), LANES), F32),
        pltpu.VMEM((rows, w_b), BF16),
        pltpu.VMEM((rows, w_a + w_b + w_c), BF16),
    ]


def _round_up(n, m):
    return (n + m - 1) // m * m


def _prompt_mixer(x, b, t, extra_rows, params, dims, alpha):
    d = x.shape[1]
    w_a, w_b, w_c, ka, kb, kc, n_groups, n_experts = dims
    tile = PROMPT_TILE
    nt = t // tile
    ha, hb, hc = _round_up(ka - 1, SUBLANES), _round_up(kb, SUBLANES), _round_up(kc - 1, SUBLANES)
    d_in = params[0].shape[1]
    extra_tiles = extra_rows // tile
    assert extra_rows % tile == 0 and extra_tiles <= nt
    kern = functools.partial(_prompt_mixer_kernel, n_params=len(params), n_seq=b,
                             extra_tiles=extra_tiles, tile=tile, hist=(ha, hb, hc),
                             pitch=PROMPT_HIST_PITCH, alpha=alpha,
                             n_groups=n_groups, n_experts=n_experts)
    last_block = b * nt + extra_tiles - 1

    def seq_tile(i, j):
        return jnp.minimum(i, b - 1), jnp.where(i < b, j, nt - 1)

    def row_block(i, j):
        s, t_ = seq_tile(i, j)
        return s * nt + t_

    out_shape = (
        jax.ShapeDtypeStruct(((b * t + extra_rows) * SUBLANES, LANES), F32),
        jax.ShapeDtypeStruct((b, TOP_K, t), I32),
        jax.ShapeDtypeStruct((b, TOP_K, t), F32),
        jax.ShapeDtypeStruct((b, ka - 1, w_a), F32),
        jax.ShapeDtypeStruct((b, kb, w_b), F32),
        jax.ShapeDtypeStruct((b, kc - 1, w_c), F32),
    )
    out_specs = (
        pl.BlockSpec((tile * SUBLANES, LANES), lambda i, j: (jnp.minimum(i * nt + j, last_block), 0)),
        pl.BlockSpec((None, TOP_K, tile), lambda i, j: (seq_tile(i, j)[0], 0, seq_tile(i, j)[1])),
        pl.BlockSpec((None, TOP_K, tile), lambda i, j: (seq_tile(i, j)[0], 0, seq_tile(i, j)[1])),
        pl.BlockSpec((None, ka - 1, w_a), lambda i, j: (jnp.minimum(i, b - 1), 0, 0)),
        pl.BlockSpec((None, kb, w_b), lambda i, j: (jnp.minimum(i, b - 1), 0, 0)),
        pl.BlockSpec((None, kc - 1, w_c), lambda i, j: (jnp.minimum(i, b - 1), 0, 0)),
    )
    in_specs = [pl.BlockSpec((tile, d), lambda i, j: (row_block(i, j), 0))] + [_full_spec(a) for a in params]
    return pl.pallas_call(
        kern,
        grid=(b + 1, nt),
        in_specs=in_specs,
        out_specs=out_specs,
        out_shape=out_shape,
        scratch_shapes=_mixer_scratch(tile, ha, hb, hc, d_in, w_a, w_b, w_c, PROMPT_HIST_PITCH),
        compiler_params=pltpu.CompilerParams(
            dimension_semantics=("arbitrary", "arbitrary"),
            vmem_limit_bytes=VMEM_LIMIT_BYTES),
        name="prompt_mixer",
    )(x, *params)


def _sample_mixer(x1_all, x_tm, st_a, st_b, st_c, params, dims, alpha, batch, steps):
    rows, d = x_tm.shape
    w_a, w_b, w_c, ka, kb, kc, n_groups, n_experts = dims
    d_in = params[0].shape[1]
    first = x1_all.shape[0] // SUBLANES - rows
    assert first % rows == 0
    kern = functools.partial(_sample_mixer_kernel, n_params=len(params), batch=batch,
                             steps=steps, alpha=alpha, n_groups=n_groups, n_experts=n_experts)
    out_shape = (
        jax.ShapeDtypeStruct(x1_all.shape, F32),
        jax.ShapeDtypeStruct((TOP_K, rows), I32),
        jax.ShapeDtypeStruct((TOP_K, rows), F32),
        jax.ShapeDtypeStruct(st_a.shape, F32),
        jax.ShapeDtypeStruct(st_b.shape, F32),
        jax.ShapeDtypeStruct(st_c.shape, F32),
    )
    args = (x_tm, st_a, st_b, st_c) + tuple(params)
    out_specs = (pl.BlockSpec((rows * SUBLANES, LANES), lambda i: (first // rows, 0)),) + tuple(
        pl.BlockSpec(s.shape, lambda i, n=len(s.shape): (0,) * n) for s in out_shape[1:])
    return pl.pallas_call(
        kern,
        grid=(1,),
        in_specs=[pl.BlockSpec(memory_space=pl.ANY)] + [_full_spec(a) for a in args],
        out_specs=out_specs,
        out_shape=out_shape,
        input_output_aliases={0: 0},
        scratch_shapes=_mixer_scratch(rows, (ka - 1) * batch, kb * batch, (kc - 1) * batch,
                                      d_in, w_a, w_b, w_c, 1),
        compiler_params=pltpu.CompilerParams(
            dimension_semantics=("arbitrary",),
            vmem_limit_bytes=VMEM_LIMIT_BYTES),
        name="sample_mixer",
    )(x1_all, *args)


def _start_row_gather(src_hbm, idx_ref, dst_vmem, n, sem):
    def body(c, carry):
        for u in range(GATHER_UNROLL):
            s = c * GATHER_UNROLL + u
            row = idx_ref[0, 0, s]
            pltpu.make_async_copy(
                src_hbm.at[pl.ds(pl.multiple_of(row * SUBLANES, SUBLANES), SUBLANES)],
                dst_vmem.at[pl.ds(s * GATHER_PITCH, SUBLANES)],
                sem).start(priority=u % DMA_PRIORITIES)
        return carry
    lax.fori_loop(0, n // GATHER_UNROLL, body, 0)


def _wait_row_gather(src_hbm, dst_vmem, n, sem):
    pltpu.make_async_copy(src_hbm.at[pl.ds(0, n * SUBLANES)], dst_vmem.at[pl.ds(0, n * SUBLANES)],
                          sem).wait()


def _moe_kernel(te_ref, cnt_ref, nu_ref, src0_ref, src1_ref, x_hbm, wg_ref, wu_ref, wd_ref, y_ref,
                xbuf, wg_b, wu_b, wd_b, sem, *, tile, sub):
    i = pl.program_id(0)
    n_used = nu_ref[0]
    slot = i % 2
    nxt = jnp.minimum(i + 1, pl.num_programs(0) - 1)

    @pl.when(i == 0)
    def _():
        _start_row_gather(x_hbm, src0_ref, xbuf.at[0], cnt_ref[0] * sub, sem.at[0])

    @pl.when(i + 1 < n_used)
    def _():
        _start_row_gather(x_hbm, src1_ref, xbuf.at[1 - slot], cnt_ref[nxt] * sub, sem.at[1 - slot])

    @pl.when(i < n_used)
    def _():
        @pl.when(jnp.logical_or(i == 0, te_ref[i] != te_ref[jnp.maximum(i - 1, 0)]))
        def _():
            wg_b[...] = wg_ref[...].astype(BF16)
            wu_b[...] = wu_ref[...].astype(BF16)
            wd_b[...] = wd_ref[...].astype(BF16)

        _wait_row_gather(x_hbm, xbuf.at[slot], cnt_ref[i] * sub, sem.at[slot])

        for c in range(1, tile // sub + 1):
            @pl.when(cnt_ref[i] == c)
            def _(rows=c * sub):
                xb = _load_row_tiles(xbuf.at[slot], rows, pitch=GATHER_PITCH).astype(BF16)
                hg = _dot(xb, wg_b[...])
                hu = _dot(xb, wu_b[...])
                h = hg * _sigmoid(hg) * hu
                _store_row_tiles(y_ref, _dot(h.astype(BF16), wd_b[...]))
                if rows < tile:
                    y_ref[pl.ds(rows * SUBLANES, (tile - rows) * SUBLANES), :] = jnp.zeros(
                        ((tile - rows) * SUBLANES, LANES), F32)

    @pl.when(i >= n_used)
    def _():
        y_ref[...] = jnp.zeros(y_ref.shape, F32)


def _moe(x1, src, tile_expert, tile_subs, n_used, w_gate, w_up, w_down, layer):
    d = w_gate.shape[-2]
    tile = MOE_TILE
    n_tiles = src.shape[0] // tile
    d_e = w_gate.shape[-1]
    grid_spec = pltpu.PrefetchScalarGridSpec(
        num_scalar_prefetch=3,
        grid=(n_tiles,),
        in_specs=[
            pl.BlockSpec((1, 1, tile), lambda i, te, cnt, nu: (i, 0, 0), memory_space=pltpu.SMEM),
            pl.BlockSpec((1, 1, tile), lambda i, te, cnt, nu: (jnp.minimum(i + 1, n_tiles - 1), 0, 0),
                         memory_space=pltpu.SMEM),
            pl.BlockSpec(memory_space=pl.ANY),
            pl.BlockSpec((None, None, d, d_e), lambda i, te, cnt, nu: (layer, te[i], 0, 0)),
            pl.BlockSpec((None, None, d, d_e), lambda i, te, cnt, nu: (layer, te[i], 0, 0)),
            pl.BlockSpec((None, None, d_e, d), lambda i, te, cnt, nu: (layer, te[i], 0, 0)),
        ],
        out_specs=pl.BlockSpec((tile * SUBLANES, LANES), lambda i, te, cnt, nu: (i, 0)),
        scratch_shapes=[
            pltpu.VMEM((2, tile * GATHER_PITCH, LANES), F32),
            pltpu.VMEM((d, d_e), BF16),
            pltpu.VMEM((d, d_e), BF16),
            pltpu.VMEM((d_e, d), BF16),
            pltpu.SemaphoreType.DMA((2,)),
        ],
    )
    src3 = src.reshape(n_tiles, 1, tile)
    return pl.pallas_call(
        functools.partial(_moe_kernel, tile=tile, sub=MOE_SUB),
        grid_spec=grid_spec,
        out_shape=jax.ShapeDtypeStruct((n_tiles * tile * SUBLANES, LANES), F32),
        compiler_params=pltpu.CompilerParams(
            dimension_semantics=("arbitrary",),
            vmem_limit_bytes=VMEM_LIMIT_BYTES),
        name="moe",
    )(tile_expert, tile_subs, n_used, src3, src3, x1, w_gate, w_up, w_down)


def _combine_kernel(pos0_ref, pos1_ref, x1_ref, w_ref, g_ref, b_ref, y_hbm, op_ref, os_ref, ybuf, sem,
                    *, tile, alpha, prompt_tiles):
    i = pl.program_id(0)
    rows = TOP_K * tile
    slot = i % 2

    @pl.when(i == 0)
    def _():
        _start_row_gather(y_hbm, pos0_ref, ybuf.at[0], rows, sem.at[0])

    @pl.when(i + 1 < pl.num_programs(0))
    def _():
        _start_row_gather(y_hbm, pos1_ref, ybuf.at[1 - slot], rows, sem.at[1 - slot])

    _wait_row_gather(y_hbm, ybuf.at[slot], rows, sem.at[slot])
    w = w_ref[...]
    y_a = _load_row_tiles(ybuf.at[slot], tile, pitch=GATHER_PITCH)
    y_b = _load_row_tiles(ybuf.at[slot], tile, first=tile * GATHER_PITCH, pitch=GATHER_PITCH)
    moe = w[:, 0:1] * y_a + w[:, 1:2] * y_b
    x1 = _load_row_tiles(x1_ref, tile)
    out = _layer_norm(alpha * x1 + moe, g_ref[...], b_ref[...])

    @pl.when(i < prompt_tiles)
    def _():
        op_ref[...] = out

    @pl.when(i >= prompt_tiles)
    def _():
        os_ref[...] = out


def _combine(x1, y, pos, wts, g, b, alpha, n_prompt):
    n, d = x1.shape[0] // SUBLANES, g.shape[-1]
    tile = COMBINE_TILE
    n_tiles = n // tile
    prompt_tiles = n_prompt // tile
    assert n % tile == 0 and n_prompt % tile == 0 and 0 < prompt_tiles < n_tiles
    pos_t = pos.reshape(TOP_K, n_tiles, tile).transpose(1, 0, 2).reshape(n_tiles, 1, TOP_K * tile)
    return pl.pallas_call(
        functools.partial(_combine_kernel, tile=tile, alpha=alpha, prompt_tiles=prompt_tiles),
        grid=(n_tiles,),
        in_specs=[
            pl.BlockSpec((1, 1, TOP_K * tile), lambda i: (i, 0, 0), memory_space=pltpu.SMEM),
            pl.BlockSpec((1, 1, TOP_K * tile), lambda i: (jnp.minimum(i + 1, n_tiles - 1), 0, 0),
                         memory_space=pltpu.SMEM),
            pl.BlockSpec((tile * SUBLANES, LANES), lambda i: (i, 0)),
            pl.BlockSpec((tile, TOP_K), lambda i: (i, 0)),
            _full_spec(g),
            _full_spec(b),
            pl.BlockSpec(memory_space=pl.ANY),
        ],
        out_specs=(
            pl.BlockSpec((tile, d), lambda i: (jnp.minimum(i, prompt_tiles - 1), 0)),
            pl.BlockSpec((tile, d), lambda i: (jnp.maximum(i - prompt_tiles, 0), 0)),
        ),
        out_shape=(jax.ShapeDtypeStruct((n_prompt, d), F32),
                   jax.ShapeDtypeStruct((n - n_prompt, d), F32)),
        scratch_shapes=[pltpu.VMEM((2, TOP_K * tile * GATHER_PITCH, LANES), F32),
                        pltpu.SemaphoreType.DMA((2,))],
        compiler_params=pltpu.CompilerParams(
            dimension_semantics=("arbitrary",),
            vmem_limit_bytes=VMEM_LIMIT_BYTES),
        name="combine",
    )(pos_t, pos_t, x1, wts.T, g, b, y)


def _route_tables(eidx, n_experts, tile, sub):
    k, n = eidx.shape
    pairs = k * n
    n_tiles = pairs // tile + n_experts
    e = eidx.reshape(pairs)
    onehot = (e[:, None] == jnp.arange(n_experts, dtype=I32)[None, :]).astype(I32)
    csum = jnp.cumsum(onehot, axis=0)
    rank = jnp.sum(onehot * csum, axis=1) - 1
    counts = csum[-1]
    ptiles = (counts + tile - 1) // tile
    tile_end = jnp.cumsum(ptiles)
    offs = (tile_end - ptiles) * tile
    subs_per_tile = tile // sub
    first_subs = (counts + sub - 1) // sub - subs_per_tile * (ptiles - 1)
    head = first_subs * sub
    skip = tile - head
    local = rank + jnp.where(rank >= jnp.sum(onehot * head[None, :], axis=1),
                             jnp.sum(onehot * skip[None, :], axis=1), 0)
    pos = jnp.sum(onehot * offs[None, :], axis=1) + local
    n_used = tile_end[-1]
    j = jnp.arange(n_tiles, dtype=I32)
    te = jnp.sum((j[:, None] >= tile_end[None, :]).astype(I32), axis=1)
    te_last = jnp.sum((n_used - 1 >= tile_end).astype(I32))
    te = jnp.where(j < n_used, te, te_last).astype(I32)
    tile_subs = jnp.where(j == (tile_end - ptiles)[te], first_subs[te], subs_per_tile)
    tile_subs = jnp.where(j < n_used, tile_subs, 0).astype(I32)
    src = _invert_positions(pos, n, n_tiles * tile)
    return pos.reshape(k, n), src, te, tile_subs, n_used.reshape(1).astype(I32)


def _invert_kernel(pos_ref, src_ref, zeros_ref, *, chunk, n):
    i = pl.program_id(0)

    @pl.when(i == 0)
    def _():
        zeros_ref[...] = jnp.zeros(zeros_ref.shape, I32)
        pltpu.sync_copy(zeros_ref, src_ref)

    first_tok = (i * chunk) % n

    def body(c, carry):
        for u in range(INVERT_UNROLL):
            s = c * INVERT_UNROLL + u
            dst = pos_ref[0, 0, s]
            src_ref[dst] = first_tok + s
        return carry
    lax.fori_loop(0, chunk // INVERT_UNROLL, body, 0)


def _invert_positions(pos, n, rows):
    pairs = pos.shape[0]
    chunk = INVERT_CHUNK
    assert pairs % chunk == 0 and n % chunk == 0 and rows % LANES == 0
    src = pl.pallas_call(
        functools.partial(_invert_kernel, chunk=chunk, n=n),
        grid=(pairs // chunk,),
        in_specs=[pl.BlockSpec((1, 1, chunk), lambda i: (i, 0, 0), memory_space=pltpu.SMEM)],
        out_specs=pl.BlockSpec(memory_space=pltpu.SMEM),
        out_shape=jax.ShapeDtypeStruct((rows,), I32),
        scratch_shapes=[pltpu.VMEM((rows,), I32)],
        compiler_params=pltpu.CompilerParams(dimension_semantics=("arbitrary",)),
        name="invert_positions",
    )(pos.reshape(pairs // chunk, 1, chunk))
    return src


def _layer_params(l, w_in_b, b_in, conv_a_w, conv_a_b, ln_a_g, ln_a_b, w_pool_bd, pool_scale,
                  conv_c_w, w_out_b, b_out, ln1_g, ln1_b, w_router, b_router):
    row = lambda a: a[l][None, :]
    return (w_in_b[l], row(b_in), conv_a_w[l], row(conv_a_b), row(ln_a_g), row(ln_a_b),
            w_pool_bd[l], row(pool_scale), conv_c_w[l], w_out_b[l], row(b_out), row(ln1_g),
            row(ln1_b), w_router[l], b_router[l])


def kernel(x_prompt, x_sample, state_conv_a, state_pool_b, state_conv_c, w_in, b_in, conv_a_w, conv_a_b, ln_a_g, ln_a_b, w_pool, pool_scale, conv_c_w, w_out, b_out, ln1_g, ln1_b, w_router_group, b_router_group, w_router_expert, b_router_expert, w_gate, w_up, w_down, ln2_g, ln2_b):
    depth = w_in.shape[0]
    bp, tp, d = x_prompt.shape
    bs, ts, _ = x_sample.shape
    ka, w_a = conv_a_w.shape[1:]
    kb, w_b = state_pool_b.shape[2:]
    kc, w_c = conv_c_w.shape[1:]
    n_groups = w_router_group.shape[-1]
    n_experts = w_router_expert.shape[-1]
    dims = (w_a, w_b, w_c, ka, kb, kc, n_groups, n_experts)
    alpha = float((2 * depth) ** 0.25)
    assert tp % PROMPT_TILE == 0 and PROMPT_TILE % ROW_CHUNK == 0 and (bs * ts) % ROW_CHUNK == 0
    assert n_groups <= SUBLANES and ROUTER_EXPERT_ROW + n_experts <= LANES
    assert d == SUBLANES * LANES

    w_in_b = w_in.astype(BF16)
    w_out_b = w_out.astype(BF16)
    n_pg, pg = w_pool.shape[1], w_pool.shape[2]
    eye = jnp.eye(n_pg, dtype=F32)
    w_pool_bd = (w_pool[:, :, :, None, :] * eye[None, :, None, :, None]).reshape(depth, n_pg * pg, n_pg * pg).astype(BF16)
    def router_lanes(group_part, expert_part):
        lead = group_part.shape[:-1]
        gap = jnp.zeros(lead + (ROUTER_EXPERT_ROW - ROUTER_GROUP_ROW - n_groups,), F32)
        tail = jnp.zeros(lead + (LANES - ROUTER_EXPERT_ROW - n_experts,), F32)
        return jnp.concatenate([group_part, gap, expert_part, tail], axis=-1)

    assert ROUTER_GROUP_ROW == 0
    wr = router_lanes(w_router_group, w_router_expert)
    wr_hi = wr.astype(BF16)
    wr_lo = (wr - wr_hi.astype(F32)).astype(BF16)
    w_router = jnp.concatenate([wr_hi, wr_lo], axis=-1)
    b_router = router_lanes(b_router_group, b_router_expert)[:, None, :]

    xp = x_prompt.reshape(bp * tp, d)
    xs = x_sample.transpose(1, 0, 2).reshape(ts * bs, d)
    new_states = [[] for _ in range(6)]
    for l in range(depth):
        params = _layer_params(l, w_in_b, b_in, conv_a_w, conv_a_b, ln_a_g, ln_a_b, w_pool_bd,
                               pool_scale, conv_c_w, w_out_b, b_out, ln1_g, ln1_b, w_router, b_router)
        g2, b2 = ln2_g[l][None, :], ln2_b[l][None, :]

        x1, eidx, wts, pa, pb, pc = _prompt_mixer(xp, bp, tp, bs * ts, params, dims, alpha)
        x1, seidx, swts, sa, sb, sc = _sample_mixer(
            x1, xs, state_conv_a[l].reshape(bs, -1), state_pool_b[l].reshape(bs, -1),
            state_conv_c[l].reshape(bs, -1), params, dims, alpha, bs, ts)
        eidx = jnp.concatenate([eidx.transpose(1, 0, 2).reshape(TOP_K, bp * tp), seidx], axis=1)
        wts = jnp.concatenate([wts.transpose(1, 0, 2).reshape(TOP_K, bp * tp), swts], axis=1)

        pos, src, te, subs, nu = _route_tables(eidx, n_experts, MOE_TILE, MOE_SUB)
        y = _moe(x1, src, te, subs, nu, w_gate, w_up, w_down, l)
        xp, xs = _combine(x1, y, pos, wts, g2, b2, alpha, bp * tp)

        for lst, val in zip(new_states, (pa, sa.reshape(bs, ka - 1, w_a), pb, sb.reshape(bs, kb, w_b),
                                         pc, sc.reshape(bs, kc - 1, w_c))):
            lst.append(val)

    y_prompt = xp.reshape(bp, tp, d)
    y_sample = xs.reshape(ts, bs, d).transpose(1, 0, 2)
    return (y_prompt, y_sample) + tuple(jnp.stack(s) for s in new_states)
```

```python
import functools

import jax
import jax.numpy as jnp
from jax import lax
from jax.experimental import pallas as pl
from jax.experimental.pallas import tpu as pltpu

F32 = jnp.float32
BF16 = jnp.bfloat16
I32 = jnp.int32

POOL_WINDOWS = (2, 4, 8, 16)
TOP_K = 2
LN_EPS = 1e-5
PAST_LEN = 16384

LANES = 128
SUBLANES = 8
SUBLANE_SHIFT = 3
LANE_SHIFT = 7
VMEM_LIMIT_BYTES = 56 * 1024 * 1024

ROW_CHUNK = 32
PROMPT_TILE = 512
PROMPT_HIST_PITCH = 2
MOE_TILE = 512
MOE_SUB = 256
COMBINE_TILE = 256
GATHER_UNROLL = 16
DMA_PRIORITIES = 2
INVERT_CHUNK = 4224
INVERT_UNROLL = 16
GATHER_PITCH = 9
ROUTER_GROUP_ROW = 0
ROUTER_EXPERT_ROW = 8


def _dot(a, b):
    return jnp.dot(a, b, preferred_element_type=F32)


def _layer_norm(x, g, b):
    mu = jnp.mean(x, axis=-1, keepdims=True)
    xc = x - mu
    var = jnp.mean(xc * xc, axis=-1, keepdims=True)
    return xc * lax.rsqrt(var + LN_EPS) * g + b


def _sigmoid(x):
    return 1.0 / (1.0 + jnp.exp(-x))


def _store_row_tiles(ref, val):
    rows, d = val.shape
    for j in range(d // LANES):
        ref[pl.ds(j, rows, stride=SUBLANES), :] = val[:, j * LANES:(j + 1) * LANES]


def _load_row_tiles(ref, rows, first=0, pitch=SUBLANES):
    return jnp.concatenate(
        [ref[pl.ds(first + j, rows, stride=pitch), :] for j in range(SUBLANES)], axis=-1)


def _hist_load(buf, start, n, pitch):
    parts = []
    for j in range(buf.shape[0]):
        if pitch == 1:
            parts.append(buf[j, pl.ds(start, n), :])
        else:
            parts.append(buf.at[j][pl.ds(pitch * start, n, stride=pitch), :])
    return parts[0] if len(parts) == 1 else jnp.concatenate(parts, axis=-1)


def _hist_store(buf, start, n, pitch, val):
    for j in range(buf.shape[0]):
        piece = val[:, j * LANES:(j + 1) * LANES]
        if pitch == 1:
            buf[j, pl.ds(start, n), :] = piece
        else:
            buf.at[j][pl.ds(pitch * start, n, stride=pitch), :] = piece


def _mixer_rows(x_ref, p, bufs, outs, *, rows, stride, pitch, new_a, new_b, new_c, pos_of_row,
                alpha, n_groups, n_experts):
    (w_in, b_in, wa, ba, lnag, lnab, wbd, pscale, wc, w_out, b_out, ln1g, ln1b,
     wr, br) = p
    proj, buf_a, buf_b, buf_c, dbuf, cat = bufs
    x1_ref, eidx_ref, wts_ref = outs
    ka = wa.shape[0]
    kc = wc.shape[0]
    w_a = wa.shape[1]
    w_b = pscale.shape[1]
    w_c = wc.shape[1]
    o_gate, o_ub, o_cbg, o_ccg, o_ch = w_a, 2 * w_a, 2 * w_a + w_b, 2 * w_a + w_b + w_c, 2 * w_a + w_b + 2 * w_c

    x = x_ref[...]
    proj[...] = _dot(x.astype(BF16), w_in[...]) + b_in[...]

    _hist_store(buf_a, new_a, rows, pitch, proj[:, 0:w_a] * _sigmoid(proj[:, o_gate:o_gate + w_a]))
    _hist_store(buf_b, new_b, rows, pitch, proj[:, o_ub:o_ub + w_b])
    _hist_store(buf_c, new_c, rows, pitch, proj[:, o_ccg:o_ccg + w_c] * proj[:, o_ch:o_ch + w_c])

    lane_b = lax.broadcasted_iota(I32, (ROW_CHUNK, w_b), 1)
    group_b = lane_b // (w_b // len(POOL_WINDOWS))
    win_b = jnp.zeros((ROW_CHUNK, w_b), I32)
    for g, w in enumerate(POOL_WINDOWS):
        win_b = jnp.where(group_b == g, w, win_b)

    def chunk(c, carry):
        r0 = c * ROW_CHUNK if isinstance(c, int) else pl.multiple_of(c * ROW_CHUNK, ROW_CHUNK)
        acc = jnp.zeros((ROW_CHUNK, w_a), F32) + ba[...]
        for k in range(ka):
            src = _hist_load(buf_a, r0 + (new_a - (ka - 1 - k) * stride), ROW_CHUNK, pitch)
            acc = acc + wa[k:k + 1, :] * src
        a = _layer_norm(acc, lnag[...], lnab[...])
        cat[pl.ds(r0, ROW_CHUNK), 0:w_a] = (a * _sigmoid(a)).astype(BF16)
        cur = _hist_load(buf_b, r0 + new_b, ROW_CHUNK, pitch)
        run = cur
        wsum = jnp.zeros((ROW_CHUNK, w_b), F32)
        for j in range(1, max(POOL_WINDOWS) + 1):
            if j in POOL_WINDOWS:
                wsum = jnp.where(win_b == j, run, wsum)
            if j < max(POOL_WINDOWS):
                run = run + _hist_load(buf_b, r0 + (new_b - j * stride), ROW_CHUNK, pitch)
        pos = pos_of_row(r0 + lax.broadcasted_iota(I32, (ROW_CHUNK, w_b), 0))
        cnt = jnp.minimum(pos + 1, win_b).astype(F32)
        dbuf[pl.ds(r0, ROW_CHUNK), :] = (wsum / cnt - cur).astype(BF16)
        accc = jnp.zeros((ROW_CHUNK, w_c), F32)
        for k in range(kc):
            src = _hist_load(buf_c, r0 + (new_c - (kc - 1 - k) * stride), ROW_CHUNK, pitch)
            accc = accc + wc[k:k + 1, :] * src
        cbg = proj[pl.ds(r0, ROW_CHUNK), o_cbg:o_cbg + w_c]
        cat[pl.ds(r0, ROW_CHUNK), w_a + w_b:w_a + w_b + w_c] = (cbg * accc).astype(BF16)
        return carry

    if pitch == 1:
        assert stride % SUBLANES == 0
        lax.fori_loop(0, rows // ROW_CHUNK, chunk, 0)
    else:
        for c in range(rows // ROW_CHUNK):
            chunk(c, 0)

    cat[:, w_a:w_a + w_b] = (_dot(dbuf[...], wbd[...]) * pscale[...]).astype(BF16)

    m = _dot(cat[...], w_out[...]) + b_out[...]
    x1 = _layer_norm(alpha * x + m, ln1g[...], ln1b[...])
    _store_row_tiles(x1_ref, x1)

    x1_hi = x1.astype(BF16)
    x1_lo = (x1 - x1_hi.astype(F32)).astype(BF16)
    p_hi = _dot(x1_hi, wr[...])
    p_lo = _dot(x1_lo, wr[...])
    logits = (p_hi[:, 0:LANES] + p_hi[:, LANES:2 * LANES]) + (p_lo[:, 0:LANES] + p_lo[:, LANES:2 * LANES])
    lt = jnp.transpose(logits + br[...])

    neg = jnp.float32(-jnp.inf)
    gl = lt[ROUTER_GROUP_ROW:ROUTER_GROUP_ROW + SUBLANES, :]
    grow = lax.broadcasted_iota(I32, gl.shape, 0)
    gvalid = grow < n_groups
    glm = jnp.where(gvalid, gl, neg)
    gmax = jnp.max(glm, axis=0, keepdims=True)
    gidx = jnp.min(jnp.where(glm == gmax, grow, SUBLANES), axis=0, keepdims=True)
    gsum = jnp.sum(jnp.where(gvalid, jnp.exp(gl - gmax), 0.0), axis=0, keepdims=True)
    g_p = 1.0 / gsum

    el = lt[ROUTER_EXPERT_ROW:ROUTER_EXPERT_ROW + n_experts, :]
    erow = lax.broadcasted_iota(I32, el.shape, 0)
    v = jnp.where(erow // (n_experts // n_groups) == gidx, el, neg)
    v1 = jnp.max(v, axis=0, keepdims=True)
    i1 = jnp.min(jnp.where(v == v1, erow, n_experts), axis=0, keepdims=True)
    vv = jnp.where(erow == i1, neg, v)
    v2 = jnp.max(vv, axis=0, keepdims=True)
    i2 = jnp.min(jnp.where(vv == v2, erow, n_experts), axis=0, keepdims=True)
    e2 = jnp.exp(v2 - v1)
    den = 1.0 + e2
    eidx_ref[0:1, :] = i1
    eidx_ref[1:2, :] = i2
    wts_ref[0:1, :] = (1.0 / den) * g_p
    wts_ref[1:2, :] = (e2 / den) * g_p


def _prompt_mixer_kernel(x_ref, *refs, n_params, n_seq, extra_tiles, tile, hist, pitch, alpha,
                         n_groups, n_experts):
    p = refs[:n_params]
    x1_ref, eidx_ref, wts_ref, sa_ref, sb_ref, sc_ref = refs[n_params:n_params + 6]
    bufs = refs[n_params + 6:]
    _, buf_a, buf_b, buf_c, _, _ = bufs
    ha, hb, hc = hist
    s = pl.program_id(0)
    t = pl.program_id(1)

    @pl.when(s < n_seq)
    def _():
        hists = ((buf_a, ha), (buf_b, hb), (buf_c, hc))

        @pl.when(t == 0)
        def _():
            for buf, h in hists:
                _hist_store(buf, 0, h, pitch, jnp.zeros((h, buf.shape[0] * LANES), F32))

        _mixer_rows(x_ref, p, bufs, (x1_ref, eidx_ref, wts_ref), rows=tile, stride=1, pitch=pitch,
                    new_a=ha, new_b=hb, new_c=hc,
                    pos_of_row=lambda r: r + t * tile,
                    alpha=alpha, n_groups=n_groups, n_experts=n_experts)

        tails = [_hist_load(buf, tile, h, pitch) for buf, h in hists]
        for (buf, h), tail in zip(hists, tails):
            _hist_store(buf, 0, h, pitch, tail)

        @pl.when(t == pl.num_programs(1) - 1)
        def _():
            for ref, (_, h), tail in zip((sa_ref, sb_ref, sc_ref), hists, tails):
                ref[...] = tail[h - ref.shape[0]:h, :]

    @pl.when(jnp.logical_and(s == n_seq, t < extra_tiles))
    def _():
        x1_ref[...] = jnp.zeros(x1_ref.shape, F32)


def _sample_mixer_kernel(x1_all_hbm, x_ref, sta_ref, stb_ref, stc_ref, *refs, n_params, batch, steps,
                         alpha, n_groups, n_experts):
    del x1_all_hbm
    p = refs[:n_params]
    x1_ref, eidx_ref, wts_ref, sa_ref, sb_ref, sc_ref = refs[n_params:n_params + 6]
    bufs = refs[n_params + 6:]
    _, buf_a, buf_b, buf_c, _, _ = bufs
    rows = batch * steps
    hists = []
    for buf, st_ref, out_ref in ((buf_a, sta_ref, sa_ref), (buf_b, stb_ref, sb_ref),
                                 (buf_c, stc_ref, sc_ref)):
        width = buf.shape[0] * LANES
        hists.append((buf, st_ref, out_ref, width, st_ref.shape[1] // width))

    for buf, st_ref, _, width, n in hists:
        for j in range(n):
            _hist_store(buf, j * batch, batch, 1, st_ref[:, j * width:(j + 1) * width])

    na, nb, nc = (h[4] for h in hists)
    _mixer_rows(x_ref, p, bufs, (x1_ref, eidx_ref, wts_ref), rows=rows, stride=batch, pitch=1,
                new_a=na * batch, new_b=nb * batch, new_c=nc * batch,
                pos_of_row=lambda r: PAST_LEN + r // batch,
                alpha=alpha, n_groups=n_groups, n_experts=n_experts)

    for buf, _, out_ref, width, n in hists:
        for j in range(n):
            out_ref[:, j * width:(j + 1) * width] = _hist_load(buf, (j + steps) * batch, batch, 1)


def _full_spec(a):
    nd = a.ndim
    return pl.BlockSpec(a.shape, lambda *_: (0,) * nd)


def _mixer_scratch(rows, ha, hb, hc, d_in, w_a, w_b, w_c, pitch):
    return [
        pltpu.VMEM((rows, d_in), F32),
        pltpu.VMEM((w_a // LANES, pitch * (ha + rows), LANES), F32),
        pltpu.VMEM((w_b // LANES, pitch * (hb + rows), LANES), F32),
        pltpu.VMEM((w_c // LANES, pitch * (hc + rows), LANES), F32),
        pltpu.VMEM((rows, w_b), BF16),
        pltpu.VMEM((rows, w_a + w_b + w_c), BF16),
    ]


def _round_up(n, m):
    return (n + m - 1) // m * m


def _prompt_mixer(x, b, t, extra_rows, params, dims, alpha):
    d = x.shape[1]
    w_a, w_b, w_c, ka, kb, kc, n_groups, n_experts = dims
    tile = PROMPT_TILE
    nt = t // tile
    ha, hb, hc = _round_up(ka - 1, SUBLANES), _round_up(kb, SUBLANES), _round_up(kc - 1, SUBLANES)
    d_in = params[0].shape[1]
    extra_tiles = extra_rows // tile
    assert extra_rows % tile == 0 and extra_tiles <= nt
    kern = functools.partial(_prompt_mixer_kernel, n_params=len(params), n_seq=b,
                             extra_tiles=extra_tiles, tile=tile, hist=(ha, hb, hc),
                             pitch=PROMPT_HIST_PITCH, alpha=alpha,
                             n_groups=n_groups, n_experts=n_experts)
    last_block = b * nt + extra_tiles - 1

    def seq_tile(i, j):
        return jnp.minimum(i, b - 1), jnp.where(i < b, j, nt - 1)

    def row_block(i, j):
        s, t_ = seq_tile(i, j)
        return s * nt + t_

    out_shape = (
        jax.ShapeDtypeStruct(((b * t + extra_rows) * SUBLANES, LANES), F32),
        jax.ShapeDtypeStruct((b, TOP_K, t), I32),
        jax.ShapeDtypeStruct((b, TOP_K, t), F32),
        jax.ShapeDtypeStruct((b, ka - 1, w_a), F32),
        jax.ShapeDtypeStruct((b, kb, w_b), F32),
        jax.ShapeDtypeStruct((b, kc - 1, w_c), F32),
    )
    out_specs = (
        pl.BlockSpec((tile * SUBLANES, LANES), lambda i, j: (jnp.minimum(i * nt + j, last_block), 0)),
        pl.BlockSpec((None, TOP_K, tile), lambda i, j: (seq_tile(i, j)[0], 0, seq_tile(i, j)[1])),
        pl.BlockSpec((None, TOP_K, tile), lambda i, j: (seq_tile(i, j)[0], 0, seq_tile(i, j)[1])),
        pl.BlockSpec((None, ka - 1, w_a), lambda i, j: (jnp.minimum(i, b - 1), 0, 0)),
        pl.BlockSpec((None, kb, w_b), lambda i, j: (jnp.minimum(i, b - 1), 0, 0)),
        pl.BlockSpec((None, kc - 1, w_c), lambda i, j: (jnp.minimum(i, b - 1), 0, 0)),
    )
    in_specs = [pl.BlockSpec((tile, d), lambda i, j: (row_block(i, j), 0))] + [_full_spec(a) for a in params]
    return pl.pallas_call(
        kern,
        grid=(b + 1, nt),
        in_specs=in_specs,
        out_specs=out_specs,
        out_shape=out_shape,
        scratch_shapes=_mixer_scratch(tile, ha, hb, hc, d_in, w_a, w_b, w_c, PROMPT_HIST_PITCH),
        compiler_params=pltpu.CompilerParams(
            dimension_semantics=("arbitrary", "arbitrary"),
            vmem_limit_bytes=VMEM_LIMIT_BYTES),
        name="prompt_mixer",
    )(x, *params)


def _sample_mixer(x1_all, x_tm, st_a, st_b, st_c, params, dims, alpha, batch, steps):
    rows, d = x_tm.shape
    w_a, w_b, w_c, ka, kb, kc, n_groups, n_experts = dims
    d_in = params[0].shape[1]
    first = x1_all.shape[0] // SUBLANES - rows
    assert first % rows == 0
    kern = functools.partial(_sample_mixer_kernel, n_params=len(params), batch=batch,
                             steps=steps, alpha=alpha, n_groups=n_groups, n_experts=n_experts)
    out_shape = (
        jax.ShapeDtypeStruct(x1_all.shape, F32),
        jax.ShapeDtypeStruct((TOP_K, rows), I32),
        jax.ShapeDtypeStruct((TOP_K, rows), F32),
        jax.ShapeDtypeStruct(st_a.shape, F32),
        jax.ShapeDtypeStruct(st_b.shape, F32),
        jax.ShapeDtypeStruct(st_c.shape, F32),
    )
    args = (x_tm, st_a, st_b, st_c) + tuple(params)
    out_specs = (pl.BlockSpec((rows * SUBLANES, LANES), lambda i: (first // rows, 0)),) + tuple(
        pl.BlockSpec(s.shape, lambda i, n=len(s.shape): (0,) * n) for s in out_shape[1:])
    return pl.pallas_call(
        kern,
        grid=(1,),
        in_specs=[pl.BlockSpec(memory_space=pl.ANY)] + [_full_spec(a) for a in args],
        out_specs=out_specs,
        out_shape=out_shape,
        input_output_aliases={0: 0},
        scratch_shapes=_mixer_scratch(rows, (ka - 1) * batch, kb * batch, (kc - 1) * batch,
                                      d_in, w_a, w_b, w_c, 1),
        compiler_params=pltpu.CompilerParams(
            dimension_semantics=("arbitrary",),
            vmem_limit_bytes=VMEM_LIMIT_BYTES),
        name="sample_mixer",
    )(x1_all, *args)


def _start_row_gather(src_hbm, idx_ref, dst_vmem, n, sem):
    def body(c, carry):
        for u in range(GATHER_UNROLL):
            s = c * GATHER_UNROLL + u
            row = idx_ref[0, 0, s]
            pltpu.make_async_copy(
                src_hbm.at[pl.ds(pl.multiple_of(row * SUBLANES, SUBLANES), SUBLANES)],
                dst_vmem.at[pl.ds(s * GATHER_PITCH, SUBLANES)],
                sem).start(priority=u % DMA_PRIORITIES)
        return carry
    lax.fori_loop(0, n // GATHER_UNROLL, body, 0)


def _wait_row_gather(src_hbm, dst_vmem, n, sem):
    pltpu.make_async_copy(src_hbm.at[pl.ds(0, n * SUBLANES)], dst_vmem.at[pl.ds(0, n * SUBLANES)],
                          sem).wait()


def _moe_kernel(ta_ref, tb_ref, cnt_ref, nu_ref, src0_ref, src1_ref, x_hbm, ga_ref, gb_ref,
                wga_ref, wua_ref, wda_ref, wgb_ref, wub_ref, wdb_ref, y_ref,
                xbuf, wga_h, wua_h, wda_h, wgb_h, wub_h, wdb_h, sem, *, tile, sub):
    i = pl.program_id(0)
    n_used = nu_ref[0]
    slot = i % 2
    nxt = jnp.minimum(i + 1, pl.num_programs(0) - 1)
    prev = jnp.maximum(i - 1, 0)

    @pl.when(i == 0)
    def _():
        _start_row_gather(x_hbm, src0_ref, xbuf.at[0], cnt_ref[0] * sub, sem.at[0])

    @pl.when(i + 1 < n_used)
    def _():
        _start_row_gather(x_hbm, src1_ref, xbuf.at[1 - slot], cnt_ref[nxt] * sub, sem.at[1 - slot])

    @pl.when(i < n_used)
    def _():
        for t_ref, pairs in ((ta_ref, ((wga_ref, wga_h), (wua_ref, wua_h), (wda_ref, wda_h))),
                             (tb_ref, ((wgb_ref, wgb_h), (wub_ref, wub_h), (wdb_ref, wdb_h)))):
            @pl.when(jnp.logical_or(i == 0, t_ref[i] != t_ref[prev]))
            def _(pairs=pairs):
                for w_ref, w_half in pairs:
                    w_half[...] = w_ref[...].astype(BF16)

        _wait_row_gather(x_hbm, xbuf.at[slot], cnt_ref[i] * sub, sem.at[slot])

        for c in range(1, tile // sub + 1):
            @pl.when(cnt_ref[i] == c)
            def _(rows=c * sub):
                xb = _load_row_tiles(xbuf.at[slot], rows, pitch=GATHER_PITCH).astype(BF16)

                def expert(wg_h, wu_h, wd_h):
                    hg = _dot(xb, wg_h[...])
                    hu = _dot(xb, wu_h[...])
                    return _dot((hg * _sigmoid(hg) * hu).astype(BF16), wd_h[...])

                y = ga_ref[0:rows, :] * expert(wga_h, wua_h, wda_h)
                y = y + gb_ref[0:rows, :] * expert(wgb_h, wub_h, wdb_h)
                _store_row_tiles(y_ref, y)
                if rows < tile:
                    y_ref[pl.ds(rows * SUBLANES, (tile - rows) * SUBLANES), :] = jnp.zeros(
                        ((tile - rows) * SUBLANES, LANES), F32)

    @pl.when(i >= n_used)
    def _():
        y_ref[...] = jnp.zeros(y_ref.shape, F32)


def _moe(x1, tables, w_gate, w_up, w_down, layer):
    src, gate_a, gate_b, tile_a, tile_b, tile_subs, n_used = tables
    d = w_gate.shape[-2]
    tile = MOE_TILE
    n_tiles = src.shape[0] // tile
    d_e = w_gate.shape[-1]

    def w_spec(shape, which):
        return pl.BlockSpec((None, None) + shape,
                            lambda i, ta, tb, cnt, nu: (layer, (ta, tb)[which][i], 0, 0))

    grid_spec = pltpu.PrefetchScalarGridSpec(
        num_scalar_prefetch=4,
        grid=(n_tiles,),
        in_specs=[
            pl.BlockSpec((1, 1, tile), lambda i, *_: (i, 0, 0), memory_space=pltpu.SMEM),
            pl.BlockSpec((1, 1, tile), lambda i, *_: (jnp.minimum(i + 1, n_tiles - 1), 0, 0),
                         memory_space=pltpu.SMEM),
            pl.BlockSpec(memory_space=pl.ANY),
            pl.BlockSpec((tile, 1), lambda i, *_: (i, 0)),
            pl.BlockSpec((tile, 1), lambda i, *_: (i, 0)),
            w_spec((d, d_e), 0), w_spec((d, d_e), 0), w_spec((d_e, d), 0),
            w_spec((d, d_e), 1), w_spec((d, d_e), 1), w_spec((d_e, d), 1),
        ],
        out_specs=pl.BlockSpec((tile * SUBLANES, LANES), lambda i, *_: (i, 0)),
        scratch_shapes=[
            pltpu.VMEM((2, tile * GATHER_PITCH, LANES), F32),
            pltpu.VMEM((d, d_e), BF16), pltpu.VMEM((d, d_e), BF16), pltpu.VMEM((d_e, d), BF16),
            pltpu.VMEM((d, d_e), BF16), pltpu.VMEM((d, d_e), BF16), pltpu.VMEM((d_e, d), BF16),
            pltpu.SemaphoreType.DMA((2,)),
        ],
    )
    src3 = src.reshape(n_tiles, 1, tile)
    return pl.pallas_call(
        functools.partial(_moe_kernel, tile=tile, sub=MOE_SUB),
        grid_spec=grid_spec,
        out_shape=jax.ShapeDtypeStruct((n_tiles * tile * SUBLANES, LANES), F32),
        compiler_params=pltpu.CompilerParams(
            dimension_semantics=("arbitrary",),
            vmem_limit_bytes=VMEM_LIMIT_BYTES),
        name="moe",
    )(tile_a, tile_b, tile_subs, n_used, src3, src3, x1, gate_a[:, None], gate_b[:, None],
      w_gate, w_up, w_down, w_gate, w_up, w_down)


def _combine_kernel(pos0_ref, pos1_ref, x1_ref, g_ref, b_ref, y_hbm, op_ref, os_ref, ybuf, sem,
                    *, tile, alpha, prompt_tiles):
    i = pl.program_id(0)
    slot = i % 2

    @pl.when(i == 0)
    def _():
        _start_row_gather(y_hbm, pos0_ref, ybuf.at[0], tile, sem.at[0])

    @pl.when(i + 1 < pl.num_programs(0))
    def _():
        _start_row_gather(y_hbm, pos1_ref, ybuf.at[1 - slot], tile, sem.at[1 - slot])

    _wait_row_gather(y_hbm, ybuf.at[slot], tile, sem.at[slot])
    moe = _load_row_tiles(ybuf.at[slot], tile, pitch=GATHER_PITCH)
    x1 = _load_row_tiles(x1_ref, tile)
    out = _layer_norm(alpha * x1 + moe, g_ref[...], b_ref[...])

    @pl.when(i < prompt_tiles)
    def _():
        op_ref[...] = out

    @pl.when(i >= prompt_tiles)
    def _():
        os_ref[...] = out


def _combine(x1, y, pos, g, b, alpha, n_prompt):
    n, d = x1.shape[0] // SUBLANES, g.shape[-1]
    tile = COMBINE_TILE
    n_tiles = n // tile
    prompt_tiles = n_prompt // tile
    assert n % tile == 0 and n_prompt % tile == 0 and 0 < prompt_tiles < n_tiles
    pos_t = pos.reshape(n_tiles, 1, tile)
    return pl.pallas_call(
        functools.partial(_combine_kernel, tile=tile, alpha=alpha, prompt_tiles=prompt_tiles),
        grid=(n_tiles,),
        in_specs=[
            pl.BlockSpec((1, 1, tile), lambda i: (i, 0, 0), memory_space=pltpu.SMEM),
            pl.BlockSpec((1, 1, tile), lambda i: (jnp.minimum(i + 1, n_tiles - 1), 0, 0),
                         memory_space=pltpu.SMEM),
            pl.BlockSpec((tile * SUBLANES, LANES), lambda i: (i, 0)),
            _full_spec(g),
            _full_spec(b),
            pl.BlockSpec(memory_space=pl.ANY),
        ],
        out_specs=(
            pl.BlockSpec((tile, d), lambda i: (jnp.minimum(i, prompt_tiles - 1), 0)),
            pl.BlockSpec((tile, d), lambda i: (jnp.maximum(i - prompt_tiles, 0), 0)),
        ),
        out_shape=(jax.ShapeDtypeStruct((n_prompt, d), F32),
                   jax.ShapeDtypeStruct((n - n_prompt, d), F32)),
        scratch_shapes=[pltpu.VMEM((2, tile * GATHER_PITCH, LANES), F32),
                        pltpu.SemaphoreType.DMA((2,))],
        compiler_params=pltpu.CompilerParams(
            dimension_semantics=("arbitrary",),
            vmem_limit_bytes=VMEM_LIMIT_BYTES),
        name="combine",
    )(pos_t, pos_t, x1, g, b, y)


def _pair_order(experts_per_group):
    order = []
    for a in range(experts_per_group):
        bs = list(range(a + 1, experts_per_group))
        order += [(a, b) for b in (bs if a % 2 == 0 else bs[::-1])]
    return order


def _route_tables(eidx, wts, n_groups, n_experts, tile, sub):
    assert eidx.shape[0] == TOP_K == 2
    n = eidx.shape[1]
    epg = n_experts // n_groups
    order = _pair_order(epg)
    first_is_lo = eidx[0] < eidx[1]
    lo, hi = jnp.minimum(eidx[0], eidx[1]), jnp.maximum(eidx[0], eidx[1])
    w_lo = jnp.where(first_is_lo, wts[0], wts[1])
    w_hi = jnp.where(first_is_lo, wts[1], wts[0])
    group = lo // epg
    a, b = lo - group * epg, hi - group * epg
    pair = sum(jnp.where((a == pa) & (b == pb), q, 0) for q, (pa, pb) in enumerate(order))
    bucket = (group * len(order) + pair).astype(I32)
    n_buckets = n_groups * len(order)

    n_tiles = n // tile + n_buckets
    onehot = (bucket[:, None] == jnp.arange(n_buckets, dtype=I32)[None, :]).astype(I32)
    csum = jnp.cumsum(onehot, axis=0)
    rank = jnp.sum(onehot * csum, axis=1) - 1
    counts = csum[-1]
    ptiles = (counts + tile - 1) // tile
    tile_end = jnp.cumsum(ptiles)
    offs = (tile_end - ptiles) * tile
    subs_per_tile = tile // sub
    first_subs = (counts + sub - 1) // sub - subs_per_tile * (ptiles - 1)
    head = first_subs * sub
    skip = tile - head
    local = rank + jnp.where(rank >= jnp.sum(onehot * head[None, :], axis=1),
                             jnp.sum(onehot * skip[None, :], axis=1), 0)
    pos = jnp.sum(onehot * offs[None, :], axis=1) + local
    n_used = tile_end[-1]
    j = jnp.arange(n_tiles, dtype=I32)
    tb = jnp.sum((j[:, None] >= tile_end[None, :]).astype(I32), axis=1)
    tb_last = jnp.sum((n_used - 1 >= tile_end).astype(I32))
    tb = jnp.where(j < n_used, tb, tb_last).astype(I32)
    tile_subs = jnp.where(j == (tile_end - ptiles)[tb], first_subs[tb], subs_per_tile)
    tile_subs = jnp.where(j < n_used, tile_subs, 0).astype(I32)
    pair_a = jnp.array([p[0] for p in order], I32)
    pair_b = jnp.array([p[1] for p in order], I32)
    tile_group = (tb // len(order)) * epg
    tile_a = (tile_group + pair_a[tb % len(order)]).astype(I32)
    tile_b = (tile_group + pair_b[tb % len(order)]).astype(I32)

    rows = n_tiles * tile
    src = _scatter_table(pos, None, rows, I32)
    gate_a = _scatter_table(pos, w_lo, rows, F32)
    gate_b = _scatter_table(pos, w_hi, rows, F32)
    return pos, (src, gate_a, gate_b, tile_a, tile_b, tile_subs, n_used.reshape(1).astype(I32))


def _scatter_kernel(*refs, chunk, has_values):
    if has_values:
        pos_ref, val_ref, out_ref, zeros_ref = refs
    else:
        pos_ref, out_ref, zeros_ref = refs
    i = pl.program_id(0)

    @pl.when(i == 0)
    def _():
        zeros_ref[...] = jnp.zeros(zeros_ref.shape, zeros_ref.dtype)
        pltpu.sync_copy(zeros_ref, out_ref)

    first = i * chunk

    def body(c, carry):
        for u in range(INVERT_UNROLL):
            s = c * INVERT_UNROLL + u
            out_ref[pos_ref[0, 0, s]] = val_ref[0, 0, s] if has_values else first + s
        return carry
    lax.fori_loop(0, chunk // INVERT_UNROLL, body, 0)


def _scatter_table(pos, values, rows, dtype):
    n = pos.shape[0]
    chunk = INVERT_CHUNK
    assert n % chunk == 0 and chunk % INVERT_UNROLL == 0
    blocked = lambda a: a.reshape(n // chunk, 1, chunk)
    smem_block = pl.BlockSpec((1, 1, chunk), lambda i: (i, 0, 0), memory_space=pltpu.SMEM)
    args = (blocked(pos),) if values is None else (blocked(pos), blocked(values))
    return pl.pallas_call(
        functools.partial(_scatter_kernel, chunk=chunk, has_values=values is not None),
        grid=(n // chunk,),
        in_specs=[smem_block] * len(args),
        out_specs=pl.BlockSpec(memory_space=pltpu.SMEM),
        out_shape=jax.ShapeDtypeStruct((rows,), dtype),
        scratch_shapes=[pltpu.VMEM((rows,), dtype)],
        compiler_params=pltpu.CompilerParams(dimension_semantics=("arbitrary",)),
        name="scatter_table",
    )(*args)


def _layer_params(l, w_in_b, b_in, conv_a_w, conv_a_b, ln_a_g, ln_a_b, w_pool_bd, pool_scale,
                  conv_c_w, w_out_b, b_out, ln1_g, ln1_b, w_router, b_router):
    row = lambda a: a[l][None, :]
    return (w_in_b[l], row(b_in), conv_a_w[l], row(conv_a_b), row(ln_a_g), row(ln_a_b),
            w_pool_bd[l], row(pool_scale), conv_c_w[l], w_out_b[l], row(b_out), row(ln1_g),
            row(ln1_b), w_router[l], b_router[l])


def kernel(x_prompt, x_sample, state_conv_a, state_pool_b, state_conv_c, w_in, b_in, conv_a_w, conv_a_b, ln_a_g, ln_a_b, w_pool, pool_scale, conv_c_w, w_out, b_out, ln1_g, ln1_b, w_router_group, b_router_group, w_router_expert, b_router_expert, w_gate, w_up, w_down, ln2_g, ln2_b):
    depth = w_in.shape[0]
    bp, tp, d = x_prompt.shape
    bs, ts, _ = x_sample.shape
    ka, w_a = conv_a_w.shape[1:]
    kb, w_b = state_pool_b.shape[2:]
    kc, w_c = conv_c_w.shape[1:]
    n_groups = w_router_group.shape[-1]
    n_experts = w_router_expert.shape[-1]
    dims = (w_a, w_b, w_c, ka, kb, kc, n_groups, n_experts)
    alpha = float((2 * depth) ** 0.25)
    assert tp % PROMPT_TILE == 0 and PROMPT_TILE % ROW_CHUNK == 0 and (bs * ts) % ROW_CHUNK == 0
    assert n_groups <= SUBLANES and ROUTER_EXPERT_ROW + n_experts <= LANES
    assert d == SUBLANES * LANES

    w_in_b = w_in.astype(BF16)
    w_out_b = w_out.astype(BF16)
    n_pg, pg = w_pool.shape[1], w_pool.shape[2]
    eye = jnp.eye(n_pg, dtype=F32)
    w_pool_bd = (w_pool[:, :, :, None, :] * eye[None, :, None, :, None]).reshape(depth, n_pg * pg, n_pg * pg).astype(BF16)
    def router_lanes(group_part, expert_part):
        lead = group_part.shape[:-1]
        gap = jnp.zeros(lead + (ROUTER_EXPERT_ROW - ROUTER_GROUP_ROW - n_groups,), F32)
        tail = jnp.zeros(lead + (LANES - ROUTER_EXPERT_ROW - n_experts,), F32)
        return jnp.concatenate([group_part, gap, expert_part, tail], axis=-1)

    assert ROUTER_GROUP_ROW == 0
    wr = router_lanes(w_router_group, w_router_expert)
    wr_hi = wr.astype(BF16)
    wr_lo = (wr - wr_hi.astype(F32)).astype(BF16)
    w_router = jnp.concatenate([wr_hi, wr_lo], axis=-1)
    b_router = router_lanes(b_router_group, b_router_expert)[:, None, :]

    xp = x_prompt.reshape(bp * tp, d)
    xs = x_sample.transpose(1, 0, 2).reshape(ts * bs, d)
    new_states = [[] for _ in range(6)]
    for l in range(depth):
        params = _layer_params(l, w_in_b, b_in, conv_a_w, conv_a_b, ln_a_g, ln_a_b, w_pool_bd,
                               pool_scale, conv_c_w, w_out_b, b_out, ln1_g, ln1_b, w_router, b_router)
        g2, b2 = ln2_g[l][None, :], ln2_b[l][None, :]

        x1, eidx, wts, pa, pb, pc = _prompt_mixer(xp, bp, tp, bs * ts, params, dims, alpha)
        x1, seidx, swts, sa, sb, sc = _sample_mixer(
            x1, xs, state_conv_a[l].reshape(bs, -1), state_pool_b[l].reshape(bs, -1),
            state_conv_c[l].reshape(bs, -1), params, dims, alpha, bs, ts)
        eidx = jnp.concatenate([eidx.transpose(1, 0, 2).reshape(TOP_K, bp * tp), seidx], axis=1)
        wts = jnp.concatenate([wts.transpose(1, 0, 2).reshape(TOP_K, bp * tp), swts], axis=1)

        pos, tables = _route_tables(eidx, wts, n_groups, n_experts, MOE_TILE, MOE_SUB)
        y = _moe(x1, tables, w_gate, w_up, w_down, l)
        xp, xs = _combine(x1, y, pos, g2, b2, alpha, bp * tp)

        for lst, val in zip(new_states, (pa, sa.reshape(bs, ka - 1, w_a), pb, sb.reshape(bs, kb, w_b),
                                         pc, sc.reshape(bs, kc - 1, w_c))):
            lst.append(val)

    y_prompt = xp.reshape(bp, tp, d)
    y_sample = xs.reshape(ts, bs, d).transpose(1, 0, 2)
    return (y_prompt, y_sample) + tuple(jnp.stack(s) for s in new_states)
```

```python
import functools

import jax
import jax.numpy as jnp
from jax import lax
from jax.experimental import pallas as pl
from jax.experimental.pallas import tpu as pltpu

F32 = jnp.float32
BF16 = jnp.bfloat16
I32 = jnp.int32

POOL_WINDOWS = (2, 4, 8, 16)
TOP_K = 2
LN_EPS = 1e-5
PAST_LEN = 16384

LANES = 128
SUBLANES = 8
VMEM_LIMIT_BYTES = 56 * 1024 * 1024

ROW_CHUNK = 32
PROMPT_TILE = 512
PROMPT_HIST_PITCH = 2
MOE_TILE = 1024
MOE_SUB = 256
COMBINE_TILE = 256
GATHER_UNROLL = 16
DMA_PRIORITIES = 2
INVERT_CHUNK = 4224
INVERT_UNROLL = 16
ROUTE_ROWS = 24
GATHER_PITCH = 9
ROUTER_GROUP_ROW = 0
ROUTER_EXPERT_ROW = 8


def _dot(a, b):
    return jnp.dot(a, b, preferred_element_type=F32)


def _layer_norm(x, g, b):
    mu = jnp.mean(x, axis=-1, keepdims=True)
    xc = x - mu
    var = jnp.mean(xc * xc, axis=-1, keepdims=True)
    return xc * lax.rsqrt(var + LN_EPS) * g + b


def _sigmoid(x):
    return 1.0 / (1.0 + jnp.exp(-x))


def _store_row_tiles(ref, val):
    rows, d = val.shape
    for j in range(d // LANES):
        ref[pl.ds(j, rows, stride=SUBLANES), :] = val[:, j * LANES:(j + 1) * LANES]


def _load_row_tiles(ref, rows, first=0, pitch=SUBLANES):
    return jnp.concatenate(
        [ref[pl.ds(first + j, rows, stride=pitch), :] for j in range(SUBLANES)], axis=-1)


def _hist_load(buf, start, n, pitch):
    parts = []
    for j in range(buf.shape[0]):
        if pitch == 1:
            parts.append(buf[j, pl.ds(start, n), :])
        else:
            parts.append(buf.at[j][pl.ds(pitch * start, n, stride=pitch), :])
    return parts[0] if len(parts) == 1 else jnp.concatenate(parts, axis=-1)


def _hist_store(buf, start, n, pitch, val):
    for j in range(buf.shape[0]):
        piece = val[:, j * LANES:(j + 1) * LANES]
        if pitch == 1:
            buf[j, pl.ds(start, n), :] = piece
        else:
            buf.at[j][pl.ds(pitch * start, n, stride=pitch), :] = piece


def _mixer_rows(x_ref, p, bufs, outs, *, rows, stride, pitch, new_a, new_b, new_c, pos_of_row,
                alpha, n_groups, n_experts):
    (w_in, b_in, wa, ba, lnag, lnab, wbd, pscale, wc, w_out, b_out, ln1g, ln1b,
     wr, br) = p
    proj, buf_a, buf_b, buf_c, dbuf, cat = bufs
    x1_ref, eidx_ref, wts_ref = outs
    ka = wa.shape[0]
    kc = wc.shape[0]
    w_a = wa.shape[1]
    w_b = pscale.shape[1]
    w_c = wc.shape[1]
    o_gate, o_ub, o_cbg, o_ccg, o_ch = w_a, 2 * w_a, 2 * w_a + w_b, 2 * w_a + w_b + w_c, 2 * w_a + w_b + 2 * w_c

    x = x_ref[...]
    proj[...] = _dot(x.astype(BF16), w_in[...]) + b_in[...]

    _hist_store(buf_a, new_a, rows, pitch, proj[:, 0:w_a] * _sigmoid(proj[:, o_gate:o_gate + w_a]))
    _hist_store(buf_b, new_b, rows, pitch, proj[:, o_ub:o_ub + w_b])
    _hist_store(buf_c, new_c, rows, pitch, proj[:, o_ccg:o_ccg + w_c] * proj[:, o_ch:o_ch + w_c])

    lane_b = lax.broadcasted_iota(I32, (ROW_CHUNK, w_b), 1)
    group_b = lane_b // (w_b // len(POOL_WINDOWS))
    win_b = jnp.zeros((ROW_CHUNK, w_b), I32)
    for g, w in enumerate(POOL_WINDOWS):
        win_b = jnp.where(group_b == g, w, win_b)

    def chunk(c, carry):
        r0 = c * ROW_CHUNK if isinstance(c, int) else pl.multiple_of(c * ROW_CHUNK, ROW_CHUNK)
        acc = jnp.zeros((ROW_CHUNK, w_a), F32) + ba[...]
        for k in range(ka):
            src = _hist_load(buf_a, r0 + (new_a - (ka - 1 - k) * stride), ROW_CHUNK, pitch)
            acc = acc + wa[k:k + 1, :] * src
        a = _layer_norm(acc, lnag[...], lnab[...])
        cat[pl.ds(r0, ROW_CHUNK), 0:w_a] = (a * _sigmoid(a)).astype(BF16)
        cur = _hist_load(buf_b, r0 + new_b, ROW_CHUNK, pitch)
        run = cur
        wsum = jnp.zeros((ROW_CHUNK, w_b), F32)
        for j in range(1, max(POOL_WINDOWS) + 1):
            if j in POOL_WINDOWS:
                wsum = jnp.where(win_b == j, run, wsum)
            if j < max(POOL_WINDOWS):
                run = run + _hist_load(buf_b, r0 + (new_b - j * stride), ROW_CHUNK, pitch)
        pos = pos_of_row(r0 + lax.broadcasted_iota(I32, (ROW_CHUNK, w_b), 0))
        cnt = jnp.minimum(pos + 1, win_b).astype(F32)
        dbuf[pl.ds(r0, ROW_CHUNK), :] = (wsum / cnt - cur).astype(BF16)
        accc = jnp.zeros((ROW_CHUNK, w_c), F32)
        for k in range(kc):
            src = _hist_load(buf_c, r0 + (new_c - (kc - 1 - k) * stride), ROW_CHUNK, pitch)
            accc = accc + wc[k:k + 1, :] * src
        cbg = proj[pl.ds(r0, ROW_CHUNK), o_cbg:o_cbg + w_c]
        cat[pl.ds(r0, ROW_CHUNK), w_a + w_b:w_a + w_b + w_c] = (cbg * accc).astype(BF16)
        return carry

    if pitch == 1:
        assert stride % SUBLANES == 0
        lax.fori_loop(0, rows // ROW_CHUNK, chunk, 0)
    else:
        for c in range(rows // ROW_CHUNK):
            chunk(c, 0)

    cat[:, w_a:w_a + w_b] = (_dot(dbuf[...], wbd[...]) * pscale[...]).astype(BF16)

    m = _dot(cat[...], w_out[...]) + b_out[...]
    x1 = _layer_norm(alpha * x + m, ln1g[...], ln1b[...])
    _store_row_tiles(x1_ref, x1)

    x1_hi = x1.astype(BF16)
    x1_lo = (x1 - x1_hi.astype(F32)).astype(BF16)
    p_hi = _dot(x1_hi, wr[...])
    p_lo = _dot(x1_lo, wr[...])
    logits = (p_hi[:, 0:LANES] + p_hi[:, LANES:2 * LANES]) + (p_lo[:, 0:LANES] + p_lo[:, LANES:2 * LANES])
    lt = jnp.transpose(logits + br[...])

    neg = jnp.float32(-jnp.inf)
    gl = lt[ROUTER_GROUP_ROW:ROUTER_GROUP_ROW + SUBLANES, :]
    grow = lax.broadcasted_iota(I32, gl.shape, 0)
    gvalid = grow < n_groups
    glm = jnp.where(gvalid, gl, neg)
    gmax = jnp.max(glm, axis=0, keepdims=True)
    gidx = jnp.min(jnp.where(glm == gmax, grow, SUBLANES), axis=0, keepdims=True)
    gsum = jnp.sum(jnp.where(gvalid, jnp.exp(gl - gmax), 0.0), axis=0, keepdims=True)
    g_p = 1.0 / gsum

    el = lt[ROUTER_EXPERT_ROW:ROUTER_EXPERT_ROW + n_experts, :]
    erow = lax.broadcasted_iota(I32, el.shape, 0)
    v = jnp.where(erow // (n_experts // n_groups) == gidx, el, neg)
    v1 = jnp.max(v, axis=0, keepdims=True)
    i1 = jnp.min(jnp.where(v == v1, erow, n_experts), axis=0, keepdims=True)
    vv = jnp.where(erow == i1, neg, v)
    v2 = jnp.max(vv, axis=0, keepdims=True)
    i2 = jnp.min(jnp.where(vv == v2, erow, n_experts), axis=0, keepdims=True)
    e2 = jnp.exp(v2 - v1)
    den = 1.0 + e2
    eidx_ref[0:1, :] = i1
    eidx_ref[1:2, :] = i2
    wts_ref[0:1, :] = (1.0 / den) * g_p
    wts_ref[1:2, :] = (e2 / den) * g_p


def _prompt_mixer_kernel(x_ref, *refs, n_params, n_seq, extra_tiles, tile, hist, pitch, alpha,
                         n_groups, n_experts):
    p = refs[:n_params]
    x1_ref, eidx_ref, wts_ref, sa_ref, sb_ref, sc_ref = refs[n_params:n_params + 6]
    bufs = refs[n_params + 6:]
    _, buf_a, buf_b, buf_c, _, _ = bufs
    ha, hb, hc = hist
    s = pl.program_id(0)
    t = pl.program_id(1)

    @pl.when(s < n_seq)
    def _():
        hists = ((buf_a, ha), (buf_b, hb), (buf_c, hc))

        @pl.when(t == 0)
        def _():
            for buf, h in hists:
                _hist_store(buf, 0, h, pitch, jnp.zeros((h, buf.shape[0] * LANES), F32))

        _mixer_rows(x_ref, p, bufs, (x1_ref, eidx_ref, wts_ref), rows=tile, stride=1, pitch=pitch,
                    new_a=ha, new_b=hb, new_c=hc,
                    pos_of_row=lambda r: r + t * tile,
                    alpha=alpha, n_groups=n_groups, n_experts=n_experts)

        tails = [_hist_load(buf, tile, h, pitch) for buf, h in hists]
        for (buf, h), tail in zip(hists, tails):
            _hist_store(buf, 0, h, pitch, tail)

        @pl.when(t == pl.num_programs(1) - 1)
        def _():
            for ref, (_, h), tail in zip((sa_ref, sb_ref, sc_ref), hists, tails):
                ref[...] = tail[h - ref.shape[0]:h, :]

    @pl.when(jnp.logical_and(s == n_seq, t < extra_tiles))
    def _():
        x1_ref[...] = jnp.zeros(x1_ref.shape, F32)


def _sample_mixer_kernel(x1_all_hbm, x_ref, sta_ref, stb_ref, stc_ref, *refs, n_params, batch, steps,
                         alpha, n_groups, n_experts):
    del x1_all_hbm
    p = refs[:n_params]
    x1_ref, eidx_ref, wts_ref, sa_ref, sb_ref, sc_ref = refs[n_params:n_params + 6]
    bufs = refs[n_params + 6:]
    _, buf_a, buf_b, buf_c, _, _ = bufs
    rows = batch * steps
    hists = []
    for buf, st_ref, out_ref in ((buf_a, sta_ref, sa_ref), (buf_b, stb_ref, sb_ref),
                                 (buf_c, stc_ref, sc_ref)):
        width = buf.shape[0] * LANES
        hists.append((buf, st_ref, out_ref, width, st_ref.shape[1] // width))

    for buf, st_ref, _, width, n in hists:
        for j in range(n):
            _hist_store(buf, j * batch, batch, 1, st_ref[:, j * width:(j + 1) * width])

    na, nb, nc = (h[4] for h in hists)
    _mixer_rows(x_ref, p, bufs, (x1_ref, eidx_ref, wts_ref), rows=rows, stride=batch, pitch=1,
                new_a=na * batch, new_b=nb * batch, new_c=nc * batch,
                pos_of_row=lambda r: PAST_LEN + r // batch,
                alpha=alpha, n_groups=n_groups, n_experts=n_experts)

    for buf, _, out_ref, width, n in hists:
        for j in range(n):
            out_ref[:, j * width:(j + 1) * width] = _hist_load(buf, (j + steps) * batch, batch, 1)


def _full_spec(a):
    nd = a.ndim
    return pl.BlockSpec(a.shape, lambda *_: (0,) * nd)


def _mixer_scratch(rows, ha, hb, hc, d_in, w_a, w_b, w_c, pitch):
    return [
        pltpu.VMEM((rows, d_in), F32),
        pltpu.VMEM((w_a // LANES, pitch * (ha + rows), LANES), F32),
        pltpu.VMEM((w_b // LANES, pitch * (hb + rows), LANES), F32),
        pltpu.VMEM((w_c // LANES, pitch * (hc + rows), LANES), F32),
        pltpu.VMEM((rows, w_b), BF16),
        pltpu.VMEM((rows, w_a + w_b + w_c), BF16),
    ]


def _round_up(n, m):
    return (n + m - 1) // m * m


def _prompt_mixer(x, b, t, extra_rows, params, dims, alpha):
    d = x.shape[1]
    w_a, w_b, w_c, ka, kb, kc, n_groups, n_experts = dims
    tile = PROMPT_TILE
    nt = t // tile
    ha, hb, hc = _round_up(ka - 1, SUBLANES), _round_up(kb, SUBLANES), _round_up(kc - 1, SUBLANES)
    d_in = params[0].shape[1]
    extra_tiles = extra_rows // tile
    assert extra_rows % tile == 0 and extra_tiles <= nt
    kern = functools.partial(_prompt_mixer_kernel, n_params=len(params), n_seq=b,
                             extra_tiles=extra_tiles, tile=tile, hist=(ha, hb, hc),
                             pitch=PROMPT_HIST_PITCH, alpha=alpha,
                             n_groups=n_groups, n_experts=n_experts)
    last_block = b * nt + extra_tiles - 1

    def seq_tile(i, j):
        return jnp.minimum(i, b - 1), jnp.where(i < b, j, nt - 1)

    def row_block(i, j):
        s, t_ = seq_tile(i, j)
        return s * nt + t_

    out_shape = (
        jax.ShapeDtypeStruct(((b * t + extra_rows) * SUBLANES, LANES), F32),
        jax.ShapeDtypeStruct((b, TOP_K, t), I32),
        jax.ShapeDtypeStruct((b, TOP_K, t), F32),
        jax.ShapeDtypeStruct((b, ka - 1, w_a), F32),
        jax.ShapeDtypeStruct((b, kb, w_b), F32),
        jax.ShapeDtypeStruct((b, kc - 1, w_c), F32),
    )
    out_specs = (
        pl.BlockSpec((tile * SUBLANES, LANES), lambda i, j: (jnp.minimum(i * nt + j, last_block), 0)),
        pl.BlockSpec((None, TOP_K, tile), lambda i, j: (seq_tile(i, j)[0], 0, seq_tile(i, j)[1])),
        pl.BlockSpec((None, TOP_K, tile), lambda i, j: (seq_tile(i, j)[0], 0, seq_tile(i, j)[1])),
        pl.BlockSpec((None, ka - 1, w_a), lambda i, j: (jnp.minimum(i, b - 1), 0, 0)),
        pl.BlockSpec((None, kb, w_b), lambda i, j: (jnp.minimum(i, b - 1), 0, 0)),
        pl.BlockSpec((None, kc - 1, w_c), lambda i, j: (jnp.minimum(i, b - 1), 0, 0)),
    )
    in_specs = [pl.BlockSpec((tile, d), lambda i, j: (row_block(i, j), 0))] + [_full_spec(a) for a in params]
    return pl.pallas_call(
        kern,
        grid=(b + 1, nt),
        in_specs=in_specs,
        out_specs=out_specs,
        out_shape=out_shape,
        scratch_shapes=_mixer_scratch(tile, ha, hb, hc, d_in, w_a, w_b, w_c, PROMPT_HIST_PITCH),
        compiler_params=pltpu.CompilerParams(
            dimension_semantics=("arbitrary", "arbitrary"),
            vmem_limit_bytes=VMEM_LIMIT_BYTES),
        name="prompt_mixer",
    )(x, *params)


def _sample_mixer(x1_all, x_tm, st_a, st_b, st_c, params, dims, alpha, batch, steps):
    rows, d = x_tm.shape
    w_a, w_b, w_c, ka, kb, kc, n_groups, n_experts = dims
    d_in = params[0].shape[1]
    first = x1_all.shape[0] // SUBLANES - rows
    assert first % rows == 0
    kern = functools.partial(_sample_mixer_kernel, n_params=len(params), batch=batch,
                             steps=steps, alpha=alpha, n_groups=n_groups, n_experts=n_experts)
    out_shape = (
        jax.ShapeDtypeStruct(x1_all.shape, F32),
        jax.ShapeDtypeStruct((TOP_K, rows), I32),
        jax.ShapeDtypeStruct((TOP_K, rows), F32),
        jax.ShapeDtypeStruct(st_a.shape, F32),
        jax.ShapeDtypeStruct(st_b.shape, F32),
        jax.ShapeDtypeStruct(st_c.shape, F32),
    )
    args = (x_tm, st_a, st_b, st_c) + tuple(params)
    out_specs = (pl.BlockSpec((rows * SUBLANES, LANES), lambda i: (first // rows, 0)),) + tuple(
        pl.BlockSpec(s.shape, lambda i, n=len(s.shape): (0,) * n) for s in out_shape[1:])
    return pl.pallas_call(
        kern,
        grid=(1,),
        in_specs=[pl.BlockSpec(memory_space=pl.ANY)] + [_full_spec(a) for a in args],
        out_specs=out_specs,
        out_shape=out_shape,
        input_output_aliases={0: 0},
        scratch_shapes=_mixer_scratch(rows, (ka - 1) * batch, kb * batch, (kc - 1) * batch,
                                      d_in, w_a, w_b, w_c, 1),
        compiler_params=pltpu.CompilerParams(
            dimension_semantics=("arbitrary",),
            vmem_limit_bytes=VMEM_LIMIT_BYTES),
        name="sample_mixer",
    )(x1_all, *args)


def _start_row_gather(src_hbm, idx_ref, dst_vmem, n, sem):
    def body(c, carry):
        for u in range(GATHER_UNROLL):
            s = c * GATHER_UNROLL + u
            row = idx_ref[0, 0, s]
            pltpu.make_async_copy(
                src_hbm.at[pl.ds(pl.multiple_of(row * SUBLANES, SUBLANES), SUBLANES)],
                dst_vmem.at[pl.ds(s * GATHER_PITCH, SUBLANES)],
                sem).start(priority=u % DMA_PRIORITIES)
        return carry
    lax.fori_loop(0, n // GATHER_UNROLL, body, 0)


def _wait_row_gather(src_hbm, dst_vmem, n, sem):
    pltpu.make_async_copy(src_hbm.at[pl.ds(0, n * SUBLANES)], dst_vmem.at[pl.ds(0, n * SUBLANES)],
                          sem).wait()


def _moe_kernel(te_ref, cnt_ref, nu_ref, src0_ref, src1_ref, x_hbm, wg_ref, wu_ref, wd_ref, y_ref,
                xbuf, wg_b, wu_b, wd_b, sem, *, tile, sub):
    i = pl.program_id(0)
    n_used = nu_ref[0]
    slot = i % 2
    nxt = jnp.minimum(i + 1, pl.num_programs(0) - 1)

    @pl.when(i == 0)
    def _():
        _start_row_gather(x_hbm, src0_ref, xbuf.at[0], cnt_ref[0] * sub, sem.at[0])

    @pl.when(i + 1 < n_used)
    def _():
        _start_row_gather(x_hbm, src1_ref, xbuf.at[1 - slot], cnt_ref[nxt] * sub, sem.at[1 - slot])

    @pl.when(i < n_used)
    def _():
        @pl.when(jnp.logical_or(i == 0, te_ref[i] != te_ref[jnp.maximum(i - 1, 0)]))
        def _():
            wg_b[...] = wg_ref[...].astype(BF16)
            wu_b[...] = wu_ref[...].astype(BF16)
            wd_b[...] = wd_ref[...].astype(BF16)

        _wait_row_gather(x_hbm, xbuf.at[slot], cnt_ref[i] * sub, sem.at[slot])

        for c in range(1, tile // sub + 1):
            @pl.when(cnt_ref[i] == c)
            def _(rows=c * sub):
                xb = _load_row_tiles(xbuf.at[slot], rows, pitch=GATHER_PITCH).astype(BF16)
                hg = _dot(xb, wg_b[...])
                hu = _dot(xb, wu_b[...])
                h = hg * _sigmoid(hg) * hu
                _store_row_tiles(y_ref, _dot(h.astype(BF16), wd_b[...]))
                if rows < tile:
                    y_ref[pl.ds(rows * SUBLANES, (tile - rows) * SUBLANES), :] = jnp.zeros(
                        ((tile - rows) * SUBLANES, LANES), F32)

    @pl.when(i >= n_used)
    def _():
        y_ref[...] = jnp.zeros(y_ref.shape, F32)


def _moe(x1, src, tile_expert, tile_subs, n_used, w_gate, w_up, w_down, layer):
    d = w_gate.shape[-2]
    tile = MOE_TILE
    n_tiles = src.shape[0] // tile
    d_e = w_gate.shape[-1]
    grid_spec = pltpu.PrefetchScalarGridSpec(
        num_scalar_prefetch=3,
        grid=(n_tiles,),
        in_specs=[
            pl.BlockSpec((1, 1, tile), lambda i, te, cnt, nu: (i, 0, 0), memory_space=pltpu.SMEM),
            pl.BlockSpec((1, 1, tile), lambda i, te, cnt, nu: (jnp.minimum(i + 1, n_tiles - 1), 0, 0),
                         memory_space=pltpu.SMEM),
            pl.BlockSpec(memory_space=pl.ANY),
            pl.BlockSpec((None, None, d, d_e), lambda i, te, cnt, nu: (layer, te[i], 0, 0)),
            pl.BlockSpec((None, None, d, d_e), lambda i, te, cnt, nu: (layer, te[i], 0, 0)),
            pl.BlockSpec((None, None, d_e, d), lambda i, te, cnt, nu: (layer, te[i], 0, 0)),
        ],
        out_specs=pl.BlockSpec((tile * SUBLANES, LANES), lambda i, te, cnt, nu: (i, 0)),
        scratch_shapes=[
            pltpu.VMEM((2, tile * GATHER_PITCH, LANES), F32),
            pltpu.VMEM((d, d_e), BF16),
            pltpu.VMEM((d, d_e), BF16),
            pltpu.VMEM((d_e, d), BF16),
            pltpu.SemaphoreType.DMA((2,)),
        ],
    )
    src3 = src.reshape(n_tiles, 1, tile)
    return pl.pallas_call(
        functools.partial(_moe_kernel, tile=tile, sub=MOE_SUB),
        grid_spec=grid_spec,
        out_shape=jax.ShapeDtypeStruct((n_tiles * tile * SUBLANES, LANES), F32),
        compiler_params=pltpu.CompilerParams(
            dimension_semantics=("arbitrary",),
            vmem_limit_bytes=VMEM_LIMIT_BYTES),
        name="moe",
    )(tile_expert, tile_subs, n_used, src3, src3, x1, w_gate, w_up, w_down)


def _combine_kernel(pos0_ref, pos1_ref, x1_ref, w_ref, g_ref, b_ref, y_hbm, op_ref, os_ref, ybuf, sem,
                    *, tile, alpha, prompt_tiles):
    i = pl.program_id(0)
    rows = TOP_K * tile
    slot = i % 2

    @pl.when(i == 0)
    def _():
        _start_row_gather(y_hbm, pos0_ref, ybuf.at[0], rows, sem.at[0])

    @pl.when(i + 1 < pl.num_programs(0))
    def _():
        _start_row_gather(y_hbm, pos1_ref, ybuf.at[1 - slot], rows, sem.at[1 - slot])

    _wait_row_gather(y_hbm, ybuf.at[slot], rows, sem.at[slot])
    w = w_ref[...]
    y_a = _load_row_tiles(ybuf.at[slot], tile, pitch=GATHER_PITCH)
    y_b = _load_row_tiles(ybuf.at[slot], tile, first=tile * GATHER_PITCH, pitch=GATHER_PITCH)
    moe = w[:, 0:1] * y_a + w[:, 1:2] * y_b
    x1 = _load_row_tiles(x1_ref, tile)
    out = _layer_norm(alpha * x1 + moe, g_ref[...], b_ref[...])

    @pl.when(i < prompt_tiles)
    def _():
        op_ref[...] = out

    @pl.when(i >= prompt_tiles)
    def _():
        os_ref[...] = out


def _combine(x1, y, pos, wts, g, b, alpha, n_prompt):
    n, d = x1.shape[0] // SUBLANES, g.shape[-1]
    tile = COMBINE_TILE
    n_tiles = n // tile
    prompt_tiles = n_prompt // tile
    assert n % tile == 0 and n_prompt % tile == 0 and 0 < prompt_tiles < n_tiles
    pos_t = pos.reshape(TOP_K, n_tiles, tile).transpose(1, 0, 2).reshape(n_tiles, 1, TOP_K * tile)
    return pl.pallas_call(
        functools.partial(_combine_kernel, tile=tile, alpha=alpha, prompt_tiles=prompt_tiles),
        grid=(n_tiles,),
        in_specs=[
            pl.BlockSpec((1, 1, TOP_K * tile), lambda i: (i, 0, 0), memory_space=pltpu.SMEM),
            pl.BlockSpec((1, 1, TOP_K * tile), lambda i: (jnp.minimum(i + 1, n_tiles - 1), 0, 0),
                         memory_space=pltpu.SMEM),
            pl.BlockSpec((tile * SUBLANES, LANES), lambda i: (i, 0)),
            pl.BlockSpec((tile, TOP_K), lambda i: (i, 0)),
            _full_spec(g),
            _full_spec(b),
            pl.BlockSpec(memory_space=pl.ANY),
        ],
        out_specs=(
            pl.BlockSpec((tile, d), lambda i: (jnp.minimum(i, prompt_tiles - 1), 0)),
            pl.BlockSpec((tile, d), lambda i: (jnp.maximum(i - prompt_tiles, 0), 0)),
        ),
        out_shape=(jax.ShapeDtypeStruct((n_prompt, d), F32),
                   jax.ShapeDtypeStruct((n - n_prompt, d), F32)),
        scratch_shapes=[pltpu.VMEM((2, TOP_K * tile * GATHER_PITCH, LANES), F32),
                        pltpu.SemaphoreType.DMA((2,))],
        compiler_params=pltpu.CompilerParams(
            dimension_semantics=("arbitrary",),
            vmem_limit_bytes=VMEM_LIMIT_BYTES),
        name="combine",
    )(pos_t, pos_t, x1, wts.T, g, b, y)


def _route_tables(eidx, n_experts, tile, sub):
    k, n = eidx.shape
    pairs = k * n
    n_tiles = pairs // tile + n_experts
    e = eidx.reshape(pairs)
    counts = jnp.sum((e[:, None] == jnp.arange(n_experts, dtype=I32)[None, :]).astype(I32), axis=0)
    ptiles = (counts + tile - 1) // tile
    tile_end = jnp.cumsum(ptiles)
    offs = (tile_end - ptiles) * tile
    subs_per_tile = tile // sub
    first_subs = (counts + sub - 1) // sub - subs_per_tile * (ptiles - 1)
    head = first_subs * sub
    skip = tile - head
    pos = _pair_positions(e, offs, head, skip)
    n_used = tile_end[-1]
    j = jnp.arange(n_tiles, dtype=I32)
    te = jnp.sum((j[:, None] >= tile_end[None, :]).astype(I32), axis=1)
    te_last = jnp.sum((n_used - 1 >= tile_end).astype(I32))
    te = jnp.where(j < n_used, te, te_last).astype(I32)
    tile_subs = jnp.where(j == (tile_end - ptiles)[te], first_subs[te], subs_per_tile)
    tile_subs = jnp.where(j < n_used, tile_subs, 0).astype(I32)
    src = _invert_positions(pos, n, n_tiles * tile)
    return pos.reshape(k, n), src, te, tile_subs, n_used.reshape(1).astype(I32)


def _positions_kernel(e_ref, tri_ref, low_ref, tab_ref, pos_ref, carry, *, n_experts):
    rows = e_ref.shape[0]

    @pl.when(pl.program_id(0) == 0)
    def _():
        carry[...] = jnp.zeros(carry.shape, F32)

    e = e_ref[...]
    masks = [e == k for k in range(n_experts)]
    stack = jnp.concatenate([m.astype(BF16) for m in masks], axis=0)
    cum = _dot(stack, tri_ref[...])
    tot = jnp.broadcast_to(cum[:, LANES - 1:LANES], cum.shape)
    before = _dot(low_ref[...], tot.astype(BF16))
    rank = jnp.zeros(e.shape, F32)
    off = jnp.zeros(e.shape, F32)
    head = jnp.zeros(e.shape, F32)
    skip = jnp.zeros(e.shape, F32)
    for k in range(n_experts):
        lo, hi = k * rows, (k + 1) * rows
        base = carry[k:k + 1, :]
        rank = jnp.where(masks[k], cum[lo:hi, :] + before[lo:hi, :] + base, rank)
        off = jnp.where(masks[k], tab_ref[0, k:k + 1, :], off)
        head = jnp.where(masks[k], tab_ref[1, k:k + 1, :], head)
        skip = jnp.where(masks[k], tab_ref[2, k:k + 1, :], skip)
        carry[k:k + 1, :] = base + before[hi - 1:hi, :] + tot[hi - 1:hi, :]
    rank = rank - 1.0
    pos_ref[...] = (off + rank + jnp.where(rank >= head, skip, 0.0)).astype(I32)


def _pair_positions(e, offs, head, skip):
    pairs = e.shape[0]
    n_experts = offs.shape[0]
    rows = ROUTE_ROWS
    assert pairs % (rows * LANES) == 0
    stacked = n_experts * rows
    tri = (jnp.arange(LANES)[:, None] <= jnp.arange(LANES)[None, :]).astype(BF16)
    r = jnp.arange(stacked)
    low = ((r[:, None] // rows == r[None, :] // rows) & (r[None, :] < r[:, None])).astype(BF16)
    tabs = jnp.broadcast_to(jnp.stack([offs, head, skip]).astype(F32)[:, :, None],
                            (3, n_experts, LANES))
    pos = pl.pallas_call(
        functools.partial(_positions_kernel, n_experts=n_experts),
        grid=(pairs // (rows * LANES),),
        in_specs=[pl.BlockSpec((rows, LANES), lambda i: (i, 0)),
                  _full_spec(tri), _full_spec(low), _full_spec(tabs)],
        out_specs=pl.BlockSpec((rows, LANES), lambda i: (i, 0)),
        out_shape=jax.ShapeDtypeStruct((pairs // LANES, LANES), I32),
        scratch_shapes=[pltpu.VMEM((n_experts, LANES), F32)],
        compiler_params=pltpu.CompilerParams(dimension_semantics=("arbitrary",)),
        name="pair_positions",
    )(e.reshape(pairs // LANES, LANES), tri, low, tabs)
    return pos.reshape(pairs)


def _invert_kernel(pos_ref, src_ref, zeros_ref, *, chunk, n):
    i = pl.program_id(0)

    @pl.when(i == 0)
    def _():
        zeros_ref[...] = jnp.zeros(zeros_ref.shape, I32)
        pltpu.sync_copy(zeros_ref, src_ref)

    first_tok = (i * chunk) % n

    def body(c, carry):
        for u in range(INVERT_UNROLL):
            s = c * INVERT_UNROLL + u
            dst = pos_ref[0, 0, s]
            src_ref[dst] = first_tok + s
        return carry
    lax.fori_loop(0, chunk // INVERT_UNROLL, body, 0)


def _invert_positions(pos, n, rows):
    pairs = pos.shape[0]
    chunk = INVERT_CHUNK
    assert pairs % chunk == 0 and n % chunk == 0 and rows % LANES == 0
    src = pl.pallas_call(
        functools.partial(_invert_kernel, chunk=chunk, n=n),
        grid=(pairs // chunk,),
        in_specs=[pl.BlockSpec((1, 1, chunk), lambda i: (i, 0, 0), memory_space=pltpu.SMEM)],
        out_specs=pl.BlockSpec(memory_space=pltpu.SMEM),
        out_shape=jax.ShapeDtypeStruct((rows,), I32),
        scratch_shapes=[pltpu.VMEM((rows,), I32)],
        compiler_params=pltpu.CompilerParams(dimension_semantics=("arbitrary",)),
        name="invert_positions",
    )(pos.reshape(pairs // chunk, 1, chunk))
    return src


def _layer_params(l, w_in_b, b_in, conv_a_w, conv_a_b, ln_a_g, ln_a_b, w_pool_bd, pool_scale,
                  conv_c_w, w_out_b, b_out, ln1_g, ln1_b, w_router, b_router):
    row = lambda a: a[l][None, :]
    return (w_in_b[l], row(b_in), conv_a_w[l], row(conv_a_b), row(ln_a_g), row(ln_a_b),
            w_pool_bd[l], row(pool_scale), conv_c_w[l], w_out_b[l], row(b_out), row(ln1_g),
            row(ln1_b), w_router[l], b_router[l])


def kernel(x_prompt, x_sample, state_conv_a, state_pool_b, state_conv_c, w_in, b_in, conv_a_w, conv_a_b, ln_a_g, ln_a_b, w_pool, pool_scale, conv_c_w, w_out, b_out, ln1_g, ln1_b, w_router_group, b_router_group, w_router_expert, b_router_expert, w_gate, w_up, w_down, ln2_g, ln2_b):
    depth = w_in.shape[0]
    bp, tp, d = x_prompt.shape
    bs, ts, _ = x_sample.shape
    ka, w_a = conv_a_w.shape[1:]
    kb, w_b = state_pool_b.shape[2:]
    kc, w_c = conv_c_w.shape[1:]
    n_groups = w_router_group.shape[-1]
    n_experts = w_router_expert.shape[-1]
    dims = (w_a, w_b, w_c, ka, kb, kc, n_groups, n_experts)
    alpha = float((2 * depth) ** 0.25)
    assert tp % PROMPT_TILE == 0 and PROMPT_TILE % ROW_CHUNK == 0 and (bs * ts) % ROW_CHUNK == 0
    assert n_groups <= SUBLANES and ROUTER_EXPERT_ROW + n_experts <= LANES
    assert d == SUBLANES * LANES

    w_in_b = w_in.astype(BF16)
    w_out_b = w_out.astype(BF16)
    n_pg, pg = w_pool.shape[1], w_pool.shape[2]
    eye = jnp.eye(n_pg, dtype=F32)
    w_pool_bd = (w_pool[:, :, :, None, :] * eye[None, :, None, :, None]).reshape(depth, n_pg * pg, n_pg * pg).astype(BF16)

    def router_lanes(group_part, expert_part):
        lead = group_part.shape[:-1]
        gap = jnp.zeros(lead + (ROUTER_EXPERT_ROW - ROUTER_GROUP_ROW - n_groups,), F32)
        tail = jnp.zeros(lead + (LANES - ROUTER_EXPERT_ROW - n_experts,), F32)
        return jnp.concatenate([group_part, gap, expert_part, tail], axis=-1)

    assert ROUTER_GROUP_ROW == 0
    wr = router_lanes(w_router_group, w_router_expert)
    wr_hi = wr.astype(BF16)
    wr_lo = (wr - wr_hi.astype(F32)).astype(BF16)
    w_router = jnp.concatenate([wr_hi, wr_lo], axis=-1)
    b_router = router_lanes(b_router_group, b_router_expert)[:, None, :]

    xp = x_prompt.reshape(bp * tp, d)
    xs = x_sample.transpose(1, 0, 2).reshape(ts * bs, d)
    new_states = [[] for _ in range(6)]
    for l in range(depth):
        params = _layer_params(l, w_in_b, b_in, conv_a_w, conv_a_b, ln_a_g, ln_a_b, w_pool_bd,
                               pool_scale, conv_c_w, w_out_b, b_out, ln1_g, ln1_b, w_router, b_router)
        g2, b2 = ln2_g[l][None, :], ln2_b[l][None, :]

        x1, eidx, wts, pa, pb, pc = _prompt_mixer(xp, bp, tp, bs * ts, params, dims, alpha)
        x1, seidx, swts, sa, sb, sc = _sample_mixer(
            x1, xs, state_conv_a[l].reshape(bs, -1), state_pool_b[l].reshape(bs, -1),
            state_conv_c[l].reshape(bs, -1), params, dims, alpha, bs, ts)
        eidx = jnp.concatenate([eidx.transpose(1, 0, 2).reshape(TOP_K, bp * tp), seidx], axis=1)
        wts = jnp.concatenate([wts.transpose(1, 0, 2).reshape(TOP_K, bp * tp), swts], axis=1)

        pos, src, te, subs, nu = _route_tables(eidx, n_experts, MOE_TILE, MOE_SUB)
        y = _moe(x1, src, te, subs, nu, w_gate, w_up, w_down, l)
        xp, xs = _combine(x1, y, pos, wts, g2, b2, alpha, bp * tp)

        for lst, val in zip(new_states, (pa, sa.reshape(bs, ka - 1, w_a), pb, sb.reshape(bs, kb, w_b),
                                         pc, sc.reshape(bs, kc - 1, w_c))):
            lst.append(val)

    y_prompt = xp.reshape(bp, tp, d)
    y_sample = xs.reshape(ts, bs, d).transpose(1, 0, 2)
    return (y_prompt, y_sample) + tuple(jnp.stack(s) for s in new_states)
```

```python
import functools

import jax
import jax.numpy as jnp
from jax import lax
from jax.experimental import pallas as pl
from jax.experimental.pallas import tpu as pltpu

F32 = jnp.float32
BF16 = jnp.bfloat16
I32 = jnp.int32

POOL_WINDOWS = (2, 4, 8, 16)
TOP_K = 2
LN_EPS = 1e-5
PAST_LEN = 16384

LANES = 128
SUBLANES = 8
VMEM_LIMIT_BYTES = 56 * 1024 * 1024

ROW_CHUNK = 32
PROMPT_TILE = 512
PROMPT_HIST_PITCH = 2
MOE_TILE = 1024
MOE_SUB = 256
COMBINE_TILE = 256
GATHER_UNROLL = 16
DMA_PRIORITIES = 2
MOE_GATHER_QUEUES = (1,)
INVERT_CHUNK = 4224
INVERT_UNROLL = 16
ROUTE_ROWS = 24
GATHER_PITCH = 9
ROUTER_GROUP_ROW = 0
ROUTER_EXPERT_ROW = 8


def _dot(a, b):
    return jnp.dot(a, b, preferred_element_type=F32)


def _layer_norm(x, g, b):
    mu = jnp.mean(x, axis=-1, keepdims=True)
    xc = x - mu
    var = jnp.mean(xc * xc, axis=-1, keepdims=True)
    return xc * lax.rsqrt(var + LN_EPS) * g + b


def _sigmoid(x):
    return 1.0 / (1.0 + jnp.exp(-x))


def _store_row_tiles(ref, val):
    rows, d = val.shape
    for j in range(d // LANES):
        ref[pl.ds(j, rows, stride=SUBLANES), :] = val[:, j * LANES:(j + 1) * LANES]


def _load_row_tiles(ref, rows, first=0, pitch=SUBLANES):
    return jnp.concatenate(
        [ref[pl.ds(first + j, rows, stride=pitch), :] for j in range(SUBLANES)], axis=-1)


def _hist_load(buf, start, n, pitch):
    parts = []
    for j in range(buf.shape[0]):
        if pitch == 1:
            parts.append(buf[j, pl.ds(start, n), :])
        else:
            parts.append(buf.at[j][pl.ds(pitch * start, n, stride=pitch), :])
    return parts[0] if len(parts) == 1 else jnp.concatenate(parts, axis=-1)


def _hist_store(buf, start, n, pitch, val):
    for j in range(buf.shape[0]):
        piece = val[:, j * LANES:(j + 1) * LANES]
        if pitch == 1:
            buf[j, pl.ds(start, n), :] = piece
        else:
            buf.at[j][pl.ds(pitch * start, n, stride=pitch), :] = piece


def _mixer_rows(x_ref, p, bufs, outs, *, rows, stride, pitch, new_a, new_b, new_c, pos_of_row,
                alpha, n_groups, n_experts):
    (w_in, b_in, wa, ba, lnag, lnab, wbd, pscale, wc, w_out, b_out, ln1g, ln1b,
     wr, br) = p
    proj, buf_a, buf_b, buf_c, dbuf, cat = bufs
    x1_ref, eidx_ref, wts_ref = outs
    ka = wa.shape[0]
    kc = wc.shape[0]
    w_a = wa.shape[1]
    w_b = pscale.shape[1]
    w_c = wc.shape[1]
    o_gate, o_ub, o_cbg, o_ccg, o_ch = w_a, 2 * w_a, 2 * w_a + w_b, 2 * w_a + w_b + w_c, 2 * w_a + w_b + 2 * w_c

    x = x_ref[...]
    proj[...] = _dot(x.astype(BF16), w_in[...]) + b_in[...]

    _hist_store(buf_a, new_a, rows, pitch, proj[:, 0:w_a] * _sigmoid(proj[:, o_gate:o_gate + w_a]))
    _hist_store(buf_b, new_b, rows, pitch, proj[:, o_ub:o_ub + w_b])
    _hist_store(buf_c, new_c, rows, pitch, proj[:, o_ccg:o_ccg + w_c] * proj[:, o_ch:o_ch + w_c])

    lane_b = lax.broadcasted_iota(I32, (ROW_CHUNK, w_b), 1)
    group_b = lane_b // (w_b // len(POOL_WINDOWS))
    win_b = jnp.zeros((ROW_CHUNK, w_b), I32)
    for g, w in enumerate(POOL_WINDOWS):
        win_b = jnp.where(group_b == g, w, win_b)

    def chunk(c, carry):
        r0 = c * ROW_CHUNK if isinstance(c, int) else pl.multiple_of(c * ROW_CHUNK, ROW_CHUNK)
        acc = jnp.zeros((ROW_CHUNK, w_a), F32) + ba[...]
        for k in range(ka):
            src = _hist_load(buf_a, r0 + (new_a - (ka - 1 - k) * stride), ROW_CHUNK, pitch)
            acc = acc + wa[k:k + 1, :] * src
        a = _layer_norm(acc, lnag[...], lnab[...])
        cat[pl.ds(r0, ROW_CHUNK), 0:w_a] = (a * _sigmoid(a)).astype(BF16)
        cur = _hist_load(buf_b, r0 + new_b, ROW_CHUNK, pitch)
        run = cur
        wsum = jnp.zeros((ROW_CHUNK, w_b), F32)
        for j in range(1, max(POOL_WINDOWS) + 1):
            if j in POOL_WINDOWS:
                wsum = jnp.where(win_b == j, run, wsum)
            if j < max(POOL_WINDOWS):
                run = run + _hist_load(buf_b, r0 + (new_b - j * stride), ROW_CHUNK, pitch)
        pos = pos_of_row(r0 + lax.broadcasted_iota(I32, (ROW_CHUNK, w_b), 0))
        cnt = jnp.minimum(pos + 1, win_b).astype(F32)
        dbuf[pl.ds(r0, ROW_CHUNK), :] = (wsum / cnt - cur).astype(BF16)
        accc = jnp.zeros((ROW_CHUNK, w_c), F32)
        for k in range(kc):
            src = _hist_load(buf_c, r0 + (new_c - (kc - 1 - k) * stride), ROW_CHUNK, pitch)
            accc = accc + wc[k:k + 1, :] * src
        cbg = proj[pl.ds(r0, ROW_CHUNK), o_cbg:o_cbg + w_c]
        cat[pl.ds(r0, ROW_CHUNK), w_a + w_b:w_a + w_b + w_c] = (cbg * accc).astype(BF16)
        return carry

    if pitch == 1:
        assert stride % SUBLANES == 0
        lax.fori_loop(0, rows // ROW_CHUNK, chunk, 0)
    else:
        for c in range(rows // ROW_CHUNK):
            chunk(c, 0)

    cat[:, w_a:w_a + w_b] = (_dot(dbuf[...], wbd[...]) * pscale[...]).astype(BF16)

    m = _dot(cat[...], w_out[...]) + b_out[...]
    x1 = _layer_norm(alpha * x + m, ln1g[...], ln1b[...])
    _store_row_tiles(x1_ref, x1)

    x1_hi = x1.astype(BF16)
    x1_lo = (x1 - x1_hi.astype(F32)).astype(BF16)
    p_hi = _dot(x1_hi, wr[...])
    p_lo = _dot(x1_lo, wr[...])
    logits = (p_hi[:, 0:LANES] + p_hi[:, LANES:2 * LANES]) + (p_lo[:, 0:LANES] + p_lo[:, LANES:2 * LANES])
    lt = jnp.transpose(logits + br[...])

    neg = jnp.float32(-jnp.inf)
    gl = lt[ROUTER_GROUP_ROW:ROUTER_GROUP_ROW + SUBLANES, :]
    grow = lax.broadcasted_iota(I32, gl.shape, 0)
    gvalid = grow < n_groups
    glm = jnp.where(gvalid, gl, neg)
    gmax = jnp.max(glm, axis=0, keepdims=True)
    gidx = jnp.min(jnp.where(glm == gmax, grow, SUBLANES), axis=0, keepdims=True)
    gsum = jnp.sum(jnp.where(gvalid, jnp.exp(gl - gmax), 0.0), axis=0, keepdims=True)
    g_p = 1.0 / gsum

    el = lt[ROUTER_EXPERT_ROW:ROUTER_EXPERT_ROW + n_experts, :]
    erow = lax.broadcasted_iota(I32, el.shape, 0)
    v = jnp.where(erow // (n_experts // n_groups) == gidx, el, neg)
    v1 = jnp.max(v, axis=0, keepdims=True)
    i1 = jnp.min(jnp.where(v == v1, erow, n_experts), axis=0, keepdims=True)
    vv = jnp.where(erow == i1, neg, v)
    v2 = jnp.max(vv, axis=0, keepdims=True)
    i2 = jnp.min(jnp.where(vv == v2, erow, n_experts), axis=0, keepdims=True)
    e2 = jnp.exp(v2 - v1)
    den = 1.0 + e2
    eidx_ref[0:1, :] = i1
    eidx_ref[1:2, :] = i2
    wts_ref[0:1, :] = (1.0 / den) * g_p
    wts_ref[1:2, :] = (e2 / den) * g_p


def _prompt_mixer_kernel(x_ref, *refs, n_params, n_seq, extra_tiles, tile, hist, pitch, alpha,
                         n_groups, n_experts):
    p = refs[:n_params]
    x1_ref, eidx_ref, wts_ref, sa_ref, sb_ref, sc_ref = refs[n_params:n_params + 6]
    bufs = refs[n_params + 6:]
    _, buf_a, buf_b, buf_c, _, _ = bufs
    ha, hb, hc = hist
    s = pl.program_id(0)
    t = pl.program_id(1)

    @pl.when(s < n_seq)
    def _():
        hists = ((buf_a, ha), (buf_b, hb), (buf_c, hc))

        @pl.when(t == 0)
        def _():
            for buf, h in hists:
                _hist_store(buf, 0, h, pitch, jnp.zeros((h, buf.shape[0] * LANES), F32))

        _mixer_rows(x_ref, p, bufs, (x1_ref, eidx_ref, wts_ref), rows=tile, stride=1, pitch=pitch,
                    new_a=ha, new_b=hb, new_c=hc,
                    pos_of_row=lambda r: r + t * tile,
                    alpha=alpha, n_groups=n_groups, n_experts=n_experts)

        tails = [_hist_load(buf, tile, h, pitch) for buf, h in hists]
        for (buf, h), tail in zip(hists, tails):
            _hist_store(buf, 0, h, pitch, tail)

        @pl.when(t == pl.num_programs(1) - 1)
        def _():
            for ref, (_, h), tail in zip((sa_ref, sb_ref, sc_ref), hists, tails):
                ref[...] = tail[h - ref.shape[0]:h, :]

    @pl.when(jnp.logical_and(s == n_seq, t < extra_tiles))
    def _():
        x1_ref[...] = jnp.zeros(x1_ref.shape, F32)


def _sample_mixer_kernel(x1_all_hbm, x_ref, sta_ref, stb_ref, stc_ref, *refs, n_params, batch, steps,
                         alpha, n_groups, n_experts):
    del x1_all_hbm
    p = refs[:n_params]
    x1_ref, eidx_ref, wts_ref, sa_ref, sb_ref, sc_ref = refs[n_params:n_params + 6]
    bufs = refs[n_params + 6:]
    _, buf_a, buf_b, buf_c, _, _ = bufs
    rows = batch * steps
    hists = []
    for buf, st_ref, out_ref in ((buf_a, sta_ref, sa_ref), (buf_b, stb_ref, sb_ref),
                                 (buf_c, stc_ref, sc_ref)):
        width = buf.shape[0] * LANES
        hists.append((buf, st_ref, out_ref, width, st_ref.shape[1] // width))

    for buf, st_ref, _, width, n in hists:
        for j in range(n):
            _hist_store(buf, j * batch, batch, 1, st_ref[:, j * width:(j + 1) * width])

    na, nb, nc = (h[4] for h in hists)
    _mixer_rows(x_ref, p, bufs, (x1_ref, eidx_ref, wts_ref), rows=rows, stride=batch, pitch=1,
                new_a=na * batch, new_b=nb * batch, new_c=nc * batch,
                pos_of_row=lambda r: PAST_LEN + r // batch,
                alpha=alpha, n_groups=n_groups, n_experts=n_experts)

    for buf, _, out_ref, width, n in hists:
        for j in range(n):
            out_ref[:, j * width:(j + 1) * width] = _hist_load(buf, (j + steps) * batch, batch, 1)


def _full_spec(a):
    nd = a.ndim
    return pl.BlockSpec(a.shape, lambda *_: (0,) * nd)


def _mixer_scratch(rows, ha, hb, hc, d_in, w_a, w_b, w_c, pitch):
    return [
        pltpu.VMEM((rows, d_in), F32),
        pltpu.VMEM((w_a // LANES, pitch * (ha + rows), LANES), F32),
        pltpu.VMEM((w_b // LANES, pitch * (hb + rows), LANES), F32),
        pltpu.VMEM((w_c // LANES, pitch * (hc + rows), LANES), F32),
        pltpu.VMEM((rows, w_b), BF16),
        pltpu.VMEM((rows, w_a + w_b + w_c), BF16),
    ]


def _round_up(n, m):
    return (n + m - 1) // m * m


def _prompt_mixer(x, b, t, extra_rows, params, dims, alpha):
    d = x.shape[1]
    w_a, w_b, w_c, ka, kb, kc, n_groups, n_experts = dims
    tile = PROMPT_TILE
    nt = t // tile
    ha, hb, hc = _round_up(ka - 1, SUBLANES), _round_up(kb, SUBLANES), _round_up(kc - 1, SUBLANES)
    d_in = params[0].shape[1]
    extra_tiles = extra_rows // tile
    assert extra_rows % tile == 0 and extra_tiles <= nt
    kern = functools.partial(_prompt_mixer_kernel, n_params=len(params), n_seq=b,
                             extra_tiles=extra_tiles, tile=tile, hist=(ha, hb, hc),
                             pitch=PROMPT_HIST_PITCH, alpha=alpha,
                             n_groups=n_groups, n_experts=n_experts)
    last_block = b * nt + extra_tiles - 1

    def seq_tile(i, j):
        return jnp.minimum(i, b - 1), jnp.where(i < b, j, nt - 1)

    def row_block(i, j):
        s, t_ = seq_tile(i, j)
        return s * nt + t_

    out_shape = (
        jax.ShapeDtypeStruct(((b * t + extra_rows) * SUBLANES, LANES), F32),
        jax.ShapeDtypeStruct((b, TOP_K, t), I32),
        jax.ShapeDtypeStruct((b, TOP_K, t), F32),
        jax.ShapeDtypeStruct((b, ka - 1, w_a), F32),
        jax.ShapeDtypeStruct((b, kb, w_b), F32),
        jax.ShapeDtypeStruct((b, kc - 1, w_c), F32),
    )
    out_specs = (
        pl.BlockSpec((tile * SUBLANES, LANES), lambda i, j: (jnp.minimum(i * nt + j, last_block), 0)),
        pl.BlockSpec((None, TOP_K, tile), lambda i, j: (seq_tile(i, j)[0], 0, seq_tile(i, j)[1])),
        pl.BlockSpec((None, TOP_K, tile), lambda i, j: (seq_tile(i, j)[0], 0, seq_tile(i, j)[1])),
        pl.BlockSpec((None, ka - 1, w_a), lambda i, j: (jnp.minimum(i, b - 1), 0, 0)),
        pl.BlockSpec((None, kb, w_b), lambda i, j: (jnp.minimum(i, b - 1), 0, 0)),
        pl.BlockSpec((None, kc - 1, w_c), lambda i, j: (jnp.minimum(i, b - 1), 0, 0)),
    )
    in_specs = [pl.BlockSpec((tile, d), lambda i, j: (row_block(i, j), 0))] + [_full_spec(a) for a in params]
    return pl.pallas_call(
        kern,
        grid=(b + 1, nt),
        in_specs=in_specs,
        out_specs=out_specs,
        out_shape=out_shape,
        scratch_shapes=_mixer_scratch(tile, ha, hb, hc, d_in, w_a, w_b, w_c, PROMPT_HIST_PITCH),
        compiler_params=pltpu.CompilerParams(
            dimension_semantics=("arbitrary", "arbitrary"),
            vmem_limit_bytes=VMEM_LIMIT_BYTES),
        name="prompt_mixer",
    )(x, *params)


def _sample_mixer(x1_all, x_tm, st_a, st_b, st_c, params, dims, alpha, batch, steps):
    rows, d = x_tm.shape
    w_a, w_b, w_c, ka, kb, kc, n_groups, n_experts = dims
    d_in = params[0].shape[1]
    first = x1_all.shape[0] // SUBLANES - rows
    assert first % rows == 0
    kern = functools.partial(_sample_mixer_kernel, n_params=len(params), batch=batch,
                             steps=steps, alpha=alpha, n_groups=n_groups, n_experts=n_experts)
    out_shape = (
        jax.ShapeDtypeStruct(x1_all.shape, F32),
        jax.ShapeDtypeStruct((TOP_K, rows), I32),
        jax.ShapeDtypeStruct((TOP_K, rows), F32),
        jax.ShapeDtypeStruct(st_a.shape, F32),
        jax.ShapeDtypeStruct(st_b.shape, F32),
        jax.ShapeDtypeStruct(st_c.shape, F32),
    )
    args = (x_tm, st_a, st_b, st_c) + tuple(params)
    out_specs = (pl.BlockSpec((rows * SUBLANES, LANES), lambda i: (first // rows, 0)),) + tuple(
        pl.BlockSpec(s.shape, lambda i, n=len(s.shape): (0,) * n) for s in out_shape[1:])
    return pl.pallas_call(
        kern,
        grid=(1,),
        in_specs=[pl.BlockSpec(memory_space=pl.ANY)] + [_full_spec(a) for a in args],
        out_specs=out_specs,
        out_shape=out_shape,
        input_output_aliases={0: 0},
        scratch_shapes=_mixer_scratch(rows, (ka - 1) * batch, kb * batch, (kc - 1) * batch,
                                      d_in, w_a, w_b, w_c, 1),
        compiler_params=pltpu.CompilerParams(
            dimension_semantics=("arbitrary",),
            vmem_limit_bytes=VMEM_LIMIT_BYTES),
        name="sample_mixer",
    )(x1_all, *args)


def _start_row_gather(src_hbm, idx_ref, dst_vmem, n, sem, priorities=tuple(range(DMA_PRIORITIES))):
    def body(c, carry):
        for u in range(GATHER_UNROLL):
            s = c * GATHER_UNROLL + u
            row = idx_ref[0, 0, s]
            pltpu.make_async_copy(
                src_hbm.at[pl.ds(pl.multiple_of(row * SUBLANES, SUBLANES), SUBLANES)],
                dst_vmem.at[pl.ds(s * GATHER_PITCH, SUBLANES)],
                sem).start(priority=priorities[u % len(priorities)])
        return carry
    lax.fori_loop(0, n // GATHER_UNROLL, body, 0)


def _wait_row_gather(src_hbm, dst_vmem, n, sem):
    pltpu.make_async_copy(src_hbm.at[pl.ds(0, n * SUBLANES)], dst_vmem.at[pl.ds(0, n * SUBLANES)],
                          sem).wait()


def _moe_kernel(te_ref, cnt_ref, nu_ref, src0_ref, src1_ref, x_hbm, wg_ref, wu_ref, wd_ref, y_ref,
                xbuf, wg_b, wu_b, wd_b, sem, *, tile, sub):
    i = pl.program_id(0)
    n_used = nu_ref[0]
    slot = i % 2
    nxt = jnp.minimum(i + 1, pl.num_programs(0) - 1)

    @pl.when(i == 0)
    def _():
        _start_row_gather(x_hbm, src0_ref, xbuf.at[0], cnt_ref[0] * sub, sem.at[0], MOE_GATHER_QUEUES)

    @pl.when(i + 1 < n_used)
    def _():
        _start_row_gather(x_hbm, src1_ref, xbuf.at[1 - slot], cnt_ref[nxt] * sub, sem.at[1 - slot],
                          MOE_GATHER_QUEUES)

    @pl.when(i < n_used)
    def _():
        @pl.when(jnp.logical_or(i == 0, te_ref[i] != te_ref[jnp.maximum(i - 1, 0)]))
        def _():
            wg_b[...] = wg_ref[...].astype(BF16)
            wu_b[...] = wu_ref[...].astype(BF16)
            wd_b[...] = wd_ref[...].astype(BF16)

        _wait_row_gather(x_hbm, xbuf.at[slot], cnt_ref[i] * sub, sem.at[slot])

        for c in range(1, tile // sub + 1):
            @pl.when(cnt_ref[i] == c)
            def _(rows=c * sub):
                xb = _load_row_tiles(xbuf.at[slot], rows, pitch=GATHER_PITCH).astype(BF16)
                hg = _dot(xb, wg_b[...])
                hu = _dot(xb, wu_b[...])
                h = hg * _sigmoid(hg) * hu
                _store_row_tiles(y_ref, _dot(h.astype(BF16), wd_b[...]))
                if rows < tile:
                    y_ref[pl.ds(rows * SUBLANES, (tile - rows) * SUBLANES), :] = jnp.zeros(
                        ((tile - rows) * SUBLANES, LANES), F32)

    @pl.when(i >= n_used)
    def _():
        y_ref[...] = jnp.zeros(y_ref.shape, F32)


def _moe(x1, src, tile_expert, tile_subs, n_used, w_gate, w_up, w_down, layer):
    d = w_gate.shape[-2]
    tile = MOE_TILE
    n_tiles = src.shape[0] // tile
    d_e = w_gate.shape[-1]
    grid_spec = pltpu.PrefetchScalarGridSpec(
        num_scalar_prefetch=3,
        grid=(n_tiles,),
        in_specs=[
            pl.BlockSpec((1, 1, tile), lambda i, te, cnt, nu: (i, 0, 0), memory_space=pltpu.SMEM),
            pl.BlockSpec((1, 1, tile), lambda i, te, cnt, nu: (jnp.minimum(i + 1, n_tiles - 1), 0, 0),
                         memory_space=pltpu.SMEM),
            pl.BlockSpec(memory_space=pl.ANY),
            pl.BlockSpec((None, None, d, d_e), lambda i, te, cnt, nu: (layer, te[i], 0, 0)),
            pl.BlockSpec((None, None, d, d_e), lambda i, te, cnt, nu: (layer, te[i], 0, 0)),
            pl.BlockSpec((None, None, d_e, d), lambda i, te, cnt, nu: (layer, te[i], 0, 0)),
        ],
        out_specs=pl.BlockSpec((tile * SUBLANES, LANES), lambda i, te, cnt, nu: (i, 0)),
        scratch_shapes=[
            pltpu.VMEM((2, tile * GATHER_PITCH, LANES), F32),
            pltpu.VMEM((d, d_e), BF16),
            pltpu.VMEM((d, d_e), BF16),
            pltpu.VMEM((d_e, d), BF16),
            pltpu.SemaphoreType.DMA((2,)),
        ],
    )
    src3 = src.reshape(n_tiles, 1, tile)
    return pl.pallas_call(
        functools.partial(_moe_kernel, tile=tile, sub=MOE_SUB),
        grid_spec=grid_spec,
        out_shape=jax.ShapeDtypeStruct((n_tiles * tile * SUBLANES, LANES), F32),
        compiler_params=pltpu.CompilerParams(
            dimension_semantics=("arbitrary",),
            vmem_limit_bytes=VMEM_LIMIT_BYTES),
        name="moe",
    )(tile_expert, tile_subs, n_used, src3, src3, x1, w_gate, w_up, w_down)


def _combine_kernel(pos0_ref, pos1_ref, x1_ref, w_ref, g_ref, b_ref, y_hbm, op_ref, os_ref, ybuf, sem,
                    *, tile, alpha, prompt_tiles):
    i = pl.program_id(0)
    rows = TOP_K * tile
    slot = i % 2

    @pl.when(i == 0)
    def _():
        _start_row_gather(y_hbm, pos0_ref, ybuf.at[0], rows, sem.at[0])

    @pl.when(i + 1 < pl.num_programs(0))
    def _():
        _start_row_gather(y_hbm, pos1_ref, ybuf.at[1 - slot], rows, sem.at[1 - slot])

    _wait_row_gather(y_hbm, ybuf.at[slot], rows, sem.at[slot])
    w = w_ref[...]
    y_a = _load_row_tiles(ybuf.at[slot], tile, pitch=GATHER_PITCH)
    y_b = _load_row_tiles(ybuf.at[slot], tile, first=tile * GATHER_PITCH, pitch=GATHER_PITCH)
    moe = w[:, 0:1] * y_a + w[:, 1:2] * y_b
    x1 = _load_row_tiles(x1_ref, tile)
    out = _layer_norm(alpha * x1 + moe, g_ref[...], b_ref[...])

    @pl.when(i < prompt_tiles)
    def _():
        op_ref[...] = out

    @pl.when(i >= prompt_tiles)
    def _():
        os_ref[...] = out


def _combine(x1, y, pos, wts, g, b, alpha, n_prompt):
    n, d = x1.shape[0] // SUBLANES, g.shape[-1]
    tile = COMBINE_TILE
    n_tiles = n // tile
    prompt_tiles = n_prompt // tile
    assert n % tile == 0 and n_prompt % tile == 0 and 0 < prompt_tiles < n_tiles
    pos_t = pos.reshape(TOP_K, n_tiles, tile).transpose(1, 0, 2).reshape(n_tiles, 1, TOP_K * tile)
    return pl.pallas_call(
        functools.partial(_combine_kernel, tile=tile, alpha=alpha, prompt_tiles=prompt_tiles),
        grid=(n_tiles,),
        in_specs=[
            pl.BlockSpec((1, 1, TOP_K * tile), lambda i: (i, 0, 0), memory_space=pltpu.SMEM),
            pl.BlockSpec((1, 1, TOP_K * tile), lambda i: (jnp.minimum(i + 1, n_tiles - 1), 0, 0),
                         memory_space=pltpu.SMEM),
            pl.BlockSpec((tile * SUBLANES, LANES), lambda i: (i, 0)),
            pl.BlockSpec((tile, TOP_K), lambda i: (i, 0)),
            _full_spec(g),
            _full_spec(b),
            pl.BlockSpec(memory_space=pl.ANY),
        ],
        out_specs=(
            pl.BlockSpec((tile, d), lambda i: (jnp.minimum(i, prompt_tiles - 1), 0)),
            pl.BlockSpec((tile, d), lambda i: (jnp.maximum(i - prompt_tiles, 0), 0)),
        ),
        out_shape=(jax.ShapeDtypeStruct((n_prompt, d), F32),
                   jax.ShapeDtypeStruct((n - n_prompt, d), F32)),
        scratch_shapes=[pltpu.VMEM((2, TOP_K * tile * GATHER_PITCH, LANES), F32),
                        pltpu.SemaphoreType.DMA((2,))],
        compiler_params=pltpu.CompilerParams(
            dimension_semantics=("arbitrary",),
            vmem_limit_bytes=VMEM_LIMIT_BYTES),
        name="combine",
    )(pos_t, pos_t, x1, wts.T, g, b, y)


def _route_tables(eidx, n_experts, tile, sub):
    k, n = eidx.shape
    pairs = k * n
    n_tiles = pairs // tile + n_experts
    e = eidx.reshape(pairs)
    counts = jnp.sum((e[:, None] == jnp.arange(n_experts, dtype=I32)[None, :]).astype(I32), axis=0)
    ptiles = (counts + tile - 1) // tile
    tile_end = jnp.cumsum(ptiles)
    offs = (tile_end - ptiles) * tile
    subs_per_tile = tile // sub
    first_subs = (counts + sub - 1) // sub - subs_per_tile * (ptiles - 1)
    head = first_subs * sub
    skip = tile - head
    pos = _pair_positions(e, offs, head, skip)
    n_used = tile_end[-1]
    j = jnp.arange(n_tiles, dtype=I32)
    te = jnp.sum((j[:, None] >= tile_end[None, :]).astype(I32), axis=1)
    te_last = jnp.sum((n_used - 1 >= tile_end).astype(I32))
    te = jnp.where(j < n_used, te, te_last).astype(I32)
    tile_subs = jnp.where(j == (tile_end - ptiles)[te], first_subs[te], subs_per_tile)
    tile_subs = jnp.where(j < n_used, tile_subs, 0).astype(I32)
    src = _invert_positions(pos, n, n_tiles * tile)
    return pos.reshape(k, n), src, te, tile_subs, n_used.reshape(1).astype(I32)


def _positions_kernel(e_ref, tri_ref, low_ref, tab_ref, pos_ref, carry, *, n_experts):
    rows = e_ref.shape[0]

    @pl.when(pl.program_id(0) == 0)
    def _():
        carry[...] = jnp.zeros(carry.shape, F32)

    e = e_ref[...]
    masks = [e == k for k in range(n_experts)]
    stack = jnp.concatenate([m.astype(BF16) for m in masks], axis=0)
    cum = _dot(stack, tri_ref[...])
    tot = jnp.broadcast_to(cum[:, LANES - 1:LANES], cum.shape)
    before = _dot(low_ref[...], tot.astype(BF16))
    rank = jnp.zeros(e.shape, F32)
    off = jnp.zeros(e.shape, F32)
    head = jnp.zeros(e.shape, F32)
    skip = jnp.zeros(e.shape, F32)
    for k in range(n_experts):
        lo, hi = k * rows, (k + 1) * rows
        base = carry[k:k + 1, :]
        rank = jnp.where(masks[k], cum[lo:hi, :] + before[lo:hi, :] + base, rank)
        off = jnp.where(masks[k], tab_ref[0, k:k + 1, :], off)
        head = jnp.where(masks[k], tab_ref[1, k:k + 1, :], head)
        skip = jnp.where(masks[k], tab_ref[2, k:k + 1, :], skip)
        carry[k:k + 1, :] = base + before[hi - 1:hi, :] + tot[hi - 1:hi, :]
    rank = rank - 1.0
    pos_ref[...] = (off + rank + jnp.where(rank >= head, skip, 0.0)).astype(I32)


def _pair_positions(e, offs, head, skip):
    pairs = e.shape[0]
    n_experts = offs.shape[0]
    rows = ROUTE_ROWS
    assert pairs % (rows * LANES) == 0
    stacked = n_experts * rows
    tri = (jnp.arange(LANES)[:, None] <= jnp.arange(LANES)[None, :]).astype(BF16)
    r = jnp.arange(stacked)
    low = ((r[:, None] // rows == r[None, :] // rows) & (r[None, :] < r[:, None])).astype(BF16)
    tabs = jnp.broadcast_to(jnp.stack([offs, head, skip]).astype(F32)[:, :, None],
                            (3, n_experts, LANES))
    pos = pl.pallas_call(
        functools.partial(_positions_kernel, n_experts=n_experts),
        grid=(pairs // (rows * LANES),),
        in_specs=[pl.BlockSpec((rows, LANES), lambda i: (i, 0)),
                  _full_spec(tri), _full_spec(low), _full_spec(tabs)],
        out_specs=pl.BlockSpec((rows, LANES), lambda i: (i, 0)),
        out_shape=jax.ShapeDtypeStruct((pairs // LANES, LANES), I32),
        scratch_shapes=[pltpu.VMEM((n_experts, LANES), F32)],
        compiler_params=pltpu.CompilerParams(dimension_semantics=("arbitrary",)),
        name="pair_positions",
    )(e.reshape(pairs // LANES, LANES), tri, low, tabs)
    return pos.reshape(pairs)


def _invert_kernel(pos_ref, src_ref, zeros_ref, *, chunk, n):
    i = pl.program_id(0)

    @pl.when(i == 0)
    def _():
        zeros_ref[...] = jnp.zeros(zeros_ref.shape, I32)
        pltpu.sync_copy(zeros_ref, src_ref)

    first_tok = (i * chunk) % n

    def body(c, carry):
        for u in range(INVERT_UNROLL):
            s = c * INVERT_UNROLL + u
            dst = pos_ref[0, 0, s]
            src_ref[dst] = first_tok + s
        return carry
    lax.fori_loop(0, chunk // INVERT_UNROLL, body, 0)


def _invert_positions(pos, n, rows):
    pairs = pos.shape[0]
    chunk = INVERT_CHUNK
    assert pairs % chunk == 0 and n % chunk == 0 and rows % LANES == 0
    src = pl.pallas_call(
        functools.partial(_invert_kernel, chunk=chunk, n=n),
        grid=(pairs // chunk,),
        in_specs=[pl.BlockSpec((1, 1, chunk), lambda i: (i, 0, 0), memory_space=pltpu.SMEM)],
        out_specs=pl.BlockSpec(memory_space=pltpu.SMEM),
        out_shape=jax.ShapeDtypeStruct((rows,), I32),
        scratch_shapes=[pltpu.VMEM((rows,), I32)],
        compiler_params=pltpu.CompilerParams(dimension_semantics=("arbitrary",)),
        name="invert_positions",
    )(pos.reshape(pairs // chunk, 1, chunk))
    return src


def _layer_params(l, w_in_b, b_in, conv_a_w, conv_a_b, ln_a_g, ln_a_b, w_pool_bd, pool_scale,
                  conv_c_w, w_out_b, b_out, ln1_g, ln1_b, w_router, b_router):
    row = lambda a: a[l][None, :]
    return (w_in_b[l], row(b_in), conv_a_w[l], row(conv_a_b), row(ln_a_g), row(ln_a_b),
            w_pool_bd[l], row(pool_scale), conv_c_w[l], w_out_b[l], row(b_out), row(ln1_g),
            row(ln1_b), w_router[l], b_router[l])


def kernel(x_prompt, x_sample, state_conv_a, state_pool_b, state_conv_c, w_in, b_in, conv_a_w, conv_a_b, ln_a_g, ln_a_b, w_pool, pool_scale, conv_c_w, w_out, b_out, ln1_g, ln1_b, w_router_group, b_router_group, w_router_expert, b_router_expert, w_gate, w_up, w_down, ln2_g, ln2_b):
    depth = w_in.shape[0]
    bp, tp, d = x_prompt.shape
    bs, ts, _ = x_sample.shape
    ka, w_a = conv_a_w.shape[1:]
    kb, w_b = state_pool_b.shape[2:]
    kc, w_c = conv_c_w.shape[1:]
    n_groups = w_router_group.shape[-1]
    n_experts = w_router_expert.shape[-1]
    dims = (w_a, w_b, w_c, ka, kb, kc, n_groups, n_experts)
    alpha = float((2 * depth) ** 0.25)
    assert tp % PROMPT_TILE == 0 and PROMPT_TILE % ROW_CHUNK == 0 and (bs * ts) % ROW_CHUNK == 0
    assert n_groups <= SUBLANES and ROUTER_EXPERT_ROW + n_experts <= LANES
    assert d == SUBLANES * LANES

    w_in_b = w_in.astype(BF16)
    w_out_b = w_out.astype(BF16)
    n_pg, pg = w_pool.shape[1], w_pool.shape[2]
    eye = jnp.eye(n_pg, dtype=F32)
    w_pool_bd = (w_pool[:, :, :, None, :] * eye[None, :, None, :, None]).reshape(depth, n_pg * pg, n_pg * pg).astype(BF16)

    def router_lanes(group_part, expert_part):
        lead = group_part.shape[:-1]
        gap = jnp.zeros(lead + (ROUTER_EXPERT_ROW - ROUTER_GROUP_ROW - n_groups,), F32)
        tail = jnp.zeros(lead + (LANES - ROUTER_EXPERT_ROW - n_experts,), F32)
        return jnp.concatenate([group_part, gap, expert_part, tail], axis=-1)

    assert ROUTER_GROUP_ROW == 0
    wr = router_lanes(w_router_group, w_router_expert)
    wr_hi = wr.astype(BF16)
    wr_lo = (wr - wr_hi.astype(F32)).astype(BF16)
    w_router = jnp.concatenate([wr_hi, wr_lo], axis=-1)
    b_router = router_lanes(b_router_group, b_router_expert)[:, None, :]

    xp = x_prompt.reshape(bp * tp, d)
    xs = x_sample.transpose(1, 0, 2).reshape(ts * bs, d)
    new_states = [[] for _ in range(6)]
    for l in range(depth):
        params = _layer_params(l, w_in_b, b_in, conv_a_w, conv_a_b, ln_a_g, ln_a_b, w_pool_bd,
                               pool_scale, conv_c_w, w_out_b, b_out, ln1_g, ln1_b, w_router, b_router)
        g2, b2 = ln2_g[l][None, :], ln2_b[l][None, :]

        x1, eidx, wts, pa, pb, pc = _prompt_mixer(xp, bp, tp, bs * ts, params, dims, alpha)
        x1, seidx, swts, sa, sb, sc = _sample_mixer(
            x1, xs, state_conv_a[l].reshape(bs, -1), state_pool_b[l].reshape(bs, -1),
            state_conv_c[l].reshape(bs, -1), params, dims, alpha, bs, ts)
        eidx = jnp.concatenate([eidx.transpose(1, 0, 2).reshape(TOP_K, bp * tp), seidx], axis=1)
        wts = jnp.concatenate([wts.transpose(1, 0, 2).reshape(TOP_K, bp * tp), swts], axis=1)

        pos, src, te, subs, nu = _route_tables(eidx, n_experts, MOE_TILE, MOE_SUB)
        y = _moe(x1, src, te, subs, nu, w_gate, w_up, w_down, l)
        xp, xs = _combine(x1, y, pos, wts, g2, b2, alpha, bp * tp)

        for lst, val in zip(new_states, (pa, sa.reshape(bs, ka - 1, w_a), pb, sb.reshape(bs, kb, w_b),
                                         pc, sc.reshape(bs, kc - 1, w_c))):
            lst.append(val)

    y_prompt = xp.reshape(bp, tp, d)
    y_sample = xs.reshape(ts, bs, d).transpose(1, 0, 2)
    return (y_prompt, y_sample) + tuple(jnp.stack(s) for s in new_states)
```

```python
import functools

import jax
import jax.numpy as jnp
from jax import lax
from jax.experimental import pallas as pl
from jax.experimental.pallas import tpu as pltpu

F32 = jnp.float32
BF16 = jnp.bfloat16
I32 = jnp.int32

POOL_WINDOWS = (2, 4, 8, 16)
TOP_K = 2
LN_EPS = 1e-5
PAST_LEN = 16384

LANES = 128
SUBLANES = 8
VMEM_LIMIT_BYTES = 56 * 1024 * 1024

ROW_CHUNK = 64
PROMPT_TILE = 512
PROMPT_HIST_PITCH = 2
MOE_TILE = 1024
MOE_SUB = 256
COMBINE_TILE = 256
GATHER_UNROLL = 16
DMA_PRIORITIES = 2
INVERT_CHUNK = 4224
INVERT_UNROLL = 16
ROUTE_ROWS = 24
GATHER_PITCH = 9
ROUTER_GROUP_ROW = 0
ROUTER_EXPERT_ROW = 8


def _dot(a, b):
    return jnp.dot(a, b, preferred_element_type=F32)


def _layer_norm(x, g, b):
    mu = jnp.mean(x, axis=-1, keepdims=True)
    xc = x - mu
    var = jnp.mean(xc * xc, axis=-1, keepdims=True)
    return xc * lax.rsqrt(var + LN_EPS) * g + b


def _sigmoid(x):
    return 1.0 / (1.0 + jnp.exp(-x))


def _store_row_tiles(ref, val):
    rows, d = val.shape
    for j in range(d // LANES):
        ref[pl.ds(j, rows, stride=SUBLANES), :] = val[:, j * LANES:(j + 1) * LANES]


def _load_row_tiles(ref, rows, first=0, pitch=SUBLANES):
    return jnp.concatenate(
        [ref[pl.ds(first + j, rows, stride=pitch), :] for j in range(SUBLANES)], axis=-1)


def _hist_load(buf, start, n, pitch):
    parts = []
    for j in range(buf.shape[0]):
        if pitch == 1:
            parts.append(buf[j, pl.ds(start, n), :])
        else:
            parts.append(buf.at[j][pl.ds(pitch * start, n, stride=pitch), :])
    return parts[0] if len(parts) == 1 else jnp.concatenate(parts, axis=-1)


def _hist_store(buf, start, n, pitch, val):
    for j in range(buf.shape[0]):
        piece = val[:, j * LANES:(j + 1) * LANES]
        if pitch == 1:
            buf[j, pl.ds(start, n), :] = piece
        else:
            buf.at[j][pl.ds(pitch * start, n, stride=pitch), :] = piece


def _mixer_rows(x_ref, p, bufs, outs, *, rows, stride, pitch, new_a, new_b, new_c, pos_of_row,
                alpha, n_groups, n_experts):
    (w_in, b_in, wa, ba, lnag, lnab, wbd, pscale, wc, w_out, b_out, ln1g, ln1b,
     wr, br) = p
    proj, buf_a, buf_b, buf_c, dbuf, cat = bufs
    x1_ref, eidx_ref, wts_ref = outs
    ka = wa.shape[0]
    kc = wc.shape[0]
    w_a = wa.shape[1]
    w_b = pscale.shape[1]
    w_c = wc.shape[1]
    o_gate, o_ub, o_cbg, o_ccg, o_ch = w_a, 2 * w_a, 2 * w_a + w_b, 2 * w_a + w_b + w_c, 2 * w_a + w_b + 2 * w_c

    x = x_ref[...]
    proj[...] = _dot(x.astype(BF16), w_in[...]) + b_in[...]

    _hist_store(buf_a, new_a, rows, pitch, proj[:, 0:w_a] * _sigmoid(proj[:, o_gate:o_gate + w_a]))
    _hist_store(buf_b, new_b, rows, pitch, proj[:, o_ub:o_ub + w_b])
    _hist_store(buf_c, new_c, rows, pitch, proj[:, o_ccg:o_ccg + w_c] * proj[:, o_ch:o_ch + w_c])

    lane_b = lax.broadcasted_iota(I32, (ROW_CHUNK, w_b), 1)
    group_b = lane_b // (w_b // len(POOL_WINDOWS))
    win_b = jnp.zeros((ROW_CHUNK, w_b), I32)
    for g, w in enumerate(POOL_WINDOWS):
        win_b = jnp.where(group_b == g, w, win_b)

    def chunk(c, carry):
        r0 = c * ROW_CHUNK if isinstance(c, int) else pl.multiple_of(c * ROW_CHUNK, ROW_CHUNK)
        acc = jnp.zeros((ROW_CHUNK, w_a), F32) + ba[...]
        for k in range(ka):
            src = _hist_load(buf_a, r0 + (new_a - (ka - 1 - k) * stride), ROW_CHUNK, pitch)
            acc = acc + wa[k:k + 1, :] * src
        a = _layer_norm(acc, lnag[...], lnab[...])
        cat[pl.ds(r0, ROW_CHUNK), 0:w_a] = (a * _sigmoid(a)).astype(BF16)
        cur = _hist_load(buf_b, r0 + new_b, ROW_CHUNK, pitch)
        run = cur
        wsum = jnp.zeros((ROW_CHUNK, w_b), F32)
        for j in range(1, max(POOL_WINDOWS) + 1):
            if j in POOL_WINDOWS:
                wsum = jnp.where(win_b == j, run, wsum)
            if j < max(POOL_WINDOWS):
                run = run + _hist_load(buf_b, r0 + (new_b - j * stride), ROW_CHUNK, pitch)
        pos = pos_of_row(r0 + lax.broadcasted_iota(I32, (ROW_CHUNK, w_b), 0))
        cnt = jnp.minimum(pos + 1, win_b).astype(F32)
        dbuf[pl.ds(r0, ROW_CHUNK), :] = (wsum / cnt - cur).astype(BF16)
        accc = jnp.zeros((ROW_CHUNK, w_c), F32)
        for k in range(kc):
            src = _hist_load(buf_c, r0 + (new_c - (kc - 1 - k) * stride), ROW_CHUNK, pitch)
            accc = accc + wc[k:k + 1, :] * src
        cbg = proj[pl.ds(r0, ROW_CHUNK), o_cbg:o_cbg + w_c]
        cat[pl.ds(r0, ROW_CHUNK), w_a + w_b:w_a + w_b + w_c] = (cbg * accc).astype(BF16)
        return carry

    if pitch == 1:
        assert stride % SUBLANES == 0
        lax.fori_loop(0, rows // ROW_CHUNK, chunk, 0)
    else:
        for c in range(rows // ROW_CHUNK):
            chunk(c, 0)

    cat[:, w_a:w_a + w_b] = (_dot(dbuf[...], wbd[...]) * pscale[...]).astype(BF16)

    m = _dot(cat[...], w_out[...]) + b_out[...]
    x1 = _layer_norm(alpha * x + m, ln1g[...], ln1b[...])
    _store_row_tiles(x1_ref, x1)

    x1_hi = x1.astype(BF16)
    x1_lo = (x1 - x1_hi.astype(F32)).astype(BF16)
    p_hi = _dot(x1_hi, wr[...])
    p_lo = _dot(x1_lo, wr[...])
    logits = (p_hi[:, 0:LANES] + p_hi[:, LANES:2 * LANES]) + (p_lo[:, 0:LANES] + p_lo[:, LANES:2 * LANES])
    lt = jnp.transpose(logits + br[...])

    neg = jnp.float32(-jnp.inf)
    gl = lt[ROUTER_GROUP_ROW:ROUTER_GROUP_ROW + SUBLANES, :]
    grow = lax.broadcasted_iota(I32, gl.shape, 0)
    gvalid = grow < n_groups
    glm = jnp.where(gvalid, gl, neg)
    gmax = jnp.max(glm, axis=0, keepdims=True)
    gidx = jnp.min(jnp.where(glm == gmax, grow, SUBLANES), axis=0, keepdims=True)
    gsum = jnp.sum(jnp.where(gvalid, jnp.exp(gl - gmax), 0.0), axis=0, keepdims=True)
    g_p = 1.0 / gsum

    el = lt[ROUTER_EXPERT_ROW:ROUTER_EXPERT_ROW + n_experts, :]
    erow = lax.broadcasted_iota(I32, el.shape, 0)
    v = jnp.where(erow // (n_experts // n_groups) == gidx, el, neg)
    v1 = jnp.max(v, axis=0, keepdims=True)
    i1 = jnp.min(jnp.where(v == v1, erow, n_experts), axis=0, keepdims=True)
    vv = jnp.where(erow == i1, neg, v)
    v2 = jnp.max(vv, axis=0, keepdims=True)
    i2 = jnp.min(jnp.where(vv == v2, erow, n_experts), axis=0, keepdims=True)
    e2 = jnp.exp(v2 - v1)
    den = 1.0 + e2
    eidx_ref[0:1, :] = i1
    eidx_ref[1:2, :] = i2
    wts_ref[0:1, :] = (1.0 / den) * g_p
    wts_ref[1:2, :] = (e2 / den) * g_p


def _prompt_mixer_kernel(x_ref, *refs, n_params, n_seq, extra_tiles, tile, hist, pitch, alpha,
                         n_groups, n_experts):
    p = refs[:n_params]
    x1_ref, eidx_ref, wts_ref, sa_ref, sb_ref, sc_ref = refs[n_params:n_params + 6]
    bufs = refs[n_params + 6:]
    _, buf_a, buf_b, buf_c, _, _ = bufs
    ha, hb, hc = hist
    s = pl.program_id(0)
    t = pl.program_id(1)

    @pl.when(s < n_seq)
    def _():
        hists = ((buf_a, ha), (buf_b, hb), (buf_c, hc))

        @pl.when(t == 0)
        def _():
            for buf, h in hists:
                _hist_store(buf, 0, h, pitch, jnp.zeros((h, buf.shape[0] * LANES), F32))

        _mixer_rows(x_ref, p, bufs, (x1_ref, eidx_ref, wts_ref), rows=tile, stride=1, pitch=pitch,
                    new_a=ha, new_b=hb, new_c=hc,
                    pos_of_row=lambda r: r + t * tile,
                    alpha=alpha, n_groups=n_groups, n_experts=n_experts)

        tails = [_hist_load(buf, tile, h, pitch) for buf, h in hists]
        for (buf, h), tail in zip(hists, tails):
            _hist_store(buf, 0, h, pitch, tail)

        @pl.when(t == pl.num_programs(1) - 1)
        def _():
            for ref, (_, h), tail in zip((sa_ref, sb_ref, sc_ref), hists, tails):
                ref[...] = tail[h - ref.shape[0]:h, :]

    @pl.when(jnp.logical_and(s == n_seq, t < extra_tiles))
    def _():
        x1_ref[...] = jnp.zeros(x1_ref.shape, F32)


def _sample_mixer_kernel(x1_all_hbm, x_ref, sta_ref, stb_ref, stc_ref, *refs, n_params, batch, steps,
                         alpha, n_groups, n_experts):
    del x1_all_hbm
    p = refs[:n_params]
    x1_ref, eidx_ref, wts_ref, sa_ref, sb_ref, sc_ref = refs[n_params:n_params + 6]
    bufs = refs[n_params + 6:]
    _, buf_a, buf_b, buf_c, _, _ = bufs
    rows = batch * steps
    hists = []
    for buf, st_ref, out_ref in ((buf_a, sta_ref, sa_ref), (buf_b, stb_ref, sb_ref),
                                 (buf_c, stc_ref, sc_ref)):
        width = buf.shape[0] * LANES
        hists.append((buf, st_ref, out_ref, width, st_ref.shape[1] // width))

    for buf, st_ref, _, width, n in hists:
        for j in range(n):
            _hist_store(buf, j * batch, batch, 1, st_ref[:, j * width:(j + 1) * width])

    na, nb, nc = (h[4] for h in hists)
    _mixer_rows(x_ref, p, bufs, (x1_ref, eidx_ref, wts_ref), rows=rows, stride=batch, pitch=1,
                new_a=na * batch, new_b=nb * batch, new_c=nc * batch,
                pos_of_row=lambda r: PAST_LEN + r // batch,
                alpha=alpha, n_groups=n_groups, n_experts=n_experts)

    for buf, _, out_ref, width, n in hists:
        for j in range(n):
            out_ref[:, j * width:(j + 1) * width] = _hist_load(buf, (j + steps) * batch, batch, 1)


def _full_spec(a):
    nd = a.ndim
    return pl.BlockSpec(a.shape, lambda *_: (0,) * nd)


def _mixer_scratch(rows, ha, hb, hc, d_in, w_a, w_b, w_c, pitch):
    return [
        pltpu.VMEM((rows, d_in), F32),
        pltpu.VMEM((w_a // LANES, pitch * (ha + rows), LANES), F32),
        pltpu.VMEM((w_b // LANES, pitch * (hb + rows), LANES), F32),
        pltpu.VMEM((w_c // LANES, pitch * (hc + rows), LANES), F32),
        pltpu.VMEM((rows, w_b), BF16),
        pltpu.VMEM((rows, w_a + w_b + w_c), BF16),
    ]


def _round_up(n, m):
    return (n + m - 1) // m * m


def _prompt_mixer(x, b, t, extra_rows, params, dims, alpha):
    d = x.shape[1]
    w_a, w_b, w_c, ka, kb, kc, n_groups, n_experts = dims
    tile = PROMPT_TILE
    nt = t // tile
    ha, hb, hc = _round_up(ka - 1, SUBLANES), _round_up(kb, SUBLANES), _round_up(kc - 1, SUBLANES)
    d_in = params[0].shape[1]
    extra_tiles = extra_rows // tile
    assert extra_rows % tile == 0 and extra_tiles <= nt
    kern = functools.partial(_prompt_mixer_kernel, n_params=len(params), n_seq=b,
                             extra_tiles=extra_tiles, tile=tile, hist=(ha, hb, hc),
                             pitch=PROMPT_HIST_PITCH, alpha=alpha,
                             n_groups=n_groups, n_experts=n_experts)
    last_block = b * nt + extra_tiles - 1

    def seq_tile(i, j):
        return jnp.minimum(i, b - 1), jnp.where(i < b, j, nt - 1)

    def row_block(i, j):
        s, t_ = seq_tile(i, j)
        return s * nt + t_

    out_shape = (
        jax.ShapeDtypeStruct(((b * t + extra_rows) * SUBLANES, LANES), F32),
        jax.ShapeDtypeStruct((b, TOP_K, t), I32),
        jax.ShapeDtypeStruct((b, TOP_K, t), F32),
        jax.ShapeDtypeStruct((b, ka - 1, w_a), F32),
        jax.ShapeDtypeStruct((b, kb, w_b), F32),
        jax.ShapeDtypeStruct((b, kc - 1, w_c), F32),
    )
    out_specs = (
        pl.BlockSpec((tile * SUBLANES, LANES), lambda i, j: (jnp.minimum(i * nt + j, last_block), 0)),
        pl.BlockSpec((None, TOP_K, tile), lambda i, j: (seq_tile(i, j)[0], 0, seq_tile(i, j)[1])),
        pl.BlockSpec((None, TOP_K, tile), lambda i, j: (seq_tile(i, j)[0], 0, seq_tile(i, j)[1])),
        pl.BlockSpec((None, ka - 1, w_a), lambda i, j: (jnp.minimum(i, b - 1), 0, 0)),
        pl.BlockSpec((None, kb, w_b), lambda i, j: (jnp.minimum(i, b - 1), 0, 0)),
        pl.BlockSpec((None, kc - 1, w_c), lambda i, j: (jnp.minimum(i, b - 1), 0, 0)),
    )
    in_specs = [pl.BlockSpec((tile, d), lambda i, j: (row_block(i, j), 0))] + [_full_spec(a) for a in params]
    return pl.pallas_call(
        kern,
        grid=(b + 1, nt),
        in_specs=in_specs,
        out_specs=out_specs,
        out_shape=out_shape,
        scratch_shapes=_mixer_scratch(tile, ha, hb, hc, d_in, w_a, w_b, w_c, PROMPT_HIST_PITCH),
        compiler_params=pltpu.CompilerParams(
            dimension_semantics=("arbitrary", "arbitrary"),
            vmem_limit_bytes=VMEM_LIMIT_BYTES),
        name="prompt_mixer",
    )(x, *params)


def _sample_mixer(x1_all, x_tm, st_a, st_b, st_c, params, dims, alpha, batch, steps):
    rows, d = x_tm.shape
    w_a, w_b, w_c, ka, kb, kc, n_groups, n_experts = dims
    d_in = params[0].shape[1]
    first = x1_all.shape[0] // SUBLANES - rows
    assert first % rows == 0
    kern = functools.partial(_sample_mixer_kernel, n_params=len(params), batch=batch,
                             steps=steps, alpha=alpha, n_groups=n_groups, n_experts=n_experts)
    out_shape = (
        jax.ShapeDtypeStruct(x1_all.shape, F32),
        jax.ShapeDtypeStruct((TOP_K, rows), I32),
        jax.ShapeDtypeStruct((TOP_K, rows), F32),
        jax.ShapeDtypeStruct(st_a.shape, F32),
        jax.ShapeDtypeStruct(st_b.shape, F32),
        jax.ShapeDtypeStruct(st_c.shape, F32),
    )
    args = (x_tm, st_a, st_b, st_c) + tuple(params)
    out_specs = (pl.BlockSpec((rows * SUBLANES, LANES), lambda i: (first // rows, 0)),) + tuple(
        pl.BlockSpec(s.shape, lambda i, n=len(s.shape): (0,) * n) for s in out_shape[1:])
    return pl.pallas_call(
        kern,
        grid=(1,),
        in_specs=[pl.BlockSpec(memory_space=pl.ANY)] + [_full_spec(a) for a in args],
        out_specs=out_specs,
        out_shape=out_shape,
        input_output_aliases={0: 0},
        scratch_shapes=_mixer_scratch(rows, (ka - 1) * batch, kb * batch, (kc - 1) * batch,
                                      d_in, w_a, w_b, w_c, 1),
        compiler_params=pltpu.CompilerParams(
            dimension_semantics=("arbitrary",),
            vmem_limit_bytes=VMEM_LIMIT_BYTES),
        name="sample_mixer",
    )(x1_all, *args)


def _start_row_gather(src_hbm, idx_ref, dst_vmem, n, sem):
    def body(c, carry):
        for u in range(GATHER_UNROLL):
            s = c * GATHER_UNROLL + u
            row = idx_ref[0, 0, s]
            pltpu.make_async_copy(
                src_hbm.at[pl.ds(pl.multiple_of(row * SUBLANES, SUBLANES), SUBLANES)],
                dst_vmem.at[pl.ds(s * GATHER_PITCH, SUBLANES)],
                sem).start(priority=u % DMA_PRIORITIES)
        return carry
    lax.fori_loop(0, n // GATHER_UNROLL, body, 0)


def _wait_row_gather(src_hbm, dst_vmem, n, sem):
    pltpu.make_async_copy(src_hbm.at[pl.ds(0, n * SUBLANES)], dst_vmem.at[pl.ds(0, n * SUBLANES)],
                          sem).wait()


def _moe_kernel(te_ref, cnt_ref, nu_ref, src0_ref, src1_ref, x_hbm, wg_ref, wu_ref, wd_ref, y_ref,
                xbuf, wg_b, wu_b, wd_b, sem, *, tile, sub):
    i = pl.program_id(0)
    n_used = nu_ref[0]
    slot = i % 2
    nxt = jnp.minimum(i + 1, pl.num_programs(0) - 1)

    @pl.when(i == 0)
    def _():
        _start_row_gather(x_hbm, src0_ref, xbuf.at[0], cnt_ref[0] * sub, sem.at[0])

    @pl.when(i + 1 < n_used)
    def _():
        _start_row_gather(x_hbm, src1_ref, xbuf.at[1 - slot], cnt_ref[nxt] * sub, sem.at[1 - slot])

    @pl.when(i < n_used)
    def _():
        @pl.when(jnp.logical_or(i == 0, te_ref[i] != te_ref[jnp.maximum(i - 1, 0)]))
        def _():
            wg_b[...] = wg_ref[...].astype(BF16)
            wu_b[...] = wu_ref[...].astype(BF16)
            wd_b[...] = wd_ref[...].astype(BF16)

        _wait_row_gather(x_hbm, xbuf.at[slot], cnt_ref[i] * sub, sem.at[slot])

        for c in range(1, tile // sub + 1):
            @pl.when(cnt_ref[i] == c)
            def _(rows=c * sub):
                xb = _load_row_tiles(xbuf.at[slot], rows, pitch=GATHER_PITCH).astype(BF16)
                hg = _dot(xb, wg_b[...])
                hu = _dot(xb, wu_b[...])
                h = hg * _sigmoid(hg) * hu
                _store_row_tiles(y_ref, _dot(h.astype(BF16), wd_b[...]))
                if rows < tile:
                    y_ref[pl.ds(rows * SUBLANES, (tile - rows) * SUBLANES), :] = jnp.zeros(
                        ((tile - rows) * SUBLANES, LANES), F32)

    @pl.when(i >= n_used)
    def _():
        y_ref[...] = jnp.zeros(y_ref.shape, F32)


def _moe(x1, src, tile_expert, tile_subs, n_used, w_gate, w_up, w_down, layer):
    d = w_gate.shape[-2]
    tile = MOE_TILE
    n_tiles = src.shape[0] // tile
    d_e = w_gate.shape[-1]
    grid_spec = pltpu.PrefetchScalarGridSpec(
        num_scalar_prefetch=3,
        grid=(n_tiles,),
        in_specs=[
            pl.BlockSpec((1, 1, tile), lambda i, te, cnt, nu: (i, 0, 0), memory_space=pltpu.SMEM),
            pl.BlockSpec((1, 1, tile), lambda i, te, cnt, nu: (jnp.minimum(i + 1, n_tiles - 1), 0, 0),
                         memory_space=pltpu.SMEM),
            pl.BlockSpec(memory_space=pl.ANY),
            pl.BlockSpec((None, None, d, d_e), lambda i, te, cnt, nu: (layer, te[i], 0, 0)),
            pl.BlockSpec((None, None, d, d_e), lambda i, te, cnt, nu: (layer, te[i], 0, 0)),
            pl.BlockSpec((None, None, d_e, d), lambda i, te, cnt, nu: (layer, te[i], 0, 0)),
        ],
        out_specs=pl.BlockSpec((tile * SUBLANES, LANES), lambda i, te, cnt, nu: (i, 0)),
        scratch_shapes=[
            pltpu.VMEM((2, tile * GATHER_PITCH, LANES), F32),
            pltpu.VMEM((d, d_e), BF16),
            pltpu.VMEM((d, d_e), BF16),
            pltpu.VMEM((d_e, d), BF16),
            pltpu.SemaphoreType.DMA((2,)),
        ],
    )
    src3 = src.reshape(n_tiles, 1, tile)
    return pl.pallas_call(
        functools.partial(_moe_kernel, tile=tile, sub=MOE_SUB),
        grid_spec=grid_spec,
        out_shape=jax.ShapeDtypeStruct((n_tiles * tile * SUBLANES, LANES), F32),
        compiler_params=pltpu.CompilerParams(
            dimension_semantics=("arbitrary",),
            vmem_limit_bytes=VMEM_LIMIT_BYTES),
        name="moe",
    )(tile_expert, tile_subs, n_used, src3, src3, x1, w_gate, w_up, w_down)


def _combine_kernel(pos0_ref, pos1_ref, x1_ref, w_ref, g_ref, b_ref, y_hbm, op_ref, os_ref, ybuf, sem,
                    *, tile, alpha, prompt_tiles):
    i = pl.program_id(0)
    rows = TOP_K * tile
    slot = i % 2

    @pl.when(i == 0)
    def _():
        _start_row_gather(y_hbm, pos0_ref, ybuf.at[0], rows, sem.at[0])

    @pl.when(i + 1 < pl.num_programs(0))
    def _():
        _start_row_gather(y_hbm, pos1_ref, ybuf.at[1 - slot], rows, sem.at[1 - slot])

    _wait_row_gather(y_hbm, ybuf.at[slot], rows, sem.at[slot])
    w = w_ref[...]
    y_a = _load_row_tiles(ybuf.at[slot], tile, pitch=GATHER_PITCH)
    y_b = _load_row_tiles(ybuf.at[slot], tile, first=tile * GATHER_PITCH, pitch=GATHER_PITCH)
    moe = w[:, 0:1] * y_a + w[:, 1:2] * y_b
    x1 = _load_row_tiles(x1_ref, tile)
    out = _layer_norm(alpha * x1 + moe, g_ref[...], b_ref[...])

    @pl.when(i < prompt_tiles)
    def _():
        op_ref[...] = out

    @pl.when(i >= prompt_tiles)
    def _():
        os_ref[...] = out


def _combine(x1, y, pos, wts, g, b, alpha, n_prompt):
    n, d = x1.shape[0] // SUBLANES, g.shape[-1]
    tile = COMBINE_TILE
    n_tiles = n // tile
    prompt_tiles = n_prompt // tile
    assert n % tile == 0 and n_prompt % tile == 0 and 0 < prompt_tiles < n_tiles
    pos_t = pos.reshape(TOP_K, n_tiles, tile).transpose(1, 0, 2).reshape(n_tiles, 1, TOP_K * tile)
    return pl.pallas_call(
        functools.partial(_combine_kernel, tile=tile, alpha=alpha, prompt_tiles=prompt_tiles),
        grid=(n_tiles,),
        in_specs=[
            pl.BlockSpec((1, 1, TOP_K * tile), lambda i: (i, 0, 0), memory_space=pltpu.SMEM),
            pl.BlockSpec((1, 1, TOP_K * tile), lambda i: (jnp.minimum(i + 1, n_tiles - 1), 0, 0),
                         memory_space=pltpu.SMEM),
            pl.BlockSpec((tile * SUBLANES, LANES), lambda i: (i, 0)),
            pl.BlockSpec((tile, TOP_K), lambda i: (i, 0)),
            _full_spec(g),
            _full_spec(b),
            pl.BlockSpec(memory_space=pl.ANY),
        ],
        out_specs=(
            pl.BlockSpec((tile, d), lambda i: (jnp.minimum(i, prompt_tiles - 1), 0)),
            pl.BlockSpec((tile, d), lambda i: (jnp.maximum(i - prompt_tiles, 0), 0)),
        ),
        out_shape=(jax.ShapeDtypeStruct((n_prompt, d), F32),
                   jax.ShapeDtypeStruct((n - n_prompt, d), F32)),
        scratch_shapes=[pltpu.VMEM((2, TOP_K * tile * GATHER_PITCH, LANES), F32),
                        pltpu.SemaphoreType.DMA((2,))],
        compiler_params=pltpu.CompilerParams(
            dimension_semantics=("arbitrary",),
            vmem_limit_bytes=VMEM_LIMIT_BYTES),
        name="combine",
    )(pos_t, pos_t, x1, wts.T, g, b, y)


def _route_tables(eidx, n_experts, tile, sub):
    k, n = eidx.shape
    pairs = k * n
    n_tiles = pairs // tile + n_experts
    e = eidx.reshape(pairs)
    counts = jnp.sum((e[:, None] == jnp.arange(n_experts, dtype=I32)[None, :]).astype(I32), axis=0)
    ptiles = (counts + tile - 1) // tile
    tile_end = jnp.cumsum(ptiles)
    offs = (tile_end - ptiles) * tile
    subs_per_tile = tile // sub
    first_subs = (counts + sub - 1) // sub - subs_per_tile * (ptiles - 1)
    head = first_subs * sub
    skip = tile - head
    pos = _pair_positions(e, offs, head, skip)
    n_used = tile_end[-1]
    j = jnp.arange(n_tiles, dtype=I32)
    te = jnp.sum((j[:, None] >= tile_end[None, :]).astype(I32), axis=1)
    te_last = jnp.sum((n_used - 1 >= tile_end).astype(I32))
    te = jnp.where(j < n_used, te, te_last).astype(I32)
    tile_subs = jnp.where(j == (tile_end - ptiles)[te], first_subs[te], subs_per_tile)
    tile_subs = jnp.where(j < n_used, tile_subs, 0).astype(I32)
    src = _invert_positions(pos, n, n_tiles * tile)
    return pos.reshape(k, n), src, te, tile_subs, n_used.reshape(1).astype(I32)


def _positions_kernel(e_ref, tri_ref, low_ref, tab_ref, pos_ref, carry, *, n_experts):
    rows = e_ref.shape[0]

    @pl.when(pl.program_id(0) == 0)
    def _():
        carry[...] = jnp.zeros(carry.shape, F32)

    e = e_ref[...]
    masks = [e == k for k in range(n_experts)]
    stack = jnp.concatenate([m.astype(BF16) for m in masks], axis=0)
    cum = _dot(stack, tri_ref[...])
    tot = jnp.broadcast_to(cum[:, LANES - 1:LANES], cum.shape)
    before = _dot(low_ref[...], tot.astype(BF16))
    rank = jnp.zeros(e.shape, F32)
    off = jnp.zeros(e.shape, F32)
    head = jnp.zeros(e.shape, F32)
    skip = jnp.zeros(e.shape, F32)
    for k in range(n_experts):
        lo, hi = k * rows, (k + 1) * rows
        base = carry[k:k + 1, :]
        rank = jnp.where(masks[k], cum[lo:hi, :] + before[lo:hi, :] + base, rank)
        off = jnp.where(masks[k], tab_ref[0, k:k + 1, :], off)
        head = jnp.where(masks[k], tab_ref[1, k:k + 1, :], head)
        skip = jnp.where(masks[k], tab_ref[2, k:k + 1, :], skip)
        carry[k:k + 1, :] = base + before[hi - 1:hi, :] + tot[hi - 1:hi, :]
    rank = rank - 1.0
    pos_ref[...] = (off + rank + jnp.where(rank >= head, skip, 0.0)).astype(I32)


def _pair_positions(e, offs, head, skip):
    pairs = e.shape[0]
    n_experts = offs.shape[0]
    rows = ROUTE_ROWS
    assert pairs % (rows * LANES) == 0
    stacked = n_experts * rows
    tri = (jnp.arange(LANES)[:, None] <= jnp.arange(LANES)[None, :]).astype(BF16)
    r = jnp.arange(stacked)
    low = ((r[:, None] // rows == r[None, :] // rows) & (r[None, :] < r[:, None])).astype(BF16)
    tabs = jnp.broadcast_to(jnp.stack([offs, head, skip]).astype(F32)[:, :, None],
                            (3, n_experts, LANES))
    pos = pl.pallas_call(
        functools.partial(_positions_kernel, n_experts=n_experts),
        grid=(pairs // (rows * LANES),),
        in_specs=[pl.BlockSpec((rows, LANES), lambda i: (i, 0)),
                  _full_spec(tri), _full_spec(low), _full_spec(tabs)],
        out_specs=pl.BlockSpec((rows, LANES), lambda i: (i, 0)),
        out_shape=jax.ShapeDtypeStruct((pairs // LANES, LANES), I32),
        scratch_shapes=[pltpu.VMEM((n_experts, LANES), F32)],
        compiler_params=pltpu.CompilerParams(dimension_semantics=("arbitrary",)),
        name="pair_positions",
    )(e.reshape(pairs // LANES, LANES), tri, low, tabs)
    return pos.reshape(pairs)


def _invert_kernel(pos_ref, src_ref, zeros_ref, *, chunk, n):
    i = pl.program_id(0)

    @pl.when(i == 0)
    def _():
        zeros_ref[...] = jnp.zeros(zeros_ref.shape, I32)
        pltpu.sync_copy(zeros_ref, src_ref)

    first_tok = (i * chunk) % n

    def body(c, carry):
        for u in range(INVERT_UNROLL):
            s = c * INVERT_UNROLL + u
            dst = pos_ref[0, 0, s]
            src_ref[dst] = first_tok + s
        return carry
    lax.fori_loop(0, chunk // INVERT_UNROLL, body, 0)


def _invert_positions(pos, n, rows):
    pairs = pos.shape[0]
    chunk = INVERT_CHUNK
    assert pairs % chunk == 0 and n % chunk == 0 and rows % LANES == 0
    src = pl.pallas_call(
        functools.partial(_invert_kernel, chunk=chunk, n=n),
        grid=(pairs // chunk,),
        in_specs=[pl.BlockSpec((1, 1, chunk), lambda i: (i, 0, 0), memory_space=pltpu.SMEM)],
        out_specs=pl.BlockSpec(memory_space=pltpu.SMEM),
        out_shape=jax.ShapeDtypeStruct((rows,), I32),
        scratch_shapes=[pltpu.VMEM((rows,), I32)],
        compiler_params=pltpu.CompilerParams(dimension_semantics=("arbitrary",)),
        name="invert_positions",
    )(pos.reshape(pairs // chunk, 1, chunk))
    return src


def _layer_params(l, w_in_b, b_in, conv_a_w, conv_a_b, ln_a_g, ln_a_b, w_pool_bd, pool_scale,
                  conv_c_w, w_out_b, b_out, ln1_g, ln1_b, w_router, b_router):
    row = lambda a: a[l][None, :]
    return (w_in_b[l], row(b_in), conv_a_w[l], row(conv_a_b), row(ln_a_g), row(ln_a_b),
            w_pool_bd[l], row(pool_scale), conv_c_w[l], w_out_b[l], row(b_out), row(ln1_g),
            row(ln1_b), w_router[l], b_router[l])


def kernel(x_prompt, x_sample, state_conv_a, state_pool_b, state_conv_c, w_in, b_in, conv_a_w, conv_a_b, ln_a_g, ln_a_b, w_pool, pool_scale, conv_c_w, w_out, b_out, ln1_g, ln1_b, w_router_group, b_router_group, w_router_expert, b_router_expert, w_gate, w_up, w_down, ln2_g, ln2_b):
    depth = w_in.shape[0]
    bp, tp, d = x_prompt.shape
    bs, ts, _ = x_sample.shape
    ka, w_a = conv_a_w.shape[1:]
    kb, w_b = state_pool_b.shape[2:]
    kc, w_c = conv_c_w.shape[1:]
    n_groups = w_router_group.shape[-1]
    n_experts = w_router_expert.shape[-1]
    dims = (w_a, w_b, w_c, ka, kb, kc, n_groups, n_experts)
    alpha = float((2 * depth) ** 0.25)
    assert tp % PROMPT_TILE == 0 and PROMPT_TILE % ROW_CHUNK == 0 and (bs * ts) % ROW_CHUNK == 0
    assert n_groups <= SUBLANES and ROUTER_EXPERT_ROW + n_experts <= LANES
    assert d == SUBLANES * LANES

    w_in_b = w_in.astype(BF16)
    w_out_b = w_out.astype(BF16)
    n_pg, pg = w_pool.shape[1], w_pool.shape[2]
    eye = jnp.eye(n_pg, dtype=F32)
    w_pool_bd = (w_pool[:, :, :, None, :] * eye[None, :, None, :, None]).reshape(depth, n_pg * pg, n_pg * pg).astype(BF16)

    def router_lanes(group_part, expert_part):
        lead = group_part.shape[:-1]
        gap = jnp.zeros(lead + (ROUTER_EXPERT_ROW - ROUTER_GROUP_ROW - n_groups,), F32)
        tail = jnp.zeros(lead + (LANES - ROUTER_EXPERT_ROW - n_experts,), F32)
        return jnp.concatenate([group_part, gap, expert_part, tail], axis=-1)

    assert ROUTER_GROUP_ROW == 0
    wr = router_lanes(w_router_group, w_router_expert)
    wr_hi = wr.astype(BF16)
    wr_lo = (wr - wr_hi.astype(F32)).astype(BF16)
    w_router = jnp.concatenate([wr_hi, wr_lo], axis=-1)
    b_router = router_lanes(b_router_group, b_router_expert)[:, None, :]

    xp = x_prompt.reshape(bp * tp, d)
    xs = x_sample.transpose(1, 0, 2).reshape(ts * bs, d)
    new_states = [[] for _ in range(6)]
    for l in range(depth):
        params = _layer_params(l, w_in_b, b_in, conv_a_w, conv_a_b, ln_a_g, ln_a_b, w_pool_bd,
                               pool_scale, conv_c_w, w_out_b, b_out, ln1_g, ln1_b, w_router, b_router)
        g2, b2 = ln2_g[l][None, :], ln2_b[l][None, :]

        x1, eidx, wts, pa, pb, pc = _prompt_mixer(xp, bp, tp, bs * ts, params, dims, alpha)
        x1, seidx, swts, sa, sb, sc = _sample_mixer(
            x1, xs, state_conv_a[l].reshape(bs, -1), state_pool_b[l].reshape(bs, -1),
            state_conv_c[l].reshape(bs, -1), params, dims, alpha, bs, ts)
        eidx = jnp.concatenate([eidx.transpose(1, 0, 2).reshape(TOP_K, bp * tp), seidx], axis=1)
        wts = jnp.concatenate([wts.transpose(1, 0, 2).reshape(TOP_K, bp * tp), swts], axis=1)

        pos, src, te, subs, nu = _route_tables(eidx, n_experts, MOE_TILE, MOE_SUB)
        y = _moe(x1, src, te, subs, nu, w_gate, w_up, w_down, l)
        xp, xs = _combine(x1, y, pos, wts, g2, b2, alpha, bp * tp)

        for lst, val in zip(new_states, (pa, sa.reshape(bs, ka - 1, w_a), pb, sb.reshape(bs, kb, w_b),
                                         pc, sc.reshape(bs, kc - 1, w_c))):
            lst.append(val)

    y_prompt = xp.reshape(bp, tp, d)
    y_sample = xs.reshape(ts, bs, d).transpose(1, 0, 2)
    return (y_prompt, y_sample) + tuple(jnp.stack(s) for s in new_states)
```

```python
import functools

import jax
import jax.numpy as jnp
from jax import lax
from jax.experimental import pallas as pl
from jax.experimental.pallas import tpu as pltpu

F32 = jnp.float32
BF16 = jnp.bfloat16
I32 = jnp.int32

POOL_WINDOWS = (2, 4, 8, 16)
TOP_K = 2
LN_EPS = 1e-5
PAST_LEN = 16384

LANES = 128
SUBLANES = 8
VMEM_LIMIT_BYTES = 56 * 1024 * 1024

ROW_CHUNK = 128
PROMPT_TILE = 512
PROMPT_HIST_PITCH = 2
MOE_TILE = 1024
MOE_SUB = 256
COMBINE_TILE = 256
GATHER_UNROLL = 32
DMA_PRIORITIES = 2
INVERT_CHUNK = 4224
INVERT_UNROLL = 16
ROUTE_ROWS = 24
GATHER_PITCH = 9
ROUTER_GROUP_ROW = 0
ROUTER_EXPERT_ROW = 8


def _dot(a, b):
    return jnp.dot(a, b, preferred_element_type=F32)


def _layer_norm(x, g, b):
    mu = jnp.mean(x, axis=-1, keepdims=True)
    xc = x - mu
    var = jnp.mean(xc * xc, axis=-1, keepdims=True)
    return xc * lax.rsqrt(var + LN_EPS) * g + b


def _sigmoid(x):
    return 1.0 / (1.0 + jnp.exp(-x))


def _store_row_tiles(ref, val):
    rows, d = val.shape
    for j in range(d // LANES):
        ref[pl.ds(j, rows, stride=SUBLANES), :] = val[:, j * LANES:(j + 1) * LANES]


def _load_row_tiles(ref, rows, first=0, pitch=SUBLANES):
    return jnp.concatenate(
        [ref[pl.ds(first + j, rows, stride=pitch), :] for j in range(SUBLANES)], axis=-1)


def _hist_load(buf, start, n, pitch):
    parts = []
    for j in range(buf.shape[0]):
        if pitch == 1:
            parts.append(buf[j, pl.ds(start, n), :])
        else:
            parts.append(buf.at[j][pl.ds(pitch * start, n, stride=pitch), :])
    return parts[0] if len(parts) == 1 else jnp.concatenate(parts, axis=-1)


def _hist_store(buf, start, n, pitch, val):
    for j in range(buf.shape[0]):
        piece = val[:, j * LANES:(j + 1) * LANES]
        if pitch == 1:
            buf[j, pl.ds(start, n), :] = piece
        else:
            buf.at[j][pl.ds(pitch * start, n, stride=pitch), :] = piece


def _mixer_rows(x_ref, p, bufs, outs, *, rows, stride, pitch, new_a, new_b, new_c, pos_of_row,
                alpha, n_groups, n_experts):
    (w_in, b_in, wa, ba, lnag, lnab, wbd, pscale, wc, w_out, b_out, ln1g, ln1b,
     wr, br) = p
    proj, buf_a, buf_b, buf_c, dbuf, cat = bufs
    x1_ref, eidx_ref, wts_ref = outs
    ka = wa.shape[0]
    kc = wc.shape[0]
    w_a = wa.shape[1]
    w_b = pscale.shape[1]
    w_c = wc.shape[1]
    o_gate, o_ub, o_cbg, o_ccg, o_ch = w_a, 2 * w_a, 2 * w_a + w_b, 2 * w_a + w_b + w_c, 2 * w_a + w_b + 2 * w_c

    x = x_ref[...]
    proj[...] = _dot(x.astype(BF16), w_in[...]) + b_in[...]

    _hist_store(buf_a, new_a, rows, pitch, proj[:, 0:w_a] * _sigmoid(proj[:, o_gate:o_gate + w_a]))
    _hist_store(buf_b, new_b, rows, pitch, proj[:, o_ub:o_ub + w_b])
    _hist_store(buf_c, new_c, rows, pitch, proj[:, o_ccg:o_ccg + w_c] * proj[:, o_ch:o_ch + w_c])

    lane_b = lax.broadcasted_iota(I32, (ROW_CHUNK, w_b), 1)
    group_b = lane_b // (w_b // len(POOL_WINDOWS))
    win_b = jnp.zeros((ROW_CHUNK, w_b), I32)
    for g, w in enumerate(POOL_WINDOWS):
        win_b = jnp.where(group_b == g, w, win_b)

    def chunk(c, carry):
        r0 = c * ROW_CHUNK if isinstance(c, int) else pl.multiple_of(c * ROW_CHUNK, ROW_CHUNK)
        acc = jnp.zeros((ROW_CHUNK, w_a), F32) + ba[...]
        for k in range(ka):
            src = _hist_load(buf_a, r0 + (new_a - (ka - 1 - k) * stride), ROW_CHUNK, pitch)
            acc = acc + wa[k:k + 1, :] * src
        a = _layer_norm(acc, lnag[...], lnab[...])
        cat[pl.ds(r0, ROW_CHUNK), 0:w_a] = (a * _sigmoid(a)).astype(BF16)
        cur = _hist_load(buf_b, r0 + new_b, ROW_CHUNK, pitch)
        run = cur
        wsum = jnp.zeros((ROW_CHUNK, w_b), F32)
        for j in range(1, max(POOL_WINDOWS) + 1):
            if j in POOL_WINDOWS:
                wsum = jnp.where(win_b == j, run, wsum)
            if j < max(POOL_WINDOWS):
                run = run + _hist_load(buf_b, r0 + (new_b - j * stride), ROW_CHUNK, pitch)
        pos = pos_of_row(r0 + lax.broadcasted_iota(I32, (ROW_CHUNK, w_b), 0))
        cnt = jnp.minimum(pos + 1, win_b).astype(F32)
        dbuf[pl.ds(r0, ROW_CHUNK), :] = (wsum / cnt - cur).astype(BF16)
        accc = jnp.zeros((ROW_CHUNK, w_c), F32)
        for k in range(kc):
            src = _hist_load(buf_c, r0 + (new_c - (kc - 1 - k) * stride), ROW_CHUNK, pitch)
            accc = accc + wc[k:k + 1, :] * src
        cbg = proj[pl.ds(r0, ROW_CHUNK), o_cbg:o_cbg + w_c]
        cat[pl.ds(r0, ROW_CHUNK), w_a + w_b:w_a + w_b + w_c] = (cbg * accc).astype(BF16)
        return carry

    if pitch == 1:
        assert stride % SUBLANES == 0
        lax.fori_loop(0, rows // ROW_CHUNK, chunk, 0)
    else:
        for c in range(rows // ROW_CHUNK):
            chunk(c, 0)

    cat[:, w_a:w_a + w_b] = (_dot(dbuf[...], wbd[...]) * pscale[...]).astype(BF16)

    m = _dot(cat[...], w_out[...]) + b_out[...]
    x1 = _layer_norm(alpha * x + m, ln1g[...], ln1b[...])
    _store_row_tiles(x1_ref, x1)

    x1_hi = x1.astype(BF16)
    x1_lo = (x1 - x1_hi.astype(F32)).astype(BF16)
    p_hi = _dot(x1_hi, wr[...])
    p_lo = _dot(x1_lo, wr[...])
    logits = (p_hi[:, 0:LANES] + p_hi[:, LANES:2 * LANES]) + (p_lo[:, 0:LANES] + p_lo[:, LANES:2 * LANES])
    lt = jnp.transpose(logits + br[...])

    neg = jnp.float32(-jnp.inf)
    gl = lt[ROUTER_GROUP_ROW:ROUTER_GROUP_ROW + SUBLANES, :]
    grow = lax.broadcasted_iota(I32, gl.shape, 0)
    gvalid = grow < n_groups
    glm = jnp.where(gvalid, gl, neg)
    gmax = jnp.max(glm, axis=0, keepdims=True)
    gidx = jnp.min(jnp.where(glm == gmax, grow, SUBLANES), axis=0, keepdims=True)
    gsum = jnp.sum(jnp.where(gvalid, jnp.exp(gl - gmax), 0.0), axis=0, keepdims=True)
    g_p = 1.0 / gsum

    el = lt[ROUTER_EXPERT_ROW:ROUTER_EXPERT_ROW + n_experts, :]
    erow = lax.broadcasted_iota(I32, el.shape, 0)
    v = jnp.where(erow // (n_experts // n_groups) == gidx, el, neg)
    v1 = jnp.max(v, axis=0, keepdims=True)
    i1 = jnp.min(jnp.where(v == v1, erow, n_experts), axis=0, keepdims=True)
    vv = jnp.where(erow == i1, neg, v)
    v2 = jnp.max(vv, axis=0, keepdims=True)
    i2 = jnp.min(jnp.where(vv == v2, erow, n_experts), axis=0, keepdims=True)
    e2 = jnp.exp(v2 - v1)
    den = 1.0 + e2
    eidx_ref[0:1, :] = i1
    eidx_ref[1:2, :] = i2
    wts_ref[0:1, :] = (1.0 / den) * g_p
    wts_ref[1:2, :] = (e2 / den) * g_p


def _prompt_mixer_kernel(x_ref, *refs, n_params, n_seq, extra_tiles, tile, hist, pitch, alpha,
                         n_groups, n_experts):
    p = refs[:n_params]
    x1_ref, eidx_ref, wts_ref, sa_ref, sb_ref, sc_ref = refs[n_params:n_params + 6]
    bufs = refs[n_params + 6:]
    _, buf_a, buf_b, buf_c, _, _ = bufs
    ha, hb, hc = hist
    s = pl.program_id(0)
    t = pl.program_id(1)

    @pl.when(s < n_seq)
    def _():
        hists = ((buf_a, ha), (buf_b, hb), (buf_c, hc))

        @pl.when(t == 0)
        def _():
            for buf, h in hists:
                _hist_store(buf, 0, h, pitch, jnp.zeros((h, buf.shape[0] * LANES), F32))

        _mixer_rows(x_ref, p, bufs, (x1_ref, eidx_ref, wts_ref), rows=tile, stride=1, pitch=pitch,
                    new_a=ha, new_b=hb, new_c=hc,
                    pos_of_row=lambda r: r + t * tile,
                    alpha=alpha, n_groups=n_groups, n_experts=n_experts)

        tails = [_hist_load(buf, tile, h, pitch) for buf, h in hists]
        for (buf, h), tail in zip(hists, tails):
            _hist_store(buf, 0, h, pitch, tail)

        @pl.when(t == pl.num_programs(1) - 1)
        def _():
            for ref, (_, h), tail in zip((sa_ref, sb_ref, sc_ref), hists, tails):
                ref[...] = tail[h - ref.shape[0]:h, :]

    @pl.when(jnp.logical_and(s == n_seq, t < extra_tiles))
    def _():
        x1_ref[...] = jnp.zeros(x1_ref.shape, F32)


def _sample_mixer_kernel(x1_all_hbm, x_ref, sta_ref, stb_ref, stc_ref, *refs, n_params, batch, steps,
                         alpha, n_groups, n_experts):
    del x1_all_hbm
    p = refs[:n_params]
    x1_ref, eidx_ref, wts_ref, sa_ref, sb_ref, sc_ref = refs[n_params:n_params + 6]
    bufs = refs[n_params + 6:]
    _, buf_a, buf_b, buf_c, _, _ = bufs
    rows = batch * steps
    hists = []
    for buf, st_ref, out_ref in ((buf_a, sta_ref, sa_ref), (buf_b, stb_ref, sb_ref),
                                 (buf_c, stc_ref, sc_ref)):
        width = buf.shape[0] * LANES
        hists.append((buf, st_ref, out_ref, width, st_ref.shape[1] // width))

    for buf, st_ref, _, width, n in hists:
        for j in range(n):
            _hist_store(buf, j * batch, batch, 1, st_ref[:, j * width:(j + 1) * width])

    na, nb, nc = (h[4] for h in hists)
    _mixer_rows(x_ref, p, bufs, (x1_ref, eidx_ref, wts_ref), rows=rows, stride=batch, pitch=1,
                new_a=na * batch, new_b=nb * batch, new_c=nc * batch,
                pos_of_row=lambda r: PAST_LEN + r // batch,
                alpha=alpha, n_groups=n_groups, n_experts=n_experts)

    for buf, _, out_ref, width, n in hists:
        for j in range(n):
            out_ref[:, j * width:(j + 1) * width] = _hist_load(buf, (j + steps) * batch, batch, 1)


def _full_spec(a):
    nd = a.ndim
    return pl.BlockSpec(a.shape, lambda *_: (0,) * nd)


def _mixer_scratch(rows, ha, hb, hc, d_in, w_a, w_b, w_c, pitch):
    return [
        pltpu.VMEM((rows, d_in), F32),
        pltpu.VMEM((w_a // LANES, pitch * (ha + rows), LANES), F32),
        pltpu.VMEM((w_b // LANES, pitch * (hb + rows), LANES), F32),
        pltpu.VMEM((w_c // LANES, pitch * (hc + rows), LANES), F32),
        pltpu.VMEM((rows, w_b), BF16),
        pltpu.VMEM((rows, w_a + w_b + w_c), BF16),
    ]


def _round_up(n, m):
    return (n + m - 1) // m * m


def _prompt_mixer(x, b, t, extra_rows, params, dims, alpha):
    d = x.shape[1]
    w_a, w_b, w_c, ka, kb, kc, n_groups, n_experts = dims
    tile = PROMPT_TILE
    nt = t // tile
    ha, hb, hc = _round_up(ka - 1, SUBLANES), _round_up(kb, SUBLANES), _round_up(kc - 1, SUBLANES)
    d_in = params[0].shape[1]
    extra_tiles = extra_rows // tile
    assert extra_rows % tile == 0 and extra_tiles <= nt
    kern = functools.partial(_prompt_mixer_kernel, n_params=len(params), n_seq=b,
                             extra_tiles=extra_tiles, tile=tile, hist=(ha, hb, hc),
                             pitch=PROMPT_HIST_PITCH, alpha=alpha,
                             n_groups=n_groups, n_experts=n_experts)
    last_block = b * nt + extra_tiles - 1

    def seq_tile(i, j):
        return jnp.minimum(i, b - 1), jnp.where(i < b, j, nt - 1)

    def row_block(i, j):
        s, t_ = seq_tile(i, j)
        return s * nt + t_

    out_shape = (
        jax.ShapeDtypeStruct(((b * t + extra_rows) * SUBLANES, LANES), F32),
        jax.ShapeDtypeStruct((b, TOP_K, t), I32),
        jax.ShapeDtypeStruct((b, TOP_K, t), F32),
        jax.ShapeDtypeStruct((b, ka - 1, w_a), F32),
        jax.ShapeDtypeStruct((b, kb, w_b), F32),
        jax.ShapeDtypeStruct((b, kc - 1, w_c), F32),
    )
    out_specs = (
        pl.BlockSpec((tile * SUBLANES, LANES), lambda i, j: (jnp.minimum(i * nt + j, last_block), 0)),
        pl.BlockSpec((None, TOP_K, tile), lambda i, j: (seq_tile(i, j)[0], 0, seq_tile(i, j)[1])),
        pl.BlockSpec((None, TOP_K, tile), lambda i, j: (seq_tile(i, j)[0], 0, seq_tile(i, j)[1])),
        pl.BlockSpec((None, ka - 1, w_a), lambda i, j: (jnp.minimum(i, b - 1), 0, 0)),
        pl.BlockSpec((None, kb, w_b), lambda i, j: (jnp.minimum(i, b - 1), 0, 0)),
        pl.BlockSpec((None, kc - 1, w_c), lambda i, j: (jnp.minimum(i, b - 1), 0, 0)),
    )
    in_specs = [pl.BlockSpec((tile, d), lambda i, j: (row_block(i, j), 0))] + [_full_spec(a) for a in params]
    return pl.pallas_call(
        kern,
        grid=(b + 1, nt),
        in_specs=in_specs,
        out_specs=out_specs,
        out_shape=out_shape,
        scratch_shapes=_mixer_scratch(tile, ha, hb, hc, d_in, w_a, w_b, w_c, PROMPT_HIST_PITCH),
        compiler_params=pltpu.CompilerParams(
            dimension_semantics=("arbitrary", "arbitrary"),
            vmem_limit_bytes=VMEM_LIMIT_BYTES),
        name="prompt_mixer",
    )(x, *params)


def _sample_mixer(x1_all, x_tm, st_a, st_b, st_c, params, dims, alpha, batch, steps):
    rows, d = x_tm.shape
    w_a, w_b, w_c, ka, kb, kc, n_groups, n_experts = dims
    d_in = params[0].shape[1]
    first = x1_all.shape[0] // SUBLANES - rows
    assert first % rows == 0
    kern = functools.partial(_sample_mixer_kernel, n_params=len(params), batch=batch,
                             steps=steps, alpha=alpha, n_groups=n_groups, n_experts=n_experts)
    out_shape = (
        jax.ShapeDtypeStruct(x1_all.shape, F32),
        jax.ShapeDtypeStruct((TOP_K, rows), I32),
        jax.ShapeDtypeStruct((TOP_K, rows), F32),
        jax.ShapeDtypeStruct(st_a.shape, F32),
        jax.ShapeDtypeStruct(st_b.shape, F32),
        jax.ShapeDtypeStruct(st_c.shape, F32),
    )
    args = (x_tm, st_a, st_b, st_c) + tuple(params)
    out_specs = (pl.BlockSpec((rows * SUBLANES, LANES), lambda i: (first // rows, 0)),) + tuple(
        pl.BlockSpec(s.shape, lambda i, n=len(s.shape): (0,) * n) for s in out_shape[1:])
    return pl.pallas_call(
        kern,
        grid=(1,),
        in_specs=[pl.BlockSpec(memory_space=pl.ANY)] + [_full_spec(a) for a in args],
        out_specs=out_specs,
        out_shape=out_shape,
        input_output_aliases={0: 0},
        scratch_shapes=_mixer_scratch(rows, (ka - 1) * batch, kb * batch, (kc - 1) * batch,
                                      d_in, w_a, w_b, w_c, 1),
        compiler_params=pltpu.CompilerParams(
            dimension_semantics=("arbitrary",),
            vmem_limit_bytes=VMEM_LIMIT_BYTES),
        name="sample_mixer",
    )(x1_all, *args)


def _start_row_gather(src_hbm, idx_ref, dst_vmem, n, sem):
    def body(c, carry):
        for u in range(GATHER_UNROLL):
            s = c * GATHER_UNROLL + u
            row = idx_ref[0, 0, s]
            pltpu.make_async_copy(
                src_hbm.at[pl.ds(pl.multiple_of(row * SUBLANES, SUBLANES), SUBLANES)],
                dst_vmem.at[pl.ds(s * GATHER_PITCH, SUBLANES)],
                sem).start(priority=u % DMA_PRIORITIES)
        return carry
    lax.fori_loop(0, n // GATHER_UNROLL, body, 0)


def _wait_row_gather(src_hbm, dst_vmem, n, sem):
    pltpu.make_async_copy(src_hbm.at[pl.ds(0, n * SUBLANES)], dst_vmem.at[pl.ds(0, n * SUBLANES)],
                          sem).wait()


def _moe_kernel(te_ref, cnt_ref, nu_ref, src0_ref, src1_ref, x_hbm, wg_ref, wu_ref, wd_ref, y_ref,
                xbuf, wg_b, wu_b, wd_b, sem, *, tile, sub):
    i = pl.program_id(0)
    n_used = nu_ref[0]
    slot = i % 2
    nxt = jnp.minimum(i + 1, pl.num_programs(0) - 1)

    @pl.when(i == 0)
    def _():
        _start_row_gather(x_hbm, src0_ref, xbuf.at[0], cnt_ref[0] * sub, sem.at[0])

    @pl.when(i + 1 < n_used)
    def _():
        _start_row_gather(x_hbm, src1_ref, xbuf.at[1 - slot], cnt_ref[nxt] * sub, sem.at[1 - slot])

    @pl.when(i < n_used)
    def _():
        @pl.when(jnp.logical_or(i == 0, te_ref[i] != te_ref[jnp.maximum(i - 1, 0)]))
        def _():
            wg_b[...] = wg_ref[...].astype(BF16)
            wu_b[...] = wu_ref[...].astype(BF16)
            wd_b[...] = wd_ref[...].astype(BF16)

        _wait_row_gather(x_hbm, xbuf.at[slot], cnt_ref[i] * sub, sem.at[slot])

        for c in range(1, tile // sub + 1):
            @pl.when(cnt_ref[i] == c)
            def _(rows=c * sub):
                xb = _load_row_tiles(xbuf.at[slot], rows, pitch=GATHER_PITCH).astype(BF16)
                hg = _dot(xb, wg_b[...])
                hu = _dot(xb, wu_b[...])
                h = hg * _sigmoid(hg) * hu
                _store_row_tiles(y_ref, _dot(h.astype(BF16), wd_b[...]))
                if rows < tile:
                    y_ref[pl.ds(rows * SUBLANES, (tile - rows) * SUBLANES), :] = jnp.zeros(
                        ((tile - rows) * SUBLANES, LANES), F32)

    @pl.when(i >= n_used)
    def _():
        y_ref[...] = jnp.zeros(y_ref.shape, F32)


def _moe(x1, src, tile_expert, tile_subs, n_used, w_gate, w_up, w_down, layer):
    d = w_gate.shape[-2]
    tile = MOE_TILE
    n_tiles = src.shape[0] // tile
    d_e = w_gate.shape[-1]
    grid_spec = pltpu.PrefetchScalarGridSpec(
        num_scalar_prefetch=3,
        grid=(n_tiles,),
        in_specs=[
            pl.BlockSpec((1, 1, tile), lambda i, te, cnt, nu: (i, 0, 0), memory_space=pltpu.SMEM),
            pl.BlockSpec((1, 1, tile), lambda i, te, cnt, nu: (jnp.minimum(i + 1, n_tiles - 1), 0, 0),
                         memory_space=pltpu.SMEM),
            pl.BlockSpec(memory_space=pl.ANY),
            pl.BlockSpec((None, None, d, d_e), lambda i, te, cnt, nu: (layer, te[i], 0, 0)),
            pl.BlockSpec((None, None, d, d_e), lambda i, te, cnt, nu: (layer, te[i], 0, 0)),
            pl.BlockSpec((None, None, d_e, d), lambda i, te, cnt, nu: (layer, te[i], 0, 0)),
        ],
        out_specs=pl.BlockSpec((tile * SUBLANES, LANES), lambda i, te, cnt, nu: (i, 0)),
        scratch_shapes=[
            pltpu.VMEM((2, tile * GATHER_PITCH, LANES), F32),
            pltpu.VMEM((d, d_e), BF16),
            pltpu.VMEM((d, d_e), BF16),
            pltpu.VMEM((d_e, d), BF16),
            pltpu.SemaphoreType.DMA((2,)),
        ],
    )
    src3 = src.reshape(n_tiles, 1, tile)
    return pl.pallas_call(
        functools.partial(_moe_kernel, tile=tile, sub=MOE_SUB),
        grid_spec=grid_spec,
        out_shape=jax.ShapeDtypeStruct((n_tiles * tile * SUBLANES, LANES), F32),
        compiler_params=pltpu.CompilerParams(
            dimension_semantics=("arbitrary",),
            vmem_limit_bytes=VMEM_LIMIT_BYTES),
        name="moe",
    )(tile_expert, tile_subs, n_used, src3, src3, x1, w_gate, w_up, w_down)


def _combine_kernel(pos0_ref, pos1_ref, x1_ref, w_ref, g_ref, b_ref, y_hbm, op_ref, os_ref, ybuf, sem,
                    *, tile, alpha, prompt_tiles):
    i = pl.program_id(0)
    rows = TOP_K * tile
    slot = i % 2

    @pl.when(i == 0)
    def _():
        _start_row_gather(y_hbm, pos0_ref, ybuf.at[0], rows, sem.at[0])

    @pl.when(i + 1 < pl.num_programs(0))
    def _():
        _start_row_gather(y_hbm, pos1_ref, ybuf.at[1 - slot], rows, sem.at[1 - slot])

    _wait_row_gather(y_hbm, ybuf.at[slot], rows, sem.at[slot])
    w = w_ref[...]
    y_a = _load_row_tiles(ybuf.at[slot], tile, pitch=GATHER_PITCH)
    y_b = _load_row_tiles(ybuf.at[slot], tile, first=tile * GATHER_PITCH, pitch=GATHER_PITCH)
    moe = w[:, 0:1] * y_a + w[:, 1:2] * y_b
    x1 = _load_row_tiles(x1_ref, tile)
    out = _layer_norm(alpha * x1 + moe, g_ref[...], b_ref[...])

    @pl.when(i < prompt_tiles)
    def _():
        op_ref[...] = out

    @pl.when(i >= prompt_tiles)
    def _():
        os_ref[...] = out


def _combine(x1, y, pos, wts, g, b, alpha, n_prompt):
    n, d = x1.shape[0] // SUBLANES, g.shape[-1]
    tile = COMBINE_TILE
    n_tiles = n // tile
    prompt_tiles = n_prompt // tile
    assert n % tile == 0 and n_prompt % tile == 0 and 0 < prompt_tiles < n_tiles
    pos_t = pos.reshape(TOP_K, n_tiles, tile).transpose(1, 0, 2).reshape(n_tiles, 1, TOP_K * tile)
    return pl.pallas_call(
        functools.partial(_combine_kernel, tile=tile, alpha=alpha, prompt_tiles=prompt_tiles),
        grid=(n_tiles,),
        in_specs=[
            pl.BlockSpec((1, 1, TOP_K * tile), lambda i: (i, 0, 0), memory_space=pltpu.SMEM),
            pl.BlockSpec((1, 1, TOP_K * tile), lambda i: (jnp.minimum(i + 1, n_tiles - 1), 0, 0),
                         memory_space=pltpu.SMEM),
            pl.BlockSpec((tile * SUBLANES, LANES), lambda i: (i, 0)),
            pl.BlockSpec((tile, TOP_K), lambda i: (i, 0)),
            _full_spec(g),
            _full_spec(b),
            pl.BlockSpec(memory_space=pl.ANY),
        ],
        out_specs=(
            pl.BlockSpec((tile, d), lambda i: (jnp.minimum(i, prompt_tiles - 1), 0)),
            pl.BlockSpec((tile, d), lambda i: (jnp.maximum(i - prompt_tiles, 0), 0)),
        ),
        out_shape=(jax.ShapeDtypeStruct((n_prompt, d), F32),
                   jax.ShapeDtypeStruct((n - n_prompt, d), F32)),
        scratch_shapes=[pltpu.VMEM((2, TOP_K * tile * GATHER_PITCH, LANES), F32),
                        pltpu.SemaphoreType.DMA((2,))],
        compiler_params=pltpu.CompilerParams(
            dimension_semantics=("arbitrary",),
            vmem_limit_bytes=VMEM_LIMIT_BYTES),
        name="combine",
    )(pos_t, pos_t, x1, wts.T, g, b, y)


def _route_tables(eidx, n_experts, tile, sub):
    k, n = eidx.shape
    pairs = k * n
    n_tiles = pairs // tile + n_experts
    e = eidx.reshape(pairs)
    counts = jnp.sum((e[:, None] == jnp.arange(n_experts, dtype=I32)[None, :]).astype(I32), axis=0)
    ptiles = (counts + tile - 1) // tile
    tile_end = jnp.cumsum(ptiles)
    offs = (tile_end - ptiles) * tile
    subs_per_tile = tile // sub
    first_subs = (counts + sub - 1) // sub - subs_per_tile * (ptiles - 1)
    head = first_subs * sub
    skip = tile - head
    pos = _pair_positions(e, offs, head, skip)
    n_used = tile_end[-1]
    j = jnp.arange(n_tiles, dtype=I32)
    te = jnp.sum((j[:, None] >= tile_end[None, :]).astype(I32), axis=1)
    te_last = jnp.sum((n_used - 1 >= tile_end).astype(I32))
    te = jnp.where(j < n_used, te, te_last).astype(I32)
    tile_subs = jnp.where(j == (tile_end - ptiles)[te], first_subs[te], subs_per_tile)
    tile_subs = jnp.where(j < n_used, tile_subs, 0).astype(I32)
    src = _invert_positions(pos, n, n_tiles * tile)
    return pos.reshape(k, n), src, te, tile_subs, n_used.reshape(1).astype(I32)


def _positions_kernel(e_ref, tri_ref, low_ref, tab_ref, pos_ref, carry, *, n_experts):
    rows = e_ref.shape[0]

    @pl.when(pl.program_id(0) == 0)
    def _():
        carry[...] = jnp.zeros(carry.shape, F32)

    e = e_ref[...]
    masks = [e == k for k in range(n_experts)]
    stack = jnp.concatenate([m.astype(BF16) for m in masks], axis=0)
    cum = _dot(stack, tri_ref[...])
    tot = jnp.broadcast_to(cum[:, LANES - 1:LANES], cum.shape)
    before = _dot(low_ref[...], tot.astype(BF16))
    rank = jnp.zeros(e.shape, F32)
    off = jnp.zeros(e.shape, F32)
    head = jnp.zeros(e.shape, F32)
    skip = jnp.zeros(e.shape, F32)
    for k in range(n_experts):
        lo, hi = k * rows, (k + 1) * rows
        base = carry[k:k + 1, :]
        rank = jnp.where(masks[k], cum[lo:hi, :] + before[lo:hi, :] + base, rank)
        off = jnp.where(masks[k], tab_ref[0, k:k + 1, :], off)
        head = jnp.where(masks[k], tab_ref[1, k:k + 1, :], head)
        skip = jnp.where(masks[k], tab_ref[2, k:k + 1, :], skip)
        carry[k:k + 1, :] = base + before[hi - 1:hi, :] + tot[hi - 1:hi, :]
    rank = rank - 1.0
    pos_ref[...] = (off + rank + jnp.where(rank >= head, skip, 0.0)).astype(I32)


def _pair_positions(e, offs, head, skip):
    pairs = e.shape[0]
    n_experts = offs.shape[0]
    rows = ROUTE_ROWS
    assert pairs % (rows * LANES) == 0
    stacked = n_experts * rows
    tri = (jnp.arange(LANES)[:, None] <= jnp.arange(LANES)[None, :]).astype(BF16)
    r = jnp.arange(stacked)
    low = ((r[:, None] // rows == r[None, :] // rows) & (r[None, :] < r[:, None])).astype(BF16)
    tabs = jnp.broadcast_to(jnp.stack([offs, head, skip]).astype(F32)[:, :, None],
                            (3, n_experts, LANES))
    pos = pl.pallas_call(
        functools.partial(_positions_kernel, n_experts=n_experts),
        grid=(pairs // (rows * LANES),),
        in_specs=[pl.BlockSpec((rows, LANES), lambda i: (i, 0)),
                  _full_spec(tri), _full_spec(low), _full_spec(tabs)],
        out_specs=pl.BlockSpec((rows, LANES), lambda i: (i, 0)),
        out_shape=jax.ShapeDtypeStruct((pairs // LANES, LANES), I32),
        scratch_shapes=[pltpu.VMEM((n_experts, LANES), F32)],
        compiler_params=pltpu.CompilerParams(dimension_semantics=("arbitrary",)),
        name="pair_positions",
    )(e.reshape(pairs // LANES, LANES), tri, low, tabs)
    return pos.reshape(pairs)


def _invert_kernel(pos_ref, src_ref, zeros_ref, *, chunk, n):
    i = pl.program_id(0)

    @pl.when(i == 0)
    def _():
        zeros_ref[...] = jnp.zeros(zeros_ref.shape, I32)
        pltpu.sync_copy(zeros_ref, src_ref)

    first_tok = (i * chunk) % n

    def body(c, carry):
        for u in range(INVERT_UNROLL):
            s = c * INVERT_UNROLL + u
            dst = pos_ref[0, 0, s]
            src_ref[dst] = first_tok + s
        return carry
    lax.fori_loop(0, chunk // INVERT_UNROLL, body, 0)


def _invert_positions(pos, n, rows):
    pairs = pos.shape[0]
    chunk = INVERT_CHUNK
    assert pairs % chunk == 0 and n % chunk == 0 and rows % LANES == 0
    src = pl.pallas_call(
        functools.partial(_invert_kernel, chunk=chunk, n=n),
        grid=(pairs // chunk,),
        in_specs=[pl.BlockSpec((1, 1, chunk), lambda i: (i, 0, 0), memory_space=pltpu.SMEM)],
        out_specs=pl.BlockSpec(memory_space=pltpu.SMEM),
        out_shape=jax.ShapeDtypeStruct((rows,), I32),
        scratch_shapes=[pltpu.VMEM((rows,), I32)],
        compiler_params=pltpu.CompilerParams(dimension_semantics=("arbitrary",)),
        name="invert_positions",
    )(pos.reshape(pairs // chunk, 1, chunk))
    return src


def _layer_params(l, w_in_b, b_in, conv_a_w, conv_a_b, ln_a_g, ln_a_b, w_pool_bd, pool_scale,
                  conv_c_w, w_out_b, b_out, ln1_g, ln1_b, w_router, b_router):
    row = lambda a: a[l][None, :]
    return (w_in_b[l], row(b_in), conv_a_w[l], row(conv_a_b), row(ln_a_g), row(ln_a_b),
            w_pool_bd[l], row(pool_scale), conv_c_w[l], w_out_b[l], row(b_out), row(ln1_g),
            row(ln1_b), w_router[l], b_router[l])


def kernel(x_prompt, x_sample, state_conv_a, state_pool_b, state_conv_c, w_in, b_in, conv_a_w, conv_a_b, ln_a_g, ln_a_b, w_pool, pool_scale, conv_c_w, w_out, b_out, ln1_g, ln1_b, w_router_group, b_router_group, w_router_expert, b_router_expert, w_gate, w_up, w_down, ln2_g, ln2_b):
    depth = w_in.shape[0]
    bp, tp, d = x_prompt.shape
    bs, ts, _ = x_sample.shape
    ka, w_a = conv_a_w.shape[1:]
    kb, w_b = state_pool_b.shape[2:]
    kc, w_c = conv_c_w.shape[1:]
    n_groups = w_router_group.shape[-1]
    n_experts = w_router_expert.shape[-1]
    dims = (w_a, w_b, w_c, ka, kb, kc, n_groups, n_experts)
    alpha = float((2 * depth) ** 0.25)
    assert tp % PROMPT_TILE == 0 and PROMPT_TILE % ROW_CHUNK == 0 and (bs * ts) % ROW_CHUNK == 0
    assert n_groups <= SUBLANES and ROUTER_EXPERT_ROW + n_experts <= LANES
    assert d == SUBLANES * LANES

    w_in_b = w_in.astype(BF16)
    w_out_b = w_out.astype(BF16)
    n_pg, pg = w_pool.shape[1], w_pool.shape[2]
    eye = jnp.eye(n_pg, dtype=F32)
    w_pool_bd = (w_pool[:, :, :, None, :] * eye[None, :, None, :, None]).reshape(depth, n_pg * pg, n_pg * pg).astype(BF16)

    def router_lanes(group_part, expert_part):
        lead = group_part.shape[:-1]
        gap = jnp.zeros(lead + (ROUTER_EXPERT_ROW - ROUTER_GROUP_ROW - n_groups,), F32)
        tail = jnp.zeros(lead + (LANES - ROUTER_EXPERT_ROW - n_experts,), F32)
        return jnp.concatenate([group_part, gap, expert_part, tail], axis=-1)

    assert ROUTER_GROUP_ROW == 0
    wr = router_lanes(w_router_group, w_router_expert)
    wr_hi = wr.astype(BF16)
    wr_lo = (wr - wr_hi.astype(F32)).astype(BF16)
    w_router = jnp.concatenate([wr_hi, wr_lo], axis=-1)
    b_router = router_lanes(b_router_group, b_router_expert)[:, None, :]

    xp = x_prompt.reshape(bp * tp, d)
    xs = x_sample.transpose(1, 0, 2).reshape(ts * bs, d)
    new_states = [[] for _ in range(6)]
    for l in range(depth):
        params = _layer_params(l, w_in_b, b_in, conv_a_w, conv_a_b, ln_a_g, ln_a_b, w_pool_bd,
                               pool_scale, conv_c_w, w_out_b, b_out, ln1_g, ln1_b, w_router, b_router)
        g2, b2 = ln2_g[l][None, :], ln2_b[l][None, :]

        x1, eidx, wts, pa, pb, pc = _prompt_mixer(xp, bp, tp, bs * ts, params, dims, alpha)
        x1, seidx, swts, sa, sb, sc = _sample_mixer(
            x1, xs, state_conv_a[l].reshape(bs, -1), state_pool_b[l].reshape(bs, -1),
            state_conv_c[l].reshape(bs, -1), params, dims, alpha, bs, ts)
        eidx = jnp.concatenate([eidx.transpose(1, 0, 2).reshape(TOP_K, bp * tp), seidx], axis=1)
        wts = jnp.concatenate([wts.transpose(1, 0, 2).reshape(TOP_K, bp * tp), swts], axis=1)

        pos, src, te, subs, nu = _route_tables(eidx, n_experts, MOE_TILE, MOE_SUB)
        y = _moe(x1, src, te, subs, nu, w_gate, w_up, w_down, l)
        xp, xs = _combine(x1, y, pos, wts, g2, b2, alpha, bp * tp)

        for lst, val in zip(new_states, (pa, sa.reshape(bs, ka - 1, w_a), pb, sb.reshape(bs, kb, w_b),
                                         pc, sc.reshape(bs, kc - 1, w_c))):
            lst.append(val)

    y_prompt = xp.reshape(bp, tp, d)
    y_sample = xs.reshape(ts, bs, d).transpose(1, 0, 2)
    return (y_prompt, y_sample) + tuple(jnp.stack(s) for s in new_states)
```

```python
import functools

import jax
import jax.numpy as jnp
from jax import lax
from jax.experimental import pallas as pl
from jax.experimental.pallas import tpu as pltpu

F32 = jnp.float32
BF16 = jnp.bfloat16
I32 = jnp.int32

POOL_WINDOWS = (2, 4, 8, 16)
TOP_K = 2
LN_EPS = 1e-5
PAST_LEN = 16384

LANES = 128
SUBLANES = 8
VMEM_LIMIT_BYTES = 56 * 1024 * 1024

ROW_CHUNK = 128
PROMPT_TILE = 512
PROMPT_HIST_PITCH = 2
MOE_TILE = 1024
MOE_SUB = 256
MOE_GATHER_SLOTS = 3
COMBINE_TILE = 256
GATHER_UNROLL = 32
DMA_PRIORITIES = 2
INVERT_CHUNK = 4224
INVERT_UNROLL = 16
ROUTE_ROWS = 24
GATHER_PITCH = 9
ROUTER_GROUP_ROW = 0
ROUTER_EXPERT_ROW = 8


def _dot(a, b):
    return jnp.dot(a, b, preferred_element_type=F32)


def _layer_norm(x, g, b):
    mu = jnp.mean(x, axis=-1, keepdims=True)
    xc = x - mu
    var = jnp.mean(xc * xc, axis=-1, keepdims=True)
    return xc * lax.rsqrt(var + LN_EPS) * g + b


def _sigmoid(x):
    return 1.0 / (1.0 + jnp.exp(-x))


def _store_row_tiles(ref, val):
    rows, d = val.shape
    for j in range(d // LANES):
        ref[pl.ds(j, rows, stride=SUBLANES), :] = val[:, j * LANES:(j + 1) * LANES]


def _load_row_tiles(ref, rows, first=0, pitch=SUBLANES):
    return jnp.concatenate(
        [ref[pl.ds(first + j, rows, stride=pitch), :] for j in range(SUBLANES)], axis=-1)


def _hist_load(buf, start, n, pitch):
    parts = []
    for j in range(buf.shape[0]):
        if pitch == 1:
            parts.append(buf[j, pl.ds(start, n), :])
        else:
            parts.append(buf.at[j][pl.ds(pitch * start, n, stride=pitch), :])
    return parts[0] if len(parts) == 1 else jnp.concatenate(parts, axis=-1)


def _hist_store(buf, start, n, pitch, val):
    for j in range(buf.shape[0]):
        piece = val[:, j * LANES:(j + 1) * LANES]
        if pitch == 1:
            buf[j, pl.ds(start, n), :] = piece
        else:
            buf.at[j][pl.ds(pitch * start, n, stride=pitch), :] = piece


def _mixer_rows(x_ref, p, bufs, outs, *, rows, stride, pitch, new_a, new_b, new_c, pos_of_row,
                alpha, n_groups, n_experts):
    (w_in, b_in, wa, ba, lnag, lnab, wbd, pscale, wc, w_out, b_out, ln1g, ln1b,
     wr, br) = p
    proj, buf_a, buf_b, buf_c, dbuf, cat = bufs
    x1_ref, eidx_ref, wts_ref = outs
    ka = wa.shape[0]
    kc = wc.shape[0]
    w_a = wa.shape[1]
    w_b = pscale.shape[1]
    w_c = wc.shape[1]
    o_gate, o_ub, o_cbg, o_ccg, o_ch = w_a, 2 * w_a, 2 * w_a + w_b, 2 * w_a + w_b + w_c, 2 * w_a + w_b + 2 * w_c

    x = x_ref[...]
    proj[...] = _dot(x.astype(BF16), w_in[...]) + b_in[...]

    _hist_store(buf_a, new_a, rows, pitch, proj[:, 0:w_a] * _sigmoid(proj[:, o_gate:o_gate + w_a]))
    _hist_store(buf_b, new_b, rows, pitch, proj[:, o_ub:o_ub + w_b])
    _hist_store(buf_c, new_c, rows, pitch, proj[:, o_ccg:o_ccg + w_c] * proj[:, o_ch:o_ch + w_c])

    lane_b = lax.broadcasted_iota(I32, (ROW_CHUNK, w_b), 1)
    group_b = lane_b // (w_b // len(POOL_WINDOWS))
    win_b = jnp.zeros((ROW_CHUNK, w_b), I32)
    for g, w in enumerate(POOL_WINDOWS):
        win_b = jnp.where(group_b == g, w, win_b)

    def chunk(c, carry):
        r0 = c * ROW_CHUNK if isinstance(c, int) else pl.multiple_of(c * ROW_CHUNK, ROW_CHUNK)
        acc = jnp.zeros((ROW_CHUNK, w_a), F32) + ba[...]
        for k in range(ka):
            src = _hist_load(buf_a, r0 + (new_a - (ka - 1 - k) * stride), ROW_CHUNK, pitch)
            acc = acc + wa[k:k + 1, :] * src
        a = _layer_norm(acc, lnag[...], lnab[...])
        cat[pl.ds(r0, ROW_CHUNK), 0:w_a] = (a * _sigmoid(a)).astype(BF16)
        cur = _hist_load(buf_b, r0 + new_b, ROW_CHUNK, pitch)
        run = cur
        wsum = jnp.zeros((ROW_CHUNK, w_b), F32)
        for j in range(1, max(POOL_WINDOWS) + 1):
            if j in POOL_WINDOWS:
                wsum = jnp.where(win_b == j, run, wsum)
            if j < max(POOL_WINDOWS):
                run = run + _hist_load(buf_b, r0 + (new_b - j * stride), ROW_CHUNK, pitch)
        pos = pos_of_row(r0 + lax.broadcasted_iota(I32, (ROW_CHUNK, w_b), 0))
        cnt = jnp.minimum(pos + 1, win_b).astype(F32)
        dbuf[pl.ds(r0, ROW_CHUNK), :] = (wsum / cnt - cur).astype(BF16)
        accc = jnp.zeros((ROW_CHUNK, w_c), F32)
        for k in range(kc):
            src = _hist_load(buf_c, r0 + (new_c - (kc - 1 - k) * stride), ROW_CHUNK, pitch)
            accc = accc + wc[k:k + 1, :] * src
        cbg = proj[pl.ds(r0, ROW_CHUNK), o_cbg:o_cbg + w_c]
        cat[pl.ds(r0, ROW_CHUNK), w_a + w_b:w_a + w_b + w_c] = (cbg * accc).astype(BF16)
        return carry

    if pitch == 1:
        assert stride % SUBLANES == 0
        lax.fori_loop(0, rows // ROW_CHUNK, chunk, 0)
    else:
        for c in range(rows // ROW_CHUNK):
            chunk(c, 0)

    cat[:, w_a:w_a + w_b] = (_dot(dbuf[...], wbd[...]) * pscale[...]).astype(BF16)

    m = _dot(cat[...], w_out[...]) + b_out[...]
    x1 = _layer_norm(alpha * x + m, ln1g[...], ln1b[...])
    _store_row_tiles(x1_ref, x1)

    x1_hi = x1.astype(BF16)
    x1_lo = (x1 - x1_hi.astype(F32)).astype(BF16)
    p_hi = _dot(x1_hi, wr[...])
    p_lo = _dot(x1_lo, wr[...])
    logits = (p_hi[:, 0:LANES] + p_hi[:, LANES:2 * LANES]) + (p_lo[:, 0:LANES] + p_lo[:, LANES:2 * LANES])
    lt = jnp.transpose(logits + br[...])

    neg = jnp.float32(-jnp.inf)
    gl = lt[ROUTER_GROUP_ROW:ROUTER_GROUP_ROW + SUBLANES, :]
    grow = lax.broadcasted_iota(I32, gl.shape, 0)
    gvalid = grow < n_groups
    glm = jnp.where(gvalid, gl, neg)
    gmax = jnp.max(glm, axis=0, keepdims=True)
    gidx = jnp.min(jnp.where(glm == gmax, grow, SUBLANES), axis=0, keepdims=True)
    gsum = jnp.sum(jnp.where(gvalid, jnp.exp(gl - gmax), 0.0), axis=0, keepdims=True)
    g_p = 1.0 / gsum

    el = lt[ROUTER_EXPERT_ROW:ROUTER_EXPERT_ROW + n_experts, :]
    erow = lax.broadcasted_iota(I32, el.shape, 0)
    v = jnp.where(erow // (n_experts // n_groups) == gidx, el, neg)
    v1 = jnp.max(v, axis=0, keepdims=True)
    i1 = jnp.min(jnp.where(v == v1, erow, n_experts), axis=0, keepdims=True)
    vv = jnp.where(erow == i1, neg, v)
    v2 = jnp.max(vv, axis=0, keepdims=True)
    i2 = jnp.min(jnp.where(vv == v2, erow, n_experts), axis=0, keepdims=True)
    e2 = jnp.exp(v2 - v1)
    den = 1.0 + e2
    eidx_ref[0:1, :] = i1
    eidx_ref[1:2, :] = i2
    wts_ref[0:1, :] = (1.0 / den) * g_p
    wts_ref[1:2, :] = (e2 / den) * g_p


def _prompt_mixer_kernel(x_ref, *refs, n_params, n_seq, extra_tiles, tile, hist, pitch, alpha,
                         n_groups, n_experts):
    p = refs[:n_params]
    x1_ref, eidx_ref, wts_ref, sa_ref, sb_ref, sc_ref = refs[n_params:n_params + 6]
    bufs = refs[n_params + 6:]
    _, buf_a, buf_b, buf_c, _, _ = bufs
    ha, hb, hc = hist
    s = pl.program_id(0)
    t = pl.program_id(1)

    @pl.when(s < n_seq)
    def _():
        hists = ((buf_a, ha), (buf_b, hb), (buf_c, hc))

        @pl.when(t == 0)
        def _():
            for buf, h in hists:
                _hist_store(buf, 0, h, pitch, jnp.zeros((h, buf.shape[0] * LANES), F32))

        _mixer_rows(x_ref, p, bufs, (x1_ref, eidx_ref, wts_ref), rows=tile, stride=1, pitch=pitch,
                    new_a=ha, new_b=hb, new_c=hc,
                    pos_of_row=lambda r: r + t * tile,
                    alpha=alpha, n_groups=n_groups, n_experts=n_experts)

        tails = [_hist_load(buf, tile, h, pitch) for buf, h in hists]
        for (buf, h), tail in zip(hists, tails):
            _hist_store(buf, 0, h, pitch, tail)

        @pl.when(t == pl.num_programs(1) - 1)
        def _():
            for ref, (_, h), tail in zip((sa_ref, sb_ref, sc_ref), hists, tails):
                ref[...] = tail[h - ref.shape[0]:h, :]

    @pl.when(jnp.logical_and(s == n_seq, t < extra_tiles))
    def _():
        x1_ref[...] = jnp.zeros(x1_ref.shape, F32)


def _sample_mixer_kernel(x1_all_hbm, x_ref, sta_ref, stb_ref, stc_ref, *refs, n_params, batch, steps,
                         alpha, n_groups, n_experts):
    del x1_all_hbm
    p = refs[:n_params]
    x1_ref, eidx_ref, wts_ref, sa_ref, sb_ref, sc_ref = refs[n_params:n_params + 6]
    bufs = refs[n_params + 6:]
    _, buf_a, buf_b, buf_c, _, _ = bufs
    rows = batch * steps
    hists = []
    for buf, st_ref, out_ref in ((buf_a, sta_ref, sa_ref), (buf_b, stb_ref, sb_ref),
                                 (buf_c, stc_ref, sc_ref)):
        width = buf.shape[0] * LANES
        hists.append((buf, st_ref, out_ref, width, st_ref.shape[1] // width))

    for buf, st_ref, _, width, n in hists:
        for j in range(n):
            _hist_store(buf, j * batch, batch, 1, st_ref[:, j * width:(j + 1) * width])

    na, nb, nc = (h[4] for h in hists)
    _mixer_rows(x_ref, p, bufs, (x1_ref, eidx_ref, wts_ref), rows=rows, stride=batch, pitch=1,
                new_a=na * batch, new_b=nb * batch, new_c=nc * batch,
                pos_of_row=lambda r: PAST_LEN + r // batch,
                alpha=alpha, n_groups=n_groups, n_experts=n_experts)

    for buf, _, out_ref, width, n in hists:
        for j in range(n):
            out_ref[:, j * width:(j + 1) * width] = _hist_load(buf, (j + steps) * batch, batch, 1)


def _full_spec(a):
    nd = a.ndim
    return pl.BlockSpec(a.shape, lambda *_: (0,) * nd)


def _mixer_scratch(rows, ha, hb, hc, d_in, w_a, w_b, w_c, pitch):
    return [
        pltpu.VMEM((rows, d_in), F32),
        pltpu.VMEM((w_a // LANES, pitch * (ha + rows), LANES), F32),
        pltpu.VMEM((w_b // LANES, pitch * (hb + rows), LANES), F32),
        pltpu.VMEM((w_c // LANES, pitch * (hc + rows), LANES), F32),
        pltpu.VMEM((rows, w_b), BF16),
        pltpu.VMEM((rows, w_a + w_b + w_c), BF16),
    ]


def _round_up(n, m):
    return (n + m - 1) // m * m


def _prompt_mixer(x, b, t, extra_rows, params, dims, alpha):
    d = x.shape[1]
    w_a, w_b, w_c, ka, kb, kc, n_groups, n_experts = dims
    tile = PROMPT_TILE
    nt = t // tile
    ha, hb, hc = _round_up(ka - 1, SUBLANES), _round_up(kb, SUBLANES), _round_up(kc - 1, SUBLANES)
    d_in = params[0].shape[1]
    extra_tiles = extra_rows // tile
    assert extra_rows % tile == 0 and extra_tiles <= nt
    kern = functools.partial(_prompt_mixer_kernel, n_params=len(params), n_seq=b,
                             extra_tiles=extra_tiles, tile=tile, hist=(ha, hb, hc),
                             pitch=PROMPT_HIST_PITCH, alpha=alpha,
                             n_groups=n_groups, n_experts=n_experts)
    last_block = b * nt + extra_tiles - 1

    def seq_tile(i, j):
        return jnp.minimum(i, b - 1), jnp.where(i < b, j, nt - 1)

    def row_block(i, j):
        s, t_ = seq_tile(i, j)
        return s * nt + t_

    out_shape = (
        jax.ShapeDtypeStruct(((b * t + extra_rows) * SUBLANES, LANES), F32),
        jax.ShapeDtypeStruct((b, TOP_K, t), I32),
        jax.ShapeDtypeStruct((b, TOP_K, t), F32),
        jax.ShapeDtypeStruct((b, ka - 1, w_a), F32),
        jax.ShapeDtypeStruct((b, kb, w_b), F32),
        jax.ShapeDtypeStruct((b, kc - 1, w_c), F32),
    )
    out_specs = (
        pl.BlockSpec((tile * SUBLANES, LANES), lambda i, j: (jnp.minimum(i * nt + j, last_block), 0)),
        pl.BlockSpec((None, TOP_K, tile), lambda i, j: (seq_tile(i, j)[0], 0, seq_tile(i, j)[1])),
        pl.BlockSpec((None, TOP_K, tile), lambda i, j: (seq_tile(i, j)[0], 0, seq_tile(i, j)[1])),
        pl.BlockSpec((None, ka - 1, w_a), lambda i, j: (jnp.minimum(i, b - 1), 0, 0)),
        pl.BlockSpec((None, kb, w_b), lambda i, j: (jnp.minimum(i, b - 1), 0, 0)),
        pl.BlockSpec((None, kc - 1, w_c), lambda i, j: (jnp.minimum(i, b - 1), 0, 0)),
    )
    in_specs = [pl.BlockSpec((tile, d), lambda i, j: (row_block(i, j), 0))] + [_full_spec(a) for a in params]
    return pl.pallas_call(
        kern,
        grid=(b + 1, nt),
        in_specs=in_specs,
        out_specs=out_specs,
        out_shape=out_shape,
        scratch_shapes=_mixer_scratch(tile, ha, hb, hc, d_in, w_a, w_b, w_c, PROMPT_HIST_PITCH),
        compiler_params=pltpu.CompilerParams(
            dimension_semantics=("arbitrary", "arbitrary"),
            vmem_limit_bytes=VMEM_LIMIT_BYTES),
        name="prompt_mixer",
    )(x, *params)


def _sample_mixer(x1_all, x_tm, st_a, st_b, st_c, params, dims, alpha, batch, steps):
    rows, d = x_tm.shape
    w_a, w_b, w_c, ka, kb, kc, n_groups, n_experts = dims
    d_in = params[0].shape[1]
    first = x1_all.shape[0] // SUBLANES - rows
    assert first % rows == 0
    kern = functools.partial(_sample_mixer_kernel, n_params=len(params), batch=batch,
                             steps=steps, alpha=alpha, n_groups=n_groups, n_experts=n_experts)
    out_shape = (
        jax.ShapeDtypeStruct(x1_all.shape, F32),
        jax.ShapeDtypeStruct((TOP_K, rows), I32),
        jax.ShapeDtypeStruct((TOP_K, rows), F32),
        jax.ShapeDtypeStruct(st_a.shape, F32),
        jax.ShapeDtypeStruct(st_b.shape, F32),
        jax.ShapeDtypeStruct(st_c.shape, F32),
    )
    args = (x_tm, st_a, st_b, st_c) + tuple(params)
    out_specs = (pl.BlockSpec((rows * SUBLANES, LANES), lambda i: (first // rows, 0)),) + tuple(
        pl.BlockSpec(s.shape, lambda i, n=len(s.shape): (0,) * n) for s in out_shape[1:])
    return pl.pallas_call(
        kern,
        grid=(1,),
        in_specs=[pl.BlockSpec(memory_space=pl.ANY)] + [_full_spec(a) for a in args],
        out_specs=out_specs,
        out_shape=out_shape,
        input_output_aliases={0: 0},
        scratch_shapes=_mixer_scratch(rows, (ka - 1) * batch, kb * batch, (kc - 1) * batch,
                                      d_in, w_a, w_b, w_c, 1),
        compiler_params=pltpu.CompilerParams(
            dimension_semantics=("arbitrary",),
            vmem_limit_bytes=VMEM_LIMIT_BYTES),
        name="sample_mixer",
    )(x1_all, *args)


def _start_row_gather(src_hbm, idx_ref, dst_vmem, n, sem):
    def body(c, carry):
        for u in range(GATHER_UNROLL):
            s = c * GATHER_UNROLL + u
            row = idx_ref[0, 0, s]
            pltpu.make_async_copy(
                src_hbm.at[pl.ds(pl.multiple_of(row * SUBLANES, SUBLANES), SUBLANES)],
                dst_vmem.at[pl.ds(s * GATHER_PITCH, SUBLANES)],
                sem).start(priority=u % DMA_PRIORITIES)
        return carry
    lax.fori_loop(0, n // GATHER_UNROLL, body, 0)


def _wait_row_gather(src_hbm, dst_vmem, n, sem):
    pltpu.make_async_copy(src_hbm.at[pl.ds(0, n * SUBLANES)], dst_vmem.at[pl.ds(0, n * SUBLANES)],
                          sem).wait()


def _moe_kernel(te_ref, cnt_ref, nu_ref, src0_ref, src1_ref, src2_ref, x_hbm, wg_ref, wu_ref, wd_ref,
                y_ref, xbuf, wg_b, wu_b, wd_b, sem, *, tile, sub):
    i = pl.program_id(0)
    n_used = nu_ref[0]
    last = pl.num_programs(0) - 1
    slot = i % MOE_GATHER_SLOTS
    ahead = MOE_GATHER_SLOTS - 1
    far = jnp.minimum(i + ahead, last)
    far_slot = (i + ahead) % MOE_GATHER_SLOTS

    @pl.when(i == 0)
    def _():
        _start_row_gather(x_hbm, src0_ref, xbuf.at[0], cnt_ref[0] * sub, sem.at[0])

        @pl.when(1 < n_used)
        def _():
            _start_row_gather(x_hbm, src1_ref, xbuf.at[1], cnt_ref[1] * sub, sem.at[1])

    @pl.when(i + ahead < n_used)
    def _():
        _start_row_gather(x_hbm, src2_ref, xbuf.at[far_slot], cnt_ref[far] * sub, sem.at[far_slot])

    @pl.when(i < n_used)
    def _():
        @pl.when(jnp.logical_or(i == 0, te_ref[i] != te_ref[jnp.maximum(i - 1, 0)]))
        def _():
            wg_b[...] = wg_ref[...].astype(BF16)
            wu_b[...] = wu_ref[...].astype(BF16)
            wd_b[...] = wd_ref[...].astype(BF16)

        _wait_row_gather(x_hbm, xbuf.at[slot], cnt_ref[i] * sub, sem.at[slot])

        for c in range(1, tile // sub + 1):
            @pl.when(cnt_ref[i] == c)
            def _(rows=c * sub):
                xb = _load_row_tiles(xbuf.at[slot], rows, pitch=GATHER_PITCH).astype(BF16)
                hg = _dot(xb, wg_b[...])
                hu = _dot(xb, wu_b[...])
                h = hg * _sigmoid(hg) * hu
                _store_row_tiles(y_ref, _dot(h.astype(BF16), wd_b[...]))
                if rows < tile:
                    y_ref[pl.ds(rows * SUBLANES, (tile - rows) * SUBLANES), :] = jnp.zeros(
                        ((tile - rows) * SUBLANES, LANES), F32)

    @pl.when(i >= n_used)
    def _():
        y_ref[...] = jnp.zeros(y_ref.shape, F32)


def _moe(x1, src, tile_expert, tile_subs, n_used, w_gate, w_up, w_down, layer):
    d = w_gate.shape[-2]
    tile = MOE_TILE
    n_tiles = src.shape[0] // tile
    d_e = w_gate.shape[-1]
    grid_spec = pltpu.PrefetchScalarGridSpec(
        num_scalar_prefetch=3,
        grid=(n_tiles,),
        in_specs=[
            pl.BlockSpec((1, 1, tile), lambda i, te, cnt, nu: (i, 0, 0), memory_space=pltpu.SMEM),
            pl.BlockSpec((1, 1, tile), lambda i, te, cnt, nu: (jnp.minimum(i + 1, n_tiles - 1), 0, 0),
                         memory_space=pltpu.SMEM),
            pl.BlockSpec((1, 1, tile), lambda i, te, cnt, nu: (jnp.minimum(i + 2, n_tiles - 1), 0, 0),
                         memory_space=pltpu.SMEM),
            pl.BlockSpec(memory_space=pl.ANY),
            pl.BlockSpec((None, None, d, d_e), lambda i, te, cnt, nu: (layer, te[i], 0, 0)),
            pl.BlockSpec((None, None, d, d_e), lambda i, te, cnt, nu: (layer, te[i], 0, 0)),
            pl.BlockSpec((None, None, d_e, d), lambda i, te, cnt, nu: (layer, te[i], 0, 0)),
        ],
        out_specs=pl.BlockSpec((tile * SUBLANES, LANES), lambda i, te, cnt, nu: (i, 0)),
        scratch_shapes=[
            pltpu.VMEM((MOE_GATHER_SLOTS, tile * GATHER_PITCH, LANES), F32),
            pltpu.VMEM((d, d_e), BF16),
            pltpu.VMEM((d, d_e), BF16),
            pltpu.VMEM((d_e, d), BF16),
            pltpu.SemaphoreType.DMA((MOE_GATHER_SLOTS,)),
        ],
    )
    assert MOE_GATHER_SLOTS == 3 and n_tiles >= MOE_GATHER_SLOTS
    src3 = src.reshape(n_tiles, 1, tile)
    return pl.pallas_call(
        functools.partial(_moe_kernel, tile=tile, sub=MOE_SUB),
        grid_spec=grid_spec,
        out_shape=jax.ShapeDtypeStruct((n_tiles * tile * SUBLANES, LANES), F32),
        compiler_params=pltpu.CompilerParams(
            dimension_semantics=("arbitrary",),
            vmem_limit_bytes=VMEM_LIMIT_BYTES),
        name="moe",
    )(tile_expert, tile_subs, n_used, src3, src3, src3, x1, w_gate, w_up, w_down)


def _combine_kernel(pos0_ref, pos1_ref, x1_ref, w_ref, g_ref, b_ref, y_hbm, op_ref, os_ref, ybuf, sem,
                    *, tile, alpha, prompt_tiles):
    i = pl.program_id(0)
    rows = TOP_K * tile
    slot = i % 2

    @pl.when(i == 0)
    def _():
        _start_row_gather(y_hbm, pos0_ref, ybuf.at[0], rows, sem.at[0])

    @pl.when(i + 1 < pl.num_programs(0))
    def _():
        _start_row_gather(y_hbm, pos1_ref, ybuf.at[1 - slot], rows, sem.at[1 - slot])

    _wait_row_gather(y_hbm, ybuf.at[slot], rows, sem.at[slot])
    w = w_ref[...]
    y_a = _load_row_tiles(ybuf.at[slot], tile, pitch=GATHER_PITCH)
    y_b = _load_row_tiles(ybuf.at[slot], tile, first=tile * GATHER_PITCH, pitch=GATHER_PITCH)
    moe = w[:, 0:1] * y_a + w[:, 1:2] * y_b
    x1 = _load_row_tiles(x1_ref, tile)
    out = _layer_norm(alpha * x1 + moe, g_ref[...], b_ref[...])

    @pl.when(i < prompt_tiles)
    def _():
        op_ref[...] = out

    @pl.when(i >= prompt_tiles)
    def _():
        os_ref[...] = out


def _combine(x1, y, pos, wts, g, b, alpha, n_prompt):
    n, d = x1.shape[0] // SUBLANES, g.shape[-1]
    tile = COMBINE_TILE
    n_tiles = n // tile
    prompt_tiles = n_prompt // tile
    assert n % tile == 0 and n_prompt % tile == 0 and 0 < prompt_tiles < n_tiles
    pos_t = pos.reshape(TOP_K, n_tiles, tile).transpose(1, 0, 2).reshape(n_tiles, 1, TOP_K * tile)
    return pl.pallas_call(
        functools.partial(_combine_kernel, tile=tile, alpha=alpha, prompt_tiles=prompt_tiles),
        grid=(n_tiles,),
        in_specs=[
            pl.BlockSpec((1, 1, TOP_K * tile), lambda i: (i, 0, 0), memory_space=pltpu.SMEM),
            pl.BlockSpec((1, 1, TOP_K * tile), lambda i: (jnp.minimum(i + 1, n_tiles - 1), 0, 0),
                         memory_space=pltpu.SMEM),
            pl.BlockSpec((tile * SUBLANES, LANES), lambda i: (i, 0)),
            pl.BlockSpec((tile, TOP_K), lambda i: (i, 0)),
            _full_spec(g),
            _full_spec(b),
            pl.BlockSpec(memory_space=pl.ANY),
        ],
        out_specs=(
            pl.BlockSpec((tile, d), lambda i: (jnp.minimum(i, prompt_tiles - 1), 0)),
            pl.BlockSpec((tile, d), lambda i: (jnp.maximum(i - prompt_tiles, 0), 0)),
        ),
        out_shape=(jax.ShapeDtypeStruct((n_prompt, d), F32),
                   jax.ShapeDtypeStruct((n - n_prompt, d), F32)),
        scratch_shapes=[pltpu.VMEM((2, TOP_K * tile * GATHER_PITCH, LANES), F32),
                        pltpu.SemaphoreType.DMA((2,))],
        compiler_params=pltpu.CompilerParams(
            dimension_semantics=("arbitrary",),
            vmem_limit_bytes=VMEM_LIMIT_BYTES),
        name="combine",
    )(pos_t, pos_t, x1, wts.T, g, b, y)


def _route_tables(eidx, n_experts, tile, sub):
    k, n = eidx.shape
    pairs = k * n
    n_tiles = pairs // tile + n_experts
    e = eidx.reshape(pairs)
    counts = jnp.sum((e[:, None] == jnp.arange(n_experts, dtype=I32)[None, :]).astype(I32), axis=0)
    ptiles = (counts + tile - 1) // tile
    tile_end = jnp.cumsum(ptiles)
    offs = (tile_end - ptiles) * tile
    subs_per_tile = tile // sub
    first_subs = (counts + sub - 1) // sub - subs_per_tile * (ptiles - 1)
    head = first_subs * sub
    skip = tile - head
    pos = _pair_positions(e, offs, head, skip)
    n_used = tile_end[-1]
    j = jnp.arange(n_tiles, dtype=I32)
    te = jnp.sum((j[:, None] >= tile_end[None, :]).astype(I32), axis=1)
    te_last = jnp.sum((n_used - 1 >= tile_end).astype(I32))
    te = jnp.where(j < n_used, te, te_last).astype(I32)
    tile_subs = jnp.where(j == (tile_end - ptiles)[te], first_subs[te], subs_per_tile)
    tile_subs = jnp.where(j < n_used, tile_subs, 0).astype(I32)
    src = _invert_positions(pos, n, n_tiles * tile)
    return pos.reshape(k, n), src, te, tile_subs, n_used.reshape(1).astype(I32)


def _positions_kernel(e_ref, tri_ref, low_ref, tab_ref, pos_ref, carry, *, n_experts):
    rows = e_ref.shape[0]

    @pl.when(pl.program_id(0) == 0)
    def _():
        carry[...] = jnp.zeros(carry.shape, F32)

    e = e_ref[...]
    masks = [e == k for k in range(n_experts)]
    stack = jnp.concatenate([m.astype(BF16) for m in masks], axis=0)
    cum = _dot(stack, tri_ref[...])
    tot = jnp.broadcast_to(cum[:, LANES - 1:LANES], cum.shape)
    before = _dot(low_ref[...], tot.astype(BF16))
    rank = jnp.zeros(e.shape, F32)
    off = jnp.zeros(e.shape, F32)
    head = jnp.zeros(e.shape, F32)
    skip = jnp.zeros(e.shape, F32)
    for k in range(n_experts):
        lo, hi = k * rows, (k + 1) * rows
        base = carry[k:k + 1, :]
        rank = jnp.where(masks[k], cum[lo:hi, :] + before[lo:hi, :] + base, rank)
        off = jnp.where(masks[k], tab_ref[0, k:k + 1, :], off)
        head = jnp.where(masks[k], tab_ref[1, k:k + 1, :], head)
        skip = jnp.where(masks[k], tab_ref[2, k:k + 1, :], skip)
        carry[k:k + 1, :] = base + before[hi - 1:hi, :] + tot[hi - 1:hi, :]
    rank = rank - 1.0
    pos_ref[...] = (off + rank + jnp.where(rank >= head, skip, 0.0)).astype(I32)


def _pair_positions(e, offs, head, skip):
    pairs = e.shape[0]
    n_experts = offs.shape[0]
    rows = ROUTE_ROWS
    assert pairs % (rows * LANES) == 0
    stacked = n_experts * rows
    tri = (jnp.arange(LANES)[:, None] <= jnp.arange(LANES)[None, :]).astype(BF16)
    r = jnp.arange(stacked)
    low = ((r[:, None] // rows == r[None, :] // rows) & (r[None, :] < r[:, None])).astype(BF16)
    tabs = jnp.broadcast_to(jnp.stack([offs, head, skip]).astype(F32)[:, :, None],
                            (3, n_experts, LANES))
    pos = pl.pallas_call(
        functools.partial(_positions_kernel, n_experts=n_experts),
        grid=(pairs // (rows * LANES),),
        in_specs=[pl.BlockSpec((rows, LANES), lambda i: (i, 0)),
                  _full_spec(tri), _full_spec(low), _full_spec(tabs)],
        out_specs=pl.BlockSpec((rows, LANES), lambda i: (i, 0)),
        out_shape=jax.ShapeDtypeStruct((pairs // LANES, LANES), I32),
        scratch_shapes=[pltpu.VMEM((n_experts, LANES), F32)],
        compiler_params=pltpu.CompilerParams(dimension_semantics=("arbitrary",)),
        name="pair_positions",
    )(e.reshape(pairs // LANES, LANES), tri, low, tabs)
    return pos.reshape(pairs)


def _invert_kernel(pos_ref, src_ref, zeros_ref, *, chunk, n):
    i = pl.program_id(0)

    @pl.when(i == 0)
    def _():
        zeros_ref[...] = jnp.zeros(zeros_ref.shape, I32)
        pltpu.sync_copy(zeros_ref, src_ref)

    first_tok = (i * chunk) % n

    def body(c, carry):
        for u in range(INVERT_UNROLL):
            s = c * INVERT_UNROLL + u
            dst = pos_ref[0, 0, s]
            src_ref[dst] = first_tok + s
        return carry
    lax.fori_loop(0, chunk // INVERT_UNROLL, body, 0)


def _invert_positions(pos, n, rows):
    pairs = pos.shape[0]
    chunk = INVERT_CHUNK
    assert pairs % chunk == 0 and n % chunk == 0 and rows % LANES == 0
    src = pl.pallas_call(
        functools.partial(_invert_kernel, chunk=chunk, n=n),
        grid=(pairs // chunk,),
        in_specs=[pl.BlockSpec((1, 1, chunk), lambda i: (i, 0, 0), memory_space=pltpu.SMEM)],
        out_specs=pl.BlockSpec(memory_space=pltpu.SMEM),
        out_shape=jax.ShapeDtypeStruct((rows,), I32),
        scratch_shapes=[pltpu.VMEM((rows,), I32)],
        compiler_params=pltpu.CompilerParams(dimension_semantics=("arbitrary",)),
        name="invert_positions",
    )(pos.reshape(pairs // chunk, 1, chunk))
    return src


def _layer_params(l, w_in_b, b_in, conv_a_w, conv_a_b, ln_a_g, ln_a_b, w_pool_bd, pool_scale,
                  conv_c_w, w_out_b, b_out, ln1_g, ln1_b, w_router, b_router):
    row = lambda a: a[l][None, :]
    return (w_in_b[l], row(b_in), conv_a_w[l], row(conv_a_b), row(ln_a_g), row(ln_a_b),
            w_pool_bd[l], row(pool_scale), conv_c_w[l], w_out_b[l], row(b_out), row(ln1_g),
            row(ln1_b), w_router[l], b_router[l])


def kernel(x_prompt, x_sample, state_conv_a, state_pool_b, state_conv_c, w_in, b_in, conv_a_w, conv_a_b, ln_a_g, ln_a_b, w_pool, pool_scale, conv_c_w, w_out, b_out, ln1_g, ln1_b, w_router_group, b_router_group, w_router_expert, b_router_expert, w_gate, w_up, w_down, ln2_g, ln2_b):
    depth = w_in.shape[0]
    bp, tp, d = x_prompt.shape
    bs, ts, _ = x_sample.shape
    ka, w_a = conv_a_w.shape[1:]
    kb, w_b = state_pool_b.shape[2:]
    kc, w_c = conv_c_w.shape[1:]
    n_groups = w_router_group.shape[-1]
    n_experts = w_router_expert.shape[-1]
    dims = (w_a, w_b, w_c, ka, kb, kc, n_groups, n_experts)
    alpha = float((2 * depth) ** 0.25)
    assert tp % PROMPT_TILE == 0 and PROMPT_TILE % ROW_CHUNK == 0 and (bs * ts) % ROW_CHUNK == 0
    assert n_groups <= SUBLANES and ROUTER_EXPERT_ROW + n_experts <= LANES
    assert d == SUBLANES * LANES

    w_in_b = w_in.astype(BF16)
    w_out_b = w_out.astype(BF16)
    n_pg, pg = w_pool.shape[1], w_pool.shape[2]
    eye = jnp.eye(n_pg, dtype=F32)
    w_pool_bd = (w_pool[:, :, :, None, :] * eye[None, :, None, :, None]).reshape(depth, n_pg * pg, n_pg * pg).astype(BF16)

    def router_lanes(group_part, expert_part):
        lead = group_part.shape[:-1]
        gap = jnp.zeros(lead + (ROUTER_EXPERT_ROW - ROUTER_GROUP_ROW - n_groups,), F32)
        tail = jnp.zeros(lead + (LANES - ROUTER_EXPERT_ROW - n_experts,), F32)
        return jnp.concatenate([group_part, gap, expert_part, tail], axis=-1)

    assert ROUTER_GROUP_ROW == 0
    wr = router_lanes(w_router_group, w_router_expert)
    wr_hi = wr.astype(BF16)
    wr_lo = (wr - wr_hi.astype(F32)).astype(BF16)
    w_router = jnp.concatenate([wr_hi, wr_lo], axis=-1)
    b_router = router_lanes(b_router_group, b_router_expert)[:, None, :]

    xp = x_prompt.reshape(bp * tp, d)
    xs = x_sample.transpose(1, 0, 2).reshape(ts * bs, d)
    new_states = [[] for _ in range(6)]
    for l in range(depth):
        params = _layer_params(l, w_in_b, b_in, conv_a_w, conv_a_b, ln_a_g, ln_a_b, w_pool_bd,
                               pool_scale, conv_c_w, w_out_b, b_out, ln1_g, ln1_b, w_router, b_router)
        g2, b2 = ln2_g[l][None, :], ln2_b[l][None, :]

        x1, eidx, wts, pa, pb, pc = _prompt_mixer(xp, bp, tp, bs * ts, params, dims, alpha)
        x1, seidx, swts, sa, sb, sc = _sample_mixer(
            x1, xs, state_conv_a[l].reshape(bs, -1), state_pool_b[l].reshape(bs, -1),
            state_conv_c[l].reshape(bs, -1), params, dims, alpha, bs, ts)
        eidx = jnp.concatenate([eidx.transpose(1, 0, 2).reshape(TOP_K, bp * tp), seidx], axis=1)
        wts = jnp.concatenate([wts.transpose(1, 0, 2).reshape(TOP_K, bp * tp), swts], axis=1)

        pos, src, te, subs, nu = _route_tables(eidx, n_experts, MOE_TILE, MOE_SUB)
        y = _moe(x1, src, te, subs, nu, w_gate, w_up, w_down, l)
        xp, xs = _combine(x1, y, pos, wts, g2, b2, alpha, bp * tp)

        for lst, val in zip(new_states, (pa, sa.reshape(bs, ka - 1, w_a), pb, sb.reshape(bs, kb, w_b),
                                         pc, sc.reshape(bs, kc - 1, w_c))):
            lst.append(val)

    y_prompt = xp.reshape(bp, tp, d)
    y_sample = xs.reshape(ts, bs, d).transpose(1, 0, 2)
    return (y_prompt, y_sample) + tuple(jnp.stack(s) for s in new_states)
```

```python
import functools

import jax
import jax.numpy as jnp
from jax import lax
from jax.experimental import pallas as pl
from jax.experimental.pallas import tpu as pltpu

F32 = jnp.float32
BF16 = jnp.bfloat16
I32 = jnp.int32

POOL_WINDOWS = (2, 4, 8, 16)
TOP_K = 2
LN_EPS = 1e-5
PAST_LEN = 16384

LANES = 128
SUBLANES = 8
VMEM_LIMIT_BYTES = 56 * 1024 * 1024

ROW_CHUNK = 128
PROMPT_TILE = 512
PROMPT_HIST_PITCH = 2
MOE_TILE = 1024
MOE_SUB = 256
MOE_GATHER_SLOTS = 3
COMBINE_TILE = 256
COMBINE_GATHER_SLOTS = 3
GATHER_UNROLL = 32
DMA_PRIORITIES = 2
INVERT_CHUNK = 4224
INVERT_UNROLL = 16
ROUTE_ROWS = 24
GATHER_PITCH = 9
ROUTER_GROUP_ROW = 0
ROUTER_EXPERT_ROW = 8


def _dot(a, b):
    return jnp.dot(a, b, preferred_element_type=F32)


def _layer_norm(x, g, b):
    mu = jnp.mean(x, axis=-1, keepdims=True)
    xc = x - mu
    var = jnp.mean(xc * xc, axis=-1, keepdims=True)
    return xc * lax.rsqrt(var + LN_EPS) * g + b


def _sigmoid(x):
    return 1.0 / (1.0 + jnp.exp(-x))


def _store_row_tiles(ref, val):
    rows, d = val.shape
    for j in range(d // LANES):
        ref[pl.ds(j, rows, stride=SUBLANES), :] = val[:, j * LANES:(j + 1) * LANES]


def _load_row_tiles(ref, rows, first=0, pitch=SUBLANES):
    return jnp.concatenate(
        [ref[pl.ds(first + j, rows, stride=pitch), :] for j in range(SUBLANES)], axis=-1)


def _hist_load(buf, start, n, pitch):
    parts = []
    for j in range(buf.shape[0]):
        if pitch == 1:
            parts.append(buf[j, pl.ds(start, n), :])
        else:
            parts.append(buf.at[j][pl.ds(pitch * start, n, stride=pitch), :])
    return parts[0] if len(parts) == 1 else jnp.concatenate(parts, axis=-1)


def _hist_store(buf, start, n, pitch, val):
    for j in range(buf.shape[0]):
        piece = val[:, j * LANES:(j + 1) * LANES]
        if pitch == 1:
            buf[j, pl.ds(start, n), :] = piece
        else:
            buf.at[j][pl.ds(pitch * start, n, stride=pitch), :] = piece


def _mixer_rows(x_ref, p, bufs, outs, *, rows, stride, pitch, new_a, new_b, new_c, pos_of_row,
                alpha, n_groups, n_experts):
    (w_in, b_in, wa, ba, lnag, lnab, wbd, pscale, wc, w_out, b_out, ln1g, ln1b,
     wr, br) = p
    proj, buf_a, buf_b, buf_c, dbuf, cat = bufs
    x1_ref, eidx_ref, wts_ref = outs
    ka = wa.shape[0]
    kc = wc.shape[0]
    w_a = wa.shape[1]
    w_b = pscale.shape[1]
    w_c = wc.shape[1]
    o_gate, o_ub, o_cbg, o_ccg, o_ch = w_a, 2 * w_a, 2 * w_a + w_b, 2 * w_a + w_b + w_c, 2 * w_a + w_b + 2 * w_c

    x = x_ref[...]
    proj[...] = _dot(x.astype(BF16), w_in[...]) + b_in[...]

    _hist_store(buf_a, new_a, rows, pitch, proj[:, 0:w_a] * _sigmoid(proj[:, o_gate:o_gate + w_a]))
    _hist_store(buf_b, new_b, rows, pitch, proj[:, o_ub:o_ub + w_b])
    _hist_store(buf_c, new_c, rows, pitch, proj[:, o_ccg:o_ccg + w_c] * proj[:, o_ch:o_ch + w_c])

    lane_b = lax.broadcasted_iota(I32, (ROW_CHUNK, w_b), 1)
    group_b = lane_b // (w_b // len(POOL_WINDOWS))
    win_b = jnp.zeros((ROW_CHUNK, w_b), I32)
    for g, w in enumerate(POOL_WINDOWS):
        win_b = jnp.where(group_b == g, w, win_b)

    def chunk(c, carry):
        r0 = c * ROW_CHUNK if isinstance(c, int) else pl.multiple_of(c * ROW_CHUNK, ROW_CHUNK)
        acc = jnp.zeros((ROW_CHUNK, w_a), F32) + ba[...]
        for k in range(ka):
            src = _hist_load(buf_a, r0 + (new_a - (ka - 1 - k) * stride), ROW_CHUNK, pitch)
            acc = acc + wa[k:k + 1, :] * src
        a = _layer_norm(acc, lnag[...], lnab[...])
        cat[pl.ds(r0, ROW_CHUNK), 0:w_a] = (a * _sigmoid(a)).astype(BF16)
        cur = _hist_load(buf_b, r0 + new_b, ROW_CHUNK, pitch)
        run = cur
        wsum = jnp.zeros((ROW_CHUNK, w_b), F32)
        for j in range(1, max(POOL_WINDOWS) + 1):
            if j in POOL_WINDOWS:
                wsum = jnp.where(win_b == j, run, wsum)
            if j < max(POOL_WINDOWS):
                run = run + _hist_load(buf_b, r0 + (new_b - j * stride), ROW_CHUNK, pitch)
        pos = pos_of_row(r0 + lax.broadcasted_iota(I32, (ROW_CHUNK, w_b), 0))
        cnt = jnp.minimum(pos + 1, win_b).astype(F32)
        dbuf[pl.ds(r0, ROW_CHUNK), :] = (wsum / cnt - cur).astype(BF16)
        accc = jnp.zeros((ROW_CHUNK, w_c), F32)
        for k in range(kc):
            src = _hist_load(buf_c, r0 + (new_c - (kc - 1 - k) * stride), ROW_CHUNK, pitch)
            accc = accc + wc[k:k + 1, :] * src
        cbg = proj[pl.ds(r0, ROW_CHUNK), o_cbg:o_cbg + w_c]
        cat[pl.ds(r0, ROW_CHUNK), w_a + w_b:w_a + w_b + w_c] = (cbg * accc).astype(BF16)
        return carry

    if pitch == 1:
        assert stride % SUBLANES == 0
        lax.fori_loop(0, rows // ROW_CHUNK, chunk, 0)
    else:
        for c in range(rows // ROW_CHUNK):
            chunk(c, 0)

    cat[:, w_a:w_a + w_b] = (_dot(dbuf[...], wbd[...]) * pscale[...]).astype(BF16)

    m = _dot(cat[...], w_out[...]) + b_out[...]
    x1 = _layer_norm(alpha * x + m, ln1g[...], ln1b[...])
    _store_row_tiles(x1_ref, x1)

    x1_hi = x1.astype(BF16)
    x1_lo = (x1 - x1_hi.astype(F32)).astype(BF16)
    p_hi = _dot(x1_hi, wr[...])
    p_lo = _dot(x1_lo, wr[...])
    logits = (p_hi[:, 0:LANES] + p_hi[:, LANES:2 * LANES]) + (p_lo[:, 0:LANES] + p_lo[:, LANES:2 * LANES])
    lt = jnp.transpose(logits + br[...])

    neg = jnp.float32(-jnp.inf)
    gl = lt[ROUTER_GROUP_ROW:ROUTER_GROUP_ROW + SUBLANES, :]
    grow = lax.broadcasted_iota(I32, gl.shape, 0)
    gvalid = grow < n_groups
    glm = jnp.where(gvalid, gl, neg)
    gmax = jnp.max(glm, axis=0, keepdims=True)
    gidx = jnp.min(jnp.where(glm == gmax, grow, SUBLANES), axis=0, keepdims=True)
    gsum = jnp.sum(jnp.where(gvalid, jnp.exp(gl - gmax), 0.0), axis=0, keepdims=True)
    g_p = 1.0 / gsum

    el = lt[ROUTER_EXPERT_ROW:ROUTER_EXPERT_ROW + n_experts, :]
    erow = lax.broadcasted_iota(I32, el.shape, 0)
    v = jnp.where(erow // (n_experts // n_groups) == gidx, el, neg)
    v1 = jnp.max(v, axis=0, keepdims=True)
    i1 = jnp.min(jnp.where(v == v1, erow, n_experts), axis=0, keepdims=True)
    vv = jnp.where(erow == i1, neg, v)
    v2 = jnp.max(vv, axis=0, keepdims=True)
    i2 = jnp.min(jnp.where(vv == v2, erow, n_experts), axis=0, keepdims=True)
    e2 = jnp.exp(v2 - v1)
    den = 1.0 + e2
    eidx_ref[0:1, :] = i1
    eidx_ref[1:2, :] = i2
    wts_ref[0:1, :] = (1.0 / den) * g_p
    wts_ref[1:2, :] = (e2 / den) * g_p


def _prompt_mixer_kernel(x_ref, *refs, n_params, n_seq, extra_tiles, tile, hist, pitch, alpha,
                         n_groups, n_experts):
    p = refs[:n_params]
    x1_ref, eidx_ref, wts_ref, sa_ref, sb_ref, sc_ref = refs[n_params:n_params + 6]
    bufs = refs[n_params + 6:]
    _, buf_a, buf_b, buf_c, _, _ = bufs
    ha, hb, hc = hist
    s = pl.program_id(0)
    t = pl.program_id(1)

    @pl.when(s < n_seq)
    def _():
        hists = ((buf_a, ha), (buf_b, hb), (buf_c, hc))

        @pl.when(t == 0)
        def _():
            for buf, h in hists:
                _hist_store(buf, 0, h, pitch, jnp.zeros((h, buf.shape[0] * LANES), F32))

        _mixer_rows(x_ref, p, bufs, (x1_ref, eidx_ref, wts_ref), rows=tile, stride=1, pitch=pitch,
                    new_a=ha, new_b=hb, new_c=hc,
                    pos_of_row=lambda r: r + t * tile,
                    alpha=alpha, n_groups=n_groups, n_experts=n_experts)

        tails = [_hist_load(buf, tile, h, pitch) for buf, h in hists]
        for (buf, h), tail in zip(hists, tails):
            _hist_store(buf, 0, h, pitch, tail)

        @pl.when(t == pl.num_programs(1) - 1)
        def _():
            for ref, (_, h), tail in zip((sa_ref, sb_ref, sc_ref), hists, tails):
                ref[...] = tail[h - ref.shape[0]:h, :]

    @pl.when(jnp.logical_and(s == n_seq, t < extra_tiles))
    def _():
        x1_ref[...] = jnp.zeros(x1_ref.shape, F32)


def _sample_mixer_kernel(x1_all_hbm, x_ref, sta_ref, stb_ref, stc_ref, *refs, n_params, batch, steps,
                         alpha, n_groups, n_experts):
    del x1_all_hbm
    p = refs[:n_params]
    x1_ref, eidx_ref, wts_ref, sa_ref, sb_ref, sc_ref = refs[n_params:n_params + 6]
    bufs = refs[n_params + 6:]
    _, buf_a, buf_b, buf_c, _, _ = bufs
    rows = batch * steps
    hists = []
    for buf, st_ref, out_ref in ((buf_a, sta_ref, sa_ref), (buf_b, stb_ref, sb_ref),
                                 (buf_c, stc_ref, sc_ref)):
        width = buf.shape[0] * LANES
        hists.append((buf, st_ref, out_ref, width, st_ref.shape[1] // width))

    for buf, st_ref, _, width, n in hists:
        for j in range(n):
            _hist_store(buf, j * batch, batch, 1, st_ref[:, j * width:(j + 1) * width])

    na, nb, nc = (h[4] for h in hists)
    _mixer_rows(x_ref, p, bufs, (x1_ref, eidx_ref, wts_ref), rows=rows, stride=batch, pitch=1,
                new_a=na * batch, new_b=nb * batch, new_c=nc * batch,
                pos_of_row=lambda r: PAST_LEN + r // batch,
                alpha=alpha, n_groups=n_groups, n_experts=n_experts)

    for buf, _, out_ref, width, n in hists:
        for j in range(n):
            out_ref[:, j * width:(j + 1) * width] = _hist_load(buf, (j + steps) * batch, batch, 1)


def _full_spec(a):
    nd = a.ndim
    return pl.BlockSpec(a.shape, lambda *_: (0,) * nd)


def _mixer_scratch(rows, ha, hb, hc, d_in, w_a, w_b, w_c, pitch):
    return [
        pltpu.VMEM((rows, d_in), F32),
        pltpu.VMEM((w_a // LANES, pitch * (ha + rows), LANES), F32),
        pltpu.VMEM((w_b // LANES, pitch * (hb + rows), LANES), F32),
        pltpu.VMEM((w_c // LANES, pitch * (hc + rows), LANES), F32),
        pltpu.VMEM((rows, w_b), BF16),
        pltpu.VMEM((rows, w_a + w_b + w_c), BF16),
    ]


def _round_up(n, m):
    return (n + m - 1) // m * m


def _prompt_mixer(x, b, t, extra_rows, params, dims, alpha):
    d = x.shape[1]
    w_a, w_b, w_c, ka, kb, kc, n_groups, n_experts = dims
    tile = PROMPT_TILE
    nt = t // tile
    ha, hb, hc = _round_up(ka - 1, SUBLANES), _round_up(kb, SUBLANES), _round_up(kc - 1, SUBLANES)
    d_in = params[0].shape[1]
    extra_tiles = extra_rows // tile
    assert extra_rows % tile == 0 and extra_tiles <= nt
    kern = functools.partial(_prompt_mixer_kernel, n_params=len(params), n_seq=b,
                             extra_tiles=extra_tiles, tile=tile, hist=(ha, hb, hc),
                             pitch=PROMPT_HIST_PITCH, alpha=alpha,
                             n_groups=n_groups, n_experts=n_experts)
    last_block = b * nt + extra_tiles - 1

    def seq_tile(i, j):
        return jnp.minimum(i, b - 1), jnp.where(i < b, j, nt - 1)

    def row_block(i, j):
        s, t_ = seq_tile(i, j)
        return s * nt + t_

    out_shape = (
        jax.ShapeDtypeStruct(((b * t + extra_rows) * SUBLANES, LANES), F32),
        jax.ShapeDtypeStruct((b, TOP_K, t), I32),
        jax.ShapeDtypeStruct((b, TOP_K, t), F32),
        jax.ShapeDtypeStruct((b, ka - 1, w_a), F32),
        jax.ShapeDtypeStruct((b, kb, w_b), F32),
        jax.ShapeDtypeStruct((b, kc - 1, w_c), F32),
    )
    out_specs = (
        pl.BlockSpec((tile * SUBLANES, LANES), lambda i, j: (jnp.minimum(i * nt + j, last_block), 0)),
        pl.BlockSpec((None, TOP_K, tile), lambda i, j: (seq_tile(i, j)[0], 0, seq_tile(i, j)[1])),
        pl.BlockSpec((None, TOP_K, tile), lambda i, j: (seq_tile(i, j)[0], 0, seq_tile(i, j)[1])),
        pl.BlockSpec((None, ka - 1, w_a), lambda i, j: (jnp.minimum(i, b - 1), 0, 0)),
        pl.BlockSpec((None, kb, w_b), lambda i, j: (jnp.minimum(i, b - 1), 0, 0)),
        pl.BlockSpec((None, kc - 1, w_c), lambda i, j: (jnp.minimum(i, b - 1), 0, 0)),
    )
    in_specs = [pl.BlockSpec((tile, d), lambda i, j: (row_block(i, j), 0))] + [_full_spec(a) for a in params]
    return pl.pallas_call(
        kern,
        grid=(b + 1, nt),
        in_specs=in_specs,
        out_specs=out_specs,
        out_shape=out_shape,
        scratch_shapes=_mixer_scratch(tile, ha, hb, hc, d_in, w_a, w_b, w_c, PROMPT_HIST_PITCH),
        compiler_params=pltpu.CompilerParams(
            dimension_semantics=("arbitrary", "arbitrary"),
            vmem_limit_bytes=VMEM_LIMIT_BYTES),
        name="prompt_mixer",
    )(x, *params)


def _sample_mixer(x1_all, x_tm, st_a, st_b, st_c, params, dims, alpha, batch, steps):
    rows, d = x_tm.shape
    w_a, w_b, w_c, ka, kb, kc, n_groups, n_experts = dims
    d_in = params[0].shape[1]
    first = x1_all.shape[0] // SUBLANES - rows
    assert first % rows == 0
    kern = functools.partial(_sample_mixer_kernel, n_params=len(params), batch=batch,
                             steps=steps, alpha=alpha, n_groups=n_groups, n_experts=n_experts)
    out_shape = (
        jax.ShapeDtypeStruct(x1_all.shape, F32),
        jax.ShapeDtypeStruct((TOP_K, rows), I32),
        jax.ShapeDtypeStruct((TOP_K, rows), F32),
        jax.ShapeDtypeStruct(st_a.shape, F32),
        jax.ShapeDtypeStruct(st_b.shape, F32),
        jax.ShapeDtypeStruct(st_c.shape, F32),
    )
    args = (x_tm, st_a, st_b, st_c) + tuple(params)
    out_specs = (pl.BlockSpec((rows * SUBLANES, LANES), lambda i: (first // rows, 0)),) + tuple(
        pl.BlockSpec(s.shape, lambda i, n=len(s.shape): (0,) * n) for s in out_shape[1:])
    return pl.pallas_call(
        kern,
        grid=(1,),
        in_specs=[pl.BlockSpec(memory_space=pl.ANY)] + [_full_spec(a) for a in args],
        out_specs=out_specs,
        out_shape=out_shape,
        input_output_aliases={0: 0},
        scratch_shapes=_mixer_scratch(rows, (ka - 1) * batch, kb * batch, (kc - 1) * batch,
                                      d_in, w_a, w_b, w_c, 1),
        compiler_params=pltpu.CompilerParams(
            dimension_semantics=("arbitrary",),
            vmem_limit_bytes=VMEM_LIMIT_BYTES),
        name="sample_mixer",
    )(x1_all, *args)


def _start_row_gather(src_hbm, idx_ref, dst_vmem, n, sem):
    def body(c, carry):
        for u in range(GATHER_UNROLL):
            s = c * GATHER_UNROLL + u
            row = idx_ref[0, 0, s]
            pltpu.make_async_copy(
                src_hbm.at[pl.ds(pl.multiple_of(row * SUBLANES, SUBLANES), SUBLANES)],
                dst_vmem.at[pl.ds(s * GATHER_PITCH, SUBLANES)],
                sem).start(priority=u % DMA_PRIORITIES)
        return carry
    lax.fori_loop(0, n // GATHER_UNROLL, body, 0)


def _wait_row_gather(src_hbm, dst_vmem, n, sem):
    pltpu.make_async_copy(src_hbm.at[pl.ds(0, n * SUBLANES)], dst_vmem.at[pl.ds(0, n * SUBLANES)],
                          sem).wait()


def _moe_kernel(te_ref, cnt_ref, nu_ref, src0_ref, src1_ref, src2_ref, x_hbm, wg_ref, wu_ref, wd_ref,
                y_ref, xbuf, wg_b, wu_b, wd_b, sem, *, tile, sub):
    i = pl.program_id(0)
    n_used = nu_ref[0]
    last = pl.num_programs(0) - 1
    slot = i % MOE_GATHER_SLOTS
    ahead = MOE_GATHER_SLOTS - 1
    far = jnp.minimum(i + ahead, last)
    far_slot = (i + ahead) % MOE_GATHER_SLOTS

    @pl.when(i == 0)
    def _():
        _start_row_gather(x_hbm, src0_ref, xbuf.at[0], cnt_ref[0] * sub, sem.at[0])

        @pl.when(1 < n_used)
        def _():
            _start_row_gather(x_hbm, src1_ref, xbuf.at[1], cnt_ref[1] * sub, sem.at[1])

    @pl.when(i + ahead < n_used)
    def _():
        _start_row_gather(x_hbm, src2_ref, xbuf.at[far_slot], cnt_ref[far] * sub, sem.at[far_slot])

    @pl.when(i < n_used)
    def _():
        @pl.when(jnp.logical_or(i == 0, te_ref[i] != te_ref[jnp.maximum(i - 1, 0)]))
        def _():
            wg_b[...] = wg_ref[...].astype(BF16)
            wu_b[...] = wu_ref[...].astype(BF16)
            wd_b[...] = wd_ref[...].astype(BF16)

        _wait_row_gather(x_hbm, xbuf.at[slot], cnt_ref[i] * sub, sem.at[slot])

        for c in range(1, tile // sub + 1):
            @pl.when(cnt_ref[i] == c)
            def _(rows=c * sub):
                xb = _load_row_tiles(xbuf.at[slot], rows, pitch=GATHER_PITCH).astype(BF16)
                hg = _dot(xb, wg_b[...])
                hu = _dot(xb, wu_b[...])
                h = hg * _sigmoid(hg) * hu
                _store_row_tiles(y_ref, _dot(h.astype(BF16), wd_b[...]))
                if rows < tile:
                    y_ref[pl.ds(rows * SUBLANES, (tile - rows) * SUBLANES), :] = jnp.zeros(
                        ((tile - rows) * SUBLANES, LANES), F32)

    @pl.when(i >= n_used)
    def _():
        y_ref[...] = jnp.zeros(y_ref.shape, F32)


def _moe(x1, src, tile_expert, tile_subs, n_used, w_gate, w_up, w_down, layer):
    d = w_gate.shape[-2]
    tile = MOE_TILE
    n_tiles = src.shape[0] // tile
    d_e = w_gate.shape[-1]
    grid_spec = pltpu.PrefetchScalarGridSpec(
        num_scalar_prefetch=3,
        grid=(n_tiles,),
        in_specs=[
            pl.BlockSpec((1, 1, tile), lambda i, te, cnt, nu: (i, 0, 0), memory_space=pltpu.SMEM),
            pl.BlockSpec((1, 1, tile), lambda i, te, cnt, nu: (jnp.minimum(i + 1, n_tiles - 1), 0, 0),
                         memory_space=pltpu.SMEM),
            pl.BlockSpec((1, 1, tile), lambda i, te, cnt, nu: (jnp.minimum(i + 2, n_tiles - 1), 0, 0),
                         memory_space=pltpu.SMEM),
            pl.BlockSpec(memory_space=pl.ANY),
            pl.BlockSpec((None, None, d, d_e), lambda i, te, cnt, nu: (layer, te[i], 0, 0)),
            pl.BlockSpec((None, None, d, d_e), lambda i, te, cnt, nu: (layer, te[i], 0, 0)),
            pl.BlockSpec((None, None, d_e, d), lambda i, te, cnt, nu: (layer, te[i], 0, 0)),
        ],
        out_specs=pl.BlockSpec((tile * SUBLANES, LANES), lambda i, te, cnt, nu: (i, 0)),
        scratch_shapes=[
            pltpu.VMEM((MOE_GATHER_SLOTS, tile * GATHER_PITCH, LANES), F32),
            pltpu.VMEM((d, d_e), BF16),
            pltpu.VMEM((d, d_e), BF16),
            pltpu.VMEM((d_e, d), BF16),
            pltpu.SemaphoreType.DMA((MOE_GATHER_SLOTS,)),
        ],
    )
    assert MOE_GATHER_SLOTS == 3 and n_tiles >= MOE_GATHER_SLOTS
    src3 = src.reshape(n_tiles, 1, tile)
    return pl.pallas_call(
        functools.partial(_moe_kernel, tile=tile, sub=MOE_SUB),
        grid_spec=grid_spec,
        out_shape=jax.ShapeDtypeStruct((n_tiles * tile * SUBLANES, LANES), F32),
        compiler_params=pltpu.CompilerParams(
            dimension_semantics=("arbitrary",),
            vmem_limit_bytes=VMEM_LIMIT_BYTES),
        name="moe",
    )(tile_expert, tile_subs, n_used, src3, src3, src3, x1, w_gate, w_up, w_down)


def _combine_kernel(pos0_ref, pos1_ref, pos2_ref, x1_ref, w_ref, g_ref, b_ref, y_hbm, op_ref, os_ref,
                    ybuf, sem, *, tile, alpha, prompt_tiles):
    i = pl.program_id(0)
    n_steps = pl.num_programs(0)
    rows = TOP_K * tile
    slot = i % COMBINE_GATHER_SLOTS
    ahead = COMBINE_GATHER_SLOTS - 1
    far_slot = (i + ahead) % COMBINE_GATHER_SLOTS

    @pl.when(i == 0)
    def _():
        _start_row_gather(y_hbm, pos0_ref, ybuf.at[0], rows, sem.at[0])

        @pl.when(1 < n_steps)
        def _():
            _start_row_gather(y_hbm, pos1_ref, ybuf.at[1], rows, sem.at[1])

    @pl.when(i + ahead < n_steps)
    def _():
        _start_row_gather(y_hbm, pos2_ref, ybuf.at[far_slot], rows, sem.at[far_slot])

    _wait_row_gather(y_hbm, ybuf.at[slot], rows, sem.at[slot])
    w = w_ref[...]
    y_a = _load_row_tiles(ybuf.at[slot], tile, pitch=GATHER_PITCH)
    y_b = _load_row_tiles(ybuf.at[slot], tile, first=tile * GATHER_PITCH, pitch=GATHER_PITCH)
    moe = w[:, 0:1] * y_a + w[:, 1:2] * y_b
    x1 = _load_row_tiles(x1_ref, tile)
    out = _layer_norm(alpha * x1 + moe, g_ref[...], b_ref[...])

    @pl.when(i < prompt_tiles)
    def _():
        op_ref[...] = out

    @pl.when(i >= prompt_tiles)
    def _():
        os_ref[...] = out


def _combine(x1, y, pos, wts, g, b, alpha, n_prompt):
    n, d = x1.shape[0] // SUBLANES, g.shape[-1]
    tile = COMBINE_TILE
    n_tiles = n // tile
    prompt_tiles = n_prompt // tile
    assert n % tile == 0 and n_prompt % tile == 0 and 0 < prompt_tiles < n_tiles
    pos_t = pos.reshape(TOP_K, n_tiles, tile).transpose(1, 0, 2).reshape(n_tiles, 1, TOP_K * tile)
    return pl.pallas_call(
        functools.partial(_combine_kernel, tile=tile, alpha=alpha, prompt_tiles=prompt_tiles),
        grid=(n_tiles,),
        in_specs=[
            pl.BlockSpec((1, 1, TOP_K * tile), lambda i: (i, 0, 0), memory_space=pltpu.SMEM),
            pl.BlockSpec((1, 1, TOP_K * tile), lambda i: (jnp.minimum(i + 1, n_tiles - 1), 0, 0),
                         memory_space=pltpu.SMEM),
            pl.BlockSpec((1, 1, TOP_K * tile), lambda i: (jnp.minimum(i + 2, n_tiles - 1), 0, 0),
                         memory_space=pltpu.SMEM),
            pl.BlockSpec((tile * SUBLANES, LANES), lambda i: (i, 0)),
            pl.BlockSpec((tile, TOP_K), lambda i: (i, 0)),
            _full_spec(g),
            _full_spec(b),
            pl.BlockSpec(memory_space=pl.ANY),
        ],
        out_specs=(
            pl.BlockSpec((tile, d), lambda i: (jnp.minimum(i, prompt_tiles - 1), 0)),
            pl.BlockSpec((tile, d), lambda i: (jnp.maximum(i - prompt_tiles, 0), 0)),
        ),
        out_shape=(jax.ShapeDtypeStruct((n_prompt, d), F32),
                   jax.ShapeDtypeStruct((n - n_prompt, d), F32)),
        scratch_shapes=[pltpu.VMEM((COMBINE_GATHER_SLOTS, TOP_K * tile * GATHER_PITCH, LANES), F32),
                        pltpu.SemaphoreType.DMA((COMBINE_GATHER_SLOTS,))],
        compiler_params=pltpu.CompilerParams(
            dimension_semantics=("arbitrary",),
            vmem_limit_bytes=VMEM_LIMIT_BYTES),
        name="combine",
    )(pos_t, pos_t, pos_t, x1, wts.T, g, b, y)


def _route_tables(eidx, n_experts, tile, sub):
    k, n = eidx.shape
    pairs = k * n
    n_tiles = pairs // tile + n_experts
    e = eidx.reshape(pairs)
    counts = jnp.sum((e[:, None] == jnp.arange(n_experts, dtype=I32)[None, :]).astype(I32), axis=0)
    ptiles = (counts + tile - 1) // tile
    tile_end = jnp.cumsum(ptiles)
    offs = (tile_end - ptiles) * tile
    subs_per_tile = tile // sub
    first_subs = (counts + sub - 1) // sub - subs_per_tile * (ptiles - 1)
    head = first_subs * sub
    skip = tile - head
    pos = _pair_positions(e, offs, head, skip)
    n_used = tile_end[-1]
    j = jnp.arange(n_tiles, dtype=I32)
    te = jnp.sum((j[:, None] >= tile_end[None, :]).astype(I32), axis=1)
    te_last = jnp.sum((n_used - 1 >= tile_end).astype(I32))
    te = jnp.where(j < n_used, te, te_last).astype(I32)
    tile_subs = jnp.where(j == (tile_end - ptiles)[te], first_subs[te], subs_per_tile)
    tile_subs = jnp.where(j < n_used, tile_subs, 0).astype(I32)
    src = _invert_positions(pos, n, n_tiles * tile)
    return pos.reshape(k, n), src, te, tile_subs, n_used.reshape(1).astype(I32)


def _positions_kernel(e_ref, tri_ref, low_ref, tab_ref, pos_ref, carry, *, n_experts):
    rows = e_ref.shape[0]

    @pl.when(pl.program_id(0) == 0)
    def _():
        carry[...] = jnp.zeros(carry.shape, F32)

    e = e_ref[...]
    masks = [e == k for k in range(n_experts)]
    stack = jnp.concatenate([m.astype(BF16) for m in masks], axis=0)
    cum = _dot(stack, tri_ref[...])
    tot = jnp.broadcast_to(cum[:, LANES - 1:LANES], cum.shape)
    before = _dot(low_ref[...], tot.astype(BF16))
    rank = jnp.zeros(e.shape, F32)
    off = jnp.zeros(e.shape, F32)
    head = jnp.zeros(e.shape, F32)
    skip = jnp.zeros(e.shape, F32)
    for k in range(n_experts):
        lo, hi = k * rows, (k + 1) * rows
        base = carry[k:k + 1, :]
        rank = jnp.where(masks[k], cum[lo:hi, :] + before[lo:hi, :] + base, rank)
        off = jnp.where(masks[k], tab_ref[0, k:k + 1, :], off)
        head = jnp.where(masks[k], tab_ref[1, k:k + 1, :], head)
        skip = jnp.where(masks[k], tab_ref[2, k:k + 1, :], skip)
        carry[k:k + 1, :] = base + before[hi - 1:hi, :] + tot[hi - 1:hi, :]
    rank = rank - 1.0
    pos_ref[...] = (off + rank + jnp.where(rank >= head, skip, 0.0)).astype(I32)


def _pair_positions(e, offs, head, skip):
    pairs = e.shape[0]
    n_experts = offs.shape[0]
    rows = ROUTE_ROWS
    assert pairs % (rows * LANES) == 0
    stacked = n_experts * rows
    tri = (jnp.arange(LANES)[:, None] <= jnp.arange(LANES)[None, :]).astype(BF16)
    r = jnp.arange(stacked)
    low = ((r[:, None] // rows == r[None, :] // rows) & (r[None, :] < r[:, None])).astype(BF16)
    tabs = jnp.broadcast_to(jnp.stack([offs, head, skip]).astype(F32)[:, :, None],
                            (3, n_experts, LANES))
    pos = pl.pallas_call(
        functools.partial(_positions_kernel, n_experts=n_experts),
        grid=(pairs // (rows * LANES),),
        in_specs=[pl.BlockSpec((rows, LANES), lambda i: (i, 0)),
                  _full_spec(tri), _full_spec(low), _full_spec(tabs)],
        out_specs=pl.BlockSpec((rows, LANES), lambda i: (i, 0)),
        out_shape=jax.ShapeDtypeStruct((pairs // LANES, LANES), I32),
        scratch_shapes=[pltpu.VMEM((n_experts, LANES), F32)],
        compiler_params=pltpu.CompilerParams(dimension_semantics=("arbitrary",)),
        name="pair_positions",
    )(e.reshape(pairs // LANES, LANES), tri, low, tabs)
    return pos.reshape(pairs)


def _invert_kernel(pos_ref, src_ref, zeros_ref, *, chunk, n):
    i = pl.program_id(0)

    @pl.when(i == 0)
    def _():
        zeros_ref[...] = jnp.zeros(zeros_ref.shape, I32)
        pltpu.sync_copy(zeros_ref, src_ref)

    first_tok = (i * chunk) % n

    def body(c, carry):
        for u in range(INVERT_UNROLL):
            s = c * INVERT_UNROLL + u
            dst = pos_ref[0, 0, s]
            src_ref[dst] = first_tok + s
        return carry
    lax.fori_loop(0, chunk // INVERT_UNROLL, body, 0)


def _invert_positions(pos, n, rows):
    pairs = pos.shape[0]
    chunk = INVERT_CHUNK
    assert pairs % chunk == 0 and n % chunk == 0 and rows % LANES == 0
    src = pl.pallas_call(
        functools.partial(_invert_kernel, chunk=chunk, n=n),
        grid=(pairs // chunk,),
        in_specs=[pl.BlockSpec((1, 1, chunk), lambda i: (i, 0, 0), memory_space=pltpu.SMEM)],
        out_specs=pl.BlockSpec(memory_space=pltpu.SMEM),
        out_shape=jax.ShapeDtypeStruct((rows,), I32),
        scratch_shapes=[pltpu.VMEM((rows,), I32)],
        compiler_params=pltpu.CompilerParams(dimension_semantics=("arbitrary",)),
        name="invert_positions",
    )(pos.reshape(pairs // chunk, 1, chunk))
    return src


def _layer_params(l, w_in_b, b_in, conv_a_w, conv_a_b, ln_a_g, ln_a_b, w_pool_bd, pool_scale,
                  conv_c_w, w_out_b, b_out, ln1_g, ln1_b, w_router, b_router):
    row = lambda a: a[l][None, :]
    return (w_in_b[l], row(b_in), conv_a_w[l], row(conv_a_b), row(ln_a_g), row(ln_a_b),
            w_pool_bd[l], row(pool_scale), conv_c_w[l], w_out_b[l], row(b_out), row(ln1_g),
            row(ln1_b), w_router[l], b_router[l])


def kernel(x_prompt, x_sample, state_conv_a, state_pool_b, state_conv_c, w_in, b_in, conv_a_w, conv_a_b, ln_a_g, ln_a_b, w_pool, pool_scale, conv_c_w, w_out, b_out, ln1_g, ln1_b, w_router_group, b_router_group, w_router_expert, b_router_expert, w_gate, w_up, w_down, ln2_g, ln2_b):
    depth = w_in.shape[0]
    bp, tp, d = x_prompt.shape
    bs, ts, _ = x_sample.shape
    ka, w_a = conv_a_w.shape[1:]
    kb, w_b = state_pool_b.shape[2:]
    kc, w_c = conv_c_w.shape[1:]
    n_groups = w_router_group.shape[-1]
    n_experts = w_router_expert.shape[-1]
    dims = (w_a, w_b, w_c, ka, kb, kc, n_groups, n_experts)
    alpha = float((2 * depth) ** 0.25)
    assert tp % PROMPT_TILE == 0 and PROMPT_TILE % ROW_CHUNK == 0 and (bs * ts) % ROW_CHUNK == 0
    assert n_groups <= SUBLANES and ROUTER_EXPERT_ROW + n_experts <= LANES
    assert d == SUBLANES * LANES

    w_in_b = w_in.astype(BF16)
    w_out_b = w_out.astype(BF16)
    n_pg, pg = w_pool.shape[1], w_pool.shape[2]
    eye = jnp.eye(n_pg, dtype=F32)
    w_pool_bd = (w_pool[:, :, :, None, :] * eye[None, :, None, :, None]).reshape(depth, n_pg * pg, n_pg * pg).astype(BF16)

    def router_lanes(group_part, expert_part):
        lead = group_part.shape[:-1]
        gap = jnp.zeros(lead + (ROUTER_EXPERT_ROW - ROUTER_GROUP_ROW - n_groups,), F32)
        tail = jnp.zeros(lead + (LANES - ROUTER_EXPERT_ROW - n_experts,), F32)
        return jnp.concatenate([group_part, gap, expert_part, tail], axis=-1)

    assert ROUTER_GROUP_ROW == 0
    wr = router_lanes(w_router_group, w_router_expert)
    wr_hi = wr.astype(BF16)
    wr_lo = (wr - wr_hi.astype(F32)).astype(BF16)
    w_router = jnp.concatenate([wr_hi, wr_lo], axis=-1)
    b_router = router_lanes(b_router_group, b_router_expert)[:, None, :]

    xp = x_prompt.reshape(bp * tp, d)
    xs = x_sample.transpose(1, 0, 2).reshape(ts * bs, d)
    new_states = [[] for _ in range(6)]
    for l in range(depth):
        params = _layer_params(l, w_in_b, b_in, conv_a_w, conv_a_b, ln_a_g, ln_a_b, w_pool_bd,
                               pool_scale, conv_c_w, w_out_b, b_out, ln1_g, ln1_b, w_router, b_router)
        g2, b2 = ln2_g[l][None, :], ln2_b[l][None, :]

        x1, eidx, wts, pa, pb, pc = _prompt_mixer(xp, bp, tp, bs * ts, params, dims, alpha)
        x1, seidx, swts, sa, sb, sc = _sample_mixer(
            x1, xs, state_conv_a[l].reshape(bs, -1), state_pool_b[l].reshape(bs, -1),
            state_conv_c[l].reshape(bs, -1), params, dims, alpha, bs, ts)
        eidx = jnp.concatenate([eidx.transpose(1, 0, 2).reshape(TOP_K, bp * tp), seidx], axis=1)
        wts = jnp.concatenate([wts.transpose(1, 0, 2).reshape(TOP_K, bp * tp), swts], axis=1)

        pos, src, te, subs, nu = _route_tables(eidx, n_experts, MOE_TILE, MOE_SUB)
        y = _moe(x1, src, te, subs, nu, w_gate, w_up, w_down, l)
        xp, xs = _combine(x1, y, pos, wts, g2, b2, alpha, bp * tp)

        for lst, val in zip(new_states, (pa, sa.reshape(bs, ka - 1, w_a), pb, sb.reshape(bs, kb, w_b),
                                         pc, sc.reshape(bs, kc - 1, w_c))):
            lst.append(val)

    y_prompt = xp.reshape(bp, tp, d)
    y_sample = xs.reshape(ts, bs, d).transpose(1, 0, 2)
    return (y_prompt, y_sample) + tuple(jnp.stack(s) for s in new_states)
```
